```python
import math
import jax, jax.numpy as jnp
from jax import lax
import numpy as np

D_MODEL = 1024
BATCH = 8
SEQ = 2048
DEPTH = 2
DEC_BATCH = 128
DEC_SEQ = 8
PAST_LEN = 8192
PAGE_SIZE = 128

MLA_HEADS = 8
NOPE_DIM = 64
ROPE_DIM = 32
QK_DIM = NOPE_DIM + ROPE_DIM
V_DIM = 64
Q_RANK = 384
KV_RANK = 256
MLA_WIDTH = MLA_HEADS * V_DIM
ROPE_THETA = 10000.0
Q_BLOCK = 128
MLA_SCALE = QK_DIM ** -0.5

GLA_HEADS = 4
GLA_DK = 32
GLA_DV = 64
GLA_KDIM = GLA_HEADS * GLA_DK
GLA_WIDTH = GLA_HEADS * GLA_DV
GLA_GATE_RANK = 16
GLA_NORMALIZER = 16.0
GLA_CHUNK = 64

MEM_HEADS = 4
MEM_HEAD_DIM = 64
MEM_WIDTH = MEM_HEADS * MEM_HEAD_DIM
N_MEM = 256
MEM_SCALE = MEM_HEAD_DIM ** -0.5

D_MIX = MLA_WIDTH + GLA_WIDTH + MEM_WIDTH
SPLIT_SIZES = (Q_RANK, KV_RANK, ROPE_DIM, MLA_WIDTH,
               GLA_KDIM, GLA_KDIM, GLA_WIDTH, GLA_GATE_RANK, GLA_WIDTH,
               MEM_WIDTH, MEM_WIDTH)
D_IN = (Q_RANK + KV_RANK + ROPE_DIM + MLA_WIDTH + GLA_KDIM + GLA_KDIM + GLA_WIDTH
        + GLA_GATE_RANK + GLA_WIDTH + MEM_WIDTH + MEM_WIDTH)

EPS = 1e-6
NEG_INF = -1e30

kernel_name = 'hymba_mla_gla_memory_decoder_step'


def _rmsnorm(x, g):
    x32 = x.astype(jnp.float32)
    y = x32 * lax.rsqrt(jnp.mean(x32 * x32, axis=-1, keepdims=True) + EPS)
    return (y * g.astype(jnp.float32)).astype(x.dtype)


def _rope_tail(x, pos):
    half = ROPE_DIM // 2
    inv_freq = ROPE_THETA ** (-jnp.arange(half, dtype=jnp.float32) * (2.0 / ROPE_DIM))
    ang = pos.astype(jnp.float32)[:, None] * inv_freq[None, :]
    cos = jnp.cos(ang)[:, None, :]
    sin = jnp.sin(ang)[:, None, :]
    xr = x[..., -ROPE_DIM:].astype(jnp.float32)
    x1, x2 = xr[..., :half], xr[..., half:]
    rot = jnp.concatenate([x1 * cos - x2 * sin, x2 * cos + x1 * sin], axis=-1).astype(x.dtype)
    return jnp.concatenate([x[..., :-ROPE_DIM], rot], axis=-1)


def _split_in(z):
    out, o = [], 0
    for n in SPLIT_SIZES:
        out.append(z[..., o:o + n])
        o += n
    return out


def _branch_inputs(x, pos, g_pre, w_in, g_qa, w_uq, g_kva, g_mla_q, w_gk, b_gk, g_mem_q):
    b, L, _ = x.shape
    z = _rmsnorm(x, g_pre) @ w_in
    (cq, ckv, kpe, gate_mla, gq, gk, gv, gg, gate_gla, mq, gate_mem) = _split_in(z)
    q = jnp.einsum('blr,rhd->blhd', _rmsnorm(cq, g_qa), w_uq)
    q = _rope_tail(_rmsnorm(q, g_mla_q), pos)
    ckv = _rmsnorm(ckv, g_kva)
    gla_q = gq.reshape(b, L, GLA_HEADS, GLA_DK)
    gla_k = gk.reshape(b, L, GLA_HEADS, GLA_DK)
    gla_v = gv.reshape(b, L, GLA_HEADS, GLA_DV)
    gla_g = (jax.nn.log_sigmoid((gg @ w_gk + b_gk).astype(jnp.float32)) / GLA_NORMALIZER)
    gla_g = gla_g.reshape(b, L, GLA_HEADS, GLA_DK)
    mem_q = _rmsnorm(mq.reshape(b, L, MEM_HEADS, MEM_HEAD_DIM), g_mem_q)
    return (q, ckv, kpe, gate_mla, gla_q, gla_k, gla_v, gla_g, gate_gla, mem_q, gate_mem)


def _mla_keys(c, kpe, pos, w_uk, g_k):
    k_nope = jnp.einsum('bpr,rhd->bphd', c, w_uk)
    kpe_h = jnp.broadcast_to(kpe[:, :, None, :], k_nope.shape[:3] + (ROPE_DIM,)).astype(k_nope.dtype)
    k = _rmsnorm(jnp.concatenate([k_nope, kpe_h], axis=-1), g_k)
    return _rope_tail(k, pos)


def _mla_prompt(q, c, kpe, w_uk, g_k):
    b, s, h, d = q.shape
    pos = jnp.arange(s, dtype=jnp.int32)
    k = _mla_keys(c, kpe, pos, w_uk, g_k)
    qb = math.gcd(s, Q_BLOCK)
    nb = s // qb
    q_blocks = q.reshape(b, nb, qb, h, d).transpose(1, 0, 2, 3, 4)
    q_pos = pos.reshape(nb, qb)

    def block(inp):
        qi, pi = inp
        sc = jnp.einsum('bqhd,bkhd->bhqk', qi, k).astype(jnp.float32) * MLA_SCALE
        sc = jnp.where(pos[None, :] <= pi[:, None], sc, NEG_INF)
        p = jax.nn.softmax(sc, axis=-1).astype(c.dtype)
        return jnp.einsum('bhqk,bkr->bqhr', p, c)

    lat = lax.map(block, (q_blocks, q_pos))
    return lat.transpose(1, 0, 2, 3, 4).reshape(b, s, h, KV_RANK)


def _online_update(carry, s, c_blk):
    m, l, acc = carry
    m_new = jnp.maximum(m, s.max(axis=-1))
    alpha = jnp.exp(m - m_new)
    p = jnp.exp(s - m_new[..., None])
    l_new = l * alpha + p.sum(axis=-1)
    acc_new = acc * alpha[..., None] + jnp.einsum('bhtp,bpr->bhtr', p, c_blk.astype(jnp.float32))
    return (m_new, l_new, acc_new)


def _mla_sample(q, c_new, kpe_new, pos_new, cache_ckv, cache_kpe, page_table, layer, w_uk, g_k):
    db, t = q.shape[:2]
    n_pages = page_table.shape[1]
    page = cache_ckv.shape[2]
    init = (jnp.full((db, MLA_HEADS, t), NEG_INF, jnp.float32),
            jnp.zeros((db, MLA_HEADS, t), jnp.float32),
            jnp.zeros((db, MLA_HEADS, t, KV_RANK), jnp.float32))

    def page_step(carry, inp):
        phys, j = inp
        c_blk = cache_ckv[layer, phys]
        kpe_blk = cache_kpe[layer, phys]
        pos = j * page + jnp.arange(page, dtype=jnp.int32)
        k = _mla_keys(c_blk, kpe_blk, pos, w_uk, g_k)
        sc = jnp.einsum('bthd,bphd->bhtp', q, k).astype(jnp.float32) * MLA_SCALE
        return _online_update(carry, sc, c_blk), None

    carry, _ = lax.scan(page_step, init, (page_table.T, jnp.arange(n_pages, dtype=jnp.int32)))
    k_new = _mla_keys(c_new, kpe_new, pos_new, w_uk, g_k)
    sc = jnp.einsum('bthd,bphd->bhtp', q, k_new).astype(jnp.float32) * MLA_SCALE
    causal = jnp.tril(jnp.ones((t, t), dtype=bool))
    sc = jnp.where(causal[None, None], sc, NEG_INF)
    m, l, acc = _online_update(carry, sc, c_new)
    return (acc / l[..., None]).transpose(0, 2, 1, 3).astype(q.dtype)


def _gla_chunked(q, k, v, g, s0):
    bsz, L, h, dk = q.shape
    dv = v.shape[-1]
    c = math.gcd(L, GLA_CHUNK)
    n = L // c

    def chunks(a):
        return a.astype(jnp.float32).reshape(bsz, n, c, h, a.shape[-1]).transpose(1, 0, 2, 3, 4)

    causal = jnp.tril(jnp.ones((c, c), dtype=bool))[None, :, :, None, None]

    def step(state, inp):
        qc, kc, vc, gc = inp
        bcum = jnp.cumsum(gc, axis=1)
        o_inter = jnp.einsum('bthk,bhkv->bthv', qc * jnp.exp(bcum), state)
        diff = jnp.where(causal, bcum[:, :, None] - bcum[:, None, :], -jnp.inf)
        a = jnp.einsum('bthk,bshk,btshk->bhts', qc, kc, jnp.exp(diff))
        o_intra = jnp.einsum('bhts,bshv->bthv', a, vc)
        b_last = bcum[:, -1]
        state = (jnp.exp(b_last)[..., None] * state
                 + jnp.einsum('bshk,bshv->bhkv', kc * jnp.exp(b_last[:, None] - bcum), vc))
        return state, o_inter + o_intra

    state, o = lax.scan(step, s0.astype(jnp.float32),
                        (chunks(q * (dk ** -0.5)), chunks(k), chunks(v), chunks(g)))
    return o.transpose(1, 0, 2, 3, 4).reshape(bsz, L, h, dv), state


def _mem_kv(mem, g_mem, w_mem_kv, g_mem_k):
    b, n, _ = mem.shape
    kv = (_rmsnorm(mem, g_mem) @ w_mem_kv).reshape(b, n, 2, MEM_HEADS, MEM_HEAD_DIM)
    return _rmsnorm(kv[:, :, 0], g_mem_k), kv[:, :, 1]


def _mem_attn(q, mk, mv):
    sc = jnp.einsum('blhd,bnhd->bhln', q, mk).astype(jnp.float32) * MEM_SCALE
    p = jax.nn.softmax(sc, axis=-1).astype(mv.dtype)
    return jnp.einsum('bhln,bnhd->blhd', p, mv)


def _merge(x, mla_lat, w_uv, gate_mla, gla_o, g_gla_o, gate_gla, mem_o, gate_mem, w_out):
    b, L, _ = x.shape
    mla_o = jnp.einsum('blhr,rhd->blhd', mla_lat, w_uv).reshape(b, L, MLA_WIDTH)
    gla_o = _rmsnorm(gla_o, g_gla_o).astype(x.dtype).reshape(b, L, GLA_WIDTH)
    mem_o = mem_o.reshape(b, L, MEM_WIDTH)
    mix = jnp.concatenate([jax.nn.silu(gate_mla) * mla_o,
                           jax.nn.silu(gate_gla) * gla_o,
                           jax.nn.silu(gate_mem) * mem_o], axis=-1)
    return x + mix @ w_out


def _normal(k, shape, scale=1.0):
    return scale * jax.random.normal(k, shape, jnp.float32)


def _gain(k, shape):
    return 1.0 + 0.1 * jax.random.normal(k, shape, jnp.float32)


def setup_inputs(seed: int = 0) -> dict:
    key = jax.random.key(seed)
    ks = jax.random.split(key, 32)
    n_pages = PAST_LEN // PAGE_SIZE
    n_used = DEC_BATCH * n_pages
    n_pool = n_used + n_used // 4
    page_table = jax.random.permutation(ks[5], n_pool)[:n_used].reshape(DEC_BATCH, n_pages).astype(jnp.int32)
    return {
        'x_prompt': _normal(ks[0], (BATCH, SEQ, D_MODEL)),
        'x_sample': _normal(ks[1], (DEC_BATCH, DEC_SEQ, D_MODEL)),
        'mem_prompt': _normal(ks[2], (BATCH, N_MEM, D_MODEL)),
        'cache_ckv': _normal(ks[3], (DEPTH, n_pool, PAGE_SIZE, KV_RANK)),
        'cache_kpe': _normal(ks[4], (DEPTH, n_pool, PAGE_SIZE, ROPE_DIM)),
        'page_table': page_table,
        'state_gla': _normal(ks[6], (DEPTH, DEC_BATCH, GLA_HEADS, GLA_DK, GLA_DV)),
        'cache_mem_k': _normal(ks[7], (DEPTH, DEC_BATCH, N_MEM, MEM_HEADS, MEM_HEAD_DIM)),
        'cache_mem_v': _normal(ks[8], (DEPTH, DEC_BATCH, N_MEM, MEM_HEADS, MEM_HEAD_DIM)),
        'g_pre': _gain(ks[9], (DEPTH, D_MODEL)),
        'w_in': _normal(ks[10], (DEPTH, D_MODEL, D_IN), D_MODEL ** -0.5),
        'g_qa': _gain(ks[11], (DEPTH, Q_RANK)),
        'w_uq': _normal(ks[12], (DEPTH, Q_RANK, MLA_HEADS, QK_DIM), Q_RANK ** -0.5),
        'g_kva': _gain(ks[13], (DEPTH, KV_RANK)),
        'w_uk': _normal(ks[14], (DEPTH, KV_RANK, MLA_HEADS, NOPE_DIM), KV_RANK ** -0.5),
        'w_uv': _normal(ks[15], (DEPTH, KV_RANK, MLA_HEADS, V_DIM), KV_RANK ** -0.5),
        'g_mla_q': _gain(ks[16], (DEPTH, QK_DIM)),
        'g_mla_k': _gain(ks[17], (DEPTH, QK_DIM)),
        'w_gk': _normal(ks[18], (DEPTH, GLA_GATE_RANK, GLA_KDIM), GLA_GATE_RANK ** -0.5),
        'b_gk': _normal(ks[19], (DEPTH, GLA_KDIM), 0.1),
        'g_gla_o': _gain(ks[20], (DEPTH, GLA_DV)),
        'g_mem': _gain(ks[21], (DEPTH, D_MODEL)),
        'w_mem_kv': _normal(ks[22], (DEPTH, D_MODEL, 2 * MEM_WIDTH), D_MODEL ** -0.5),
        'g_mem_q': _gain(ks[23], (DEPTH, MEM_HEAD_DIM)),
        'g_mem_k': _gain(ks[24], (DEPTH, MEM_HEAD_DIM)),
        'w_out': _normal(ks[25], (DEPTH, D_MIX, D_MODEL), D_MIX ** -0.5),
    }


def reference(x_prompt, x_sample, mem_prompt, cache_ckv, cache_kpe, page_table, state_gla,
              cache_mem_k, cache_mem_v, g_pre, w_in, g_qa, w_uq, g_kva, w_uk, w_uv, g_mla_q,
              g_mla_k, w_gk, b_gk, g_gla_o, g_mem, w_mem_kv, g_mem_q, g_mem_k, w_out):
    pos_p = jnp.arange(x_prompt.shape[1], dtype=jnp.int32)
    pos_s = PAST_LEN + jnp.arange(x_sample.shape[1], dtype=jnp.int32)
    xp, xs = x_prompt, x_sample
    ckv_p, kpe_p, gla_p, mk_p, mv_p = [], [], [], [], []
    ckv_s, kpe_s, gla_s = [], [], []
    for l in range(DEPTH):
        shared = (g_pre[l], w_in[l], g_qa[l], w_uq[l], g_kva[l], g_mla_q[l], w_gk[l], b_gk[l], g_mem_q[l])
        (q, ckv, kpe, gate_mla, gq, gk, gv, gg, gate_gla, mq, gate_mem) = _branch_inputs(xp, pos_p, *shared)
        lat = _mla_prompt(q, ckv, kpe, w_uk[l], g_mla_k[l])
        s0 = jnp.zeros((xp.shape[0], GLA_HEADS, GLA_DK, GLA_DV), jnp.float32)
        gla_o, s_fin = _gla_chunked(gq, gk, gv, gg, s0)
        mk, mv = _mem_kv(mem_prompt, g_mem[l], w_mem_kv[l], g_mem_k[l])
        mem_o = _mem_attn(mq, mk, mv)
        xp = _merge(xp, lat, w_uv[l], gate_mla, gla_o, g_gla_o[l], gate_gla, mem_o, gate_mem, w_out[l])
        ckv_p.append(ckv)
        kpe_p.append(kpe)
        gla_p.append(s_fin.astype(xp.dtype))
        mk_p.append(mk)
        mv_p.append(mv)
        (q, ckv, kpe, gate_mla, gq, gk, gv, gg, gate_gla, mq, gate_mem) = _branch_inputs(xs, pos_s, *shared)
        lat = _mla_sample(q, ckv, kpe, pos_s, cache_ckv, cache_kpe, page_table, l, w_uk[l], g_mla_k[l])
        gla_o, s_fin = _gla_chunked(gq, gk, gv, gg, state_gla[l])
        mem_o = _mem_attn(mq, cache_mem_k[l], cache_mem_v[l])
        xs = _merge(xs, lat, w_uv[l], gate_mla, gla_o, g_gla_o[l], gate_gla, mem_o, gate_mem, w_out[l])
        ckv_s.append(ckv)
        kpe_s.append(kpe)
        gla_s.append(s_fin.astype(state_gla.dtype))
    return (xp, xs, jnp.stack(ckv_p), jnp.stack(kpe_p), jnp.stack(gla_p), jnp.stack(mk_p),
            jnp.stack(mv_p), jnp.stack(ckv_s), jnp.stack(kpe_s), jnp.stack(gla_s))
```

```python
import functools

import jax
import jax.numpy as jnp
from jax import lax
from jax.experimental import pallas as pl
from jax.experimental.pallas import tpu as pltpu

f32, bf16 = jnp.float32, jnp.bfloat16

D_MODEL = 1024
PAGE_SIZE = 128
MLA_HEADS = 8
NOPE_DIM = 64
ROPE_DIM = 32
HALF_ROPE = ROPE_DIM // 2
QK_DIM = NOPE_DIM + ROPE_DIM
V_DIM = 64
Q_RANK = 384
KV_RANK = 256
MLA_WIDTH = MLA_HEADS * V_DIM
ROPE_THETA = 10000.0
MLA_SCALE = QK_DIM ** -0.5
GLA_HEADS = 4
GLA_DK = 32
GLA_DV = 64
GLA_KDIM = GLA_HEADS * GLA_DK
GLA_WIDTH = GLA_HEADS * GLA_DV
GLA_GATE_RANK = 16
GLA_NORMALIZER = 16.0
MEM_HEADS = 4
MEM_HEAD_DIM = 64
MEM_WIDTH = MEM_HEADS * MEM_HEAD_DIM
MEM_SCALE = MEM_HEAD_DIM ** -0.5
D_MIX = MLA_WIDTH + GLA_WIDTH + MEM_WIDTH
EPS = 1e-6
NEG_INF = -1e30

LANES = 128
HEAD_PAD = LANES
QK_PACKED = MLA_HEADS * HEAD_PAD

OFF_CQ = 0
OFF_CKV = OFF_CQ + Q_RANK
OFF_GATE_MLA = OFF_CKV + KV_RANK
OFF_GQ = OFF_GATE_MLA + MLA_WIDTH
OFF_GK = OFF_GQ + GLA_KDIM
OFF_GV = OFF_GK + GLA_KDIM
OFF_GATE_GLA = OFF_GV + GLA_WIDTH
OFF_MQ = OFF_GATE_GLA + GLA_WIDTH
OFF_GATE_MEM = OFF_MQ + MEM_WIDTH
OFF_MISC = OFF_GATE_MEM + MEM_WIDTH
D_IN_PACKED = OFF_MISC + LANES
KPE_LANE = NOPE_DIM

VMEM_LIMIT = 56 * 1024 * 1024


def _cparams(*sem):
    return pltpu.CompilerParams(dimension_semantics=sem, vmem_limit_bytes=VMEM_LIMIT)


def _nt(a, b):
    return lax.dot_general(a, b, (((1,), (1,)), ((), ())), preferred_element_type=f32)


def _tn(a, b):
    return lax.dot_general(a, b, (((0,), (0,)), ((), ())), preferred_element_type=f32)


def _mm(a, b):
    return jnp.dot(a, b, preferred_element_type=f32)


def _rms_rows(x, g):
    return x * lax.rsqrt(jnp.mean(x * x, axis=-1, keepdims=True) + EPS) * g


def _half_head_rms(blk, g, lane):
    sq = blk * blk
    lo = lane < 64
    ss_lo = jnp.sum(jnp.where(lo, sq, 0.0), axis=-1, keepdims=True)
    ss_hi = jnp.sum(jnp.where(lo, 0.0, sq), axis=-1, keepdims=True)
    ss = jnp.where(lo, ss_lo, ss_hi)
    return blk * lax.rsqrt(ss * (1.0 / 64.0) + EPS) * g


def _rope_packed(x, cos, sin_up, sin_dn):
    return x * cos + pltpu.roll(x, HALF_ROPE, 1) * sin_up + pltpu.roll(x, LANES - HALF_ROPE, 1) * sin_dn


def _inproj_body(x_ref, gpre_ref, win_ref, gqa_ref, wuq_ref, gkva_ref, wuk_ref, gq_ref, gk_ref,
                 cos_ref, sup_ref, sdn_ref, wgk_ref, bgk_ref, gmq_ref,
                 q_out, k_out, ckv_out, kpe_out, gates_out, glaq_out, glak_out, glav_out, glag_out, memq_out):
    x = x_ref[...]
    xb = _rms_rows(x, gpre_ref[...]).astype(bf16)

    def seg(off, n):
        return _mm(xb, win_ref[:, off:off + n])

    cos, sup, sdn = cos_ref[...], sup_ref[...], sdn_ref[...]
    lane = lax.broadcasted_iota(jnp.int32, (1, LANES), 1)

    cq = _rms_rows(seg(OFF_CQ, Q_RANK), gqa_ref[...]).astype(bf16)
    qf = _mm(cq, wuq_ref[...])
    gq = gq_ref[...]
    for h in range(MLA_HEADS):
        qh = qf[:, h * HEAD_PAD:(h + 1) * HEAD_PAD]
        ss = jnp.sum(qh * qh, axis=-1, keepdims=True)
        qn = qh * lax.rsqrt(ss * (1.0 / QK_DIM) + EPS) * gq
        q_out[:, h * HEAD_PAD:(h + 1) * HEAD_PAD] = (_rope_packed(qn, cos, sup, sdn) * MLA_SCALE).astype(bf16)

    c = _rms_rows(seg(OFF_CKV, KV_RANK), gkva_ref[...])
    ckv_out[...] = c
    misc = seg(OFF_MISC, LANES)
    kpe_wide = jnp.where((lane >= KPE_LANE) & (lane < KPE_LANE + ROPE_DIM), misc, 0.0)
    kpe_out[...] = misc[:, KPE_LANE:KPE_LANE + ROPE_DIM]
    kf = _mm(c.astype(bf16), wuk_ref[...])
    gk = gk_ref[...]
    for h in range(MLA_HEADS):
        kh = kf[:, h * HEAD_PAD:(h + 1) * HEAD_PAD] + kpe_wide
        ss = jnp.sum(kh * kh, axis=-1, keepdims=True)
        kn = kh * lax.rsqrt(ss * (1.0 / QK_DIM) + EPS) * gk
        k_out[:, h * HEAD_PAD:(h + 1) * HEAD_PAD] = _rope_packed(kn, cos, sup, sdn).astype(bf16)

    g1 = seg(OFF_GATE_MLA, MLA_WIDTH)
    gates_out[:, 0:MLA_WIDTH] = g1 * jax.nn.sigmoid(g1)
    g2 = seg(OFF_GATE_GLA, GLA_WIDTH)
    gates_out[:, MLA_WIDTH:MLA_WIDTH + GLA_WIDTH] = g2 * jax.nn.sigmoid(g2)
    g3 = seg(OFF_GATE_MEM, MEM_WIDTH)
    gates_out[:, MLA_WIDTH + GLA_WIDTH:D_MIX] = g3 * jax.nn.sigmoid(g3)

    glaq_out[...] = seg(OFF_GQ, GLA_KDIM) * (GLA_DK ** -0.5)
    glak_out[...] = seg(OFF_GK, GLA_KDIM)
    glav_out[...] = seg(OFF_GV, GLA_WIDTH)
    gl = _mm(misc.astype(bf16), wgk_ref[...]) + bgk_ref[...]
    glag_out[...] = jax.nn.log_sigmoid(gl) * (1.0 / GLA_NORMALIZER)

    mq = seg(OFF_MQ, MEM_WIDTH)
    gmq = gmq_ref[...]
    for j in range(MEM_WIDTH // LANES):
        blk = _half_head_rms(mq[:, j * LANES:(j + 1) * LANES], gmq, lane)
        memq_out[:, j * LANES:(j + 1) * LANES] = (blk * MEM_SCALE).astype(bf16)


def _inproj(x2, tabs, w, tm):
    T = x2.shape[0]
    nt = tabs[0].shape[0] // tm
    row = lambda i: (i, 0)
    fix = lambda i: (0, 0)
    tab = lambda i: (i % nt, 0)

    def full(a):
        return pl.BlockSpec(a.shape, fix)

    consts1 = (w['g_pre'], w['w_in'], w['g_qa'], w['w_uq'], w['g_kva'], w['w_uk'], w['g_q'], w['g_k'])
    consts2 = (w['w_gk'], w['b_gk'], w['g_mem_q'])
    outs = [(QK_PACKED, bf16), (QK_PACKED, bf16), (KV_RANK, f32), (ROPE_DIM, f32), (D_MIX, f32),
            (GLA_KDIM, f32), (GLA_KDIM, f32), (GLA_WIDTH, f32), (GLA_KDIM, f32), (MEM_WIDTH, bf16)]
    return pl.pallas_call(
        _inproj_body,
        grid=(T // tm,),
        in_specs=[pl.BlockSpec((tm, D_MODEL), row)] + [full(a) for a in consts1]
        + [pl.BlockSpec((tm, LANES), tab)] * 3 + [full(a) for a in consts2],
        out_specs=[pl.BlockSpec((tm, n), row) for n, _ in outs],
        out_shape=[jax.ShapeDtypeStruct((T, n), dt) for n, dt in outs],
        compiler_params=_cparams("arbitrary"),
        name="inproj",
    )(x2, *consts1, *tabs, *consts2)


def _mla_prompt_body(q_ref, k_ref, c_ref, wuv_ref, o_ref, cb_scr, acc_scr, m_scr, l_scr, *, tq):
    i = pl.program_id(1)

    @pl.when(i == 0)
    def _():
        cb_scr[...] = c_ref[...].astype(bf16)

    row = lax.broadcasted_iota(jnp.int32, (tq, tq), 0)
    col = lax.broadcasted_iota(jnp.int32, (tq, tq), 1)
    causal = col <= row

    for h in range(MLA_HEADS):
        hs = slice(h * HEAD_PAD, (h + 1) * HEAD_PAD)
        q_h = q_ref[:, hs]
        m_scr[...] = jnp.full((tq, 1), NEG_INF, f32)
        l_scr[...] = jnp.zeros((tq, 1), f32)
        acc_scr[...] = jnp.zeros((tq, KV_RANK), f32)

        def step(kb, masked):
            ks = pl.ds(pl.multiple_of(kb * tq, tq), tq)
            s = _nt(q_h, k_ref[ks, hs])
            if masked:
                s = jnp.where(causal, s, NEG_INF)
            m_old = m_scr[...]
            m_new = jnp.maximum(m_old, jnp.max(s, axis=-1, keepdims=True))
            alpha = jnp.exp(m_old - m_new)
            p = jnp.exp(s - m_new)
            l_scr[...] = l_scr[...] * alpha + jnp.sum(p, axis=-1, keepdims=True)
            acc_scr[...] = acc_scr[...] * alpha + _mm(p.astype(bf16), cb_scr[ks, :])
            m_scr[...] = m_new

        def body(kb, carry):
            step(kb, False)
            return carry

        lax.fori_loop(0, i, body, 0)
        step(i, True)

        lat = (acc_scr[...] / l_scr[...]).astype(bf16)
        contrib = _mm(lat, wuv_ref[h])
        ps = slice((h // 2) * LANES, (h // 2 + 1) * LANES)
        if h % 2 == 0:
            o_ref[:, ps] = contrib
        else:
            o_ref[:, ps] += contrib


def _mla_prompt(q, k, c, wuv_pairs, nb, seq, tq):
    nq = seq // tq
    return pl.pallas_call(
        functools.partial(_mla_prompt_body, tq=tq),
        grid=(nb, nq),
        in_specs=[pl.BlockSpec((tq, QK_PACKED), lambda b, i: (b * nq + i, 0)),
                  pl.BlockSpec((seq, QK_PACKED), lambda b, i: (b, 0)),
                  pl.BlockSpec((seq, KV_RANK), lambda b, i: (b, 0)),
                  pl.BlockSpec(wuv_pairs.shape, lambda b, i: (0, 0, 0))],
        out_specs=pl.BlockSpec((tq, MLA_WIDTH), lambda b, i: (b * nq + i, 0)),
        out_shape=jax.ShapeDtypeStruct((nb * seq, MLA_WIDTH), f32),
        scratch_shapes=[pltpu.VMEM((seq, KV_RANK), bf16), pltpu.VMEM((tq, KV_RANK), f32),
                        pltpu.VMEM((tq, 1), f32), pltpu.VMEM((tq, 1), f32)],
        compiler_params=_cparams("arbitrary", "arbitrary"),
        name="mla_prompt",
    )(q, k, c, wuv_pairs)


def _mla_sample_body(pt_ref, q_ref, cnew_ref, kpenew_ref, wukt_ref, wukg_ref, gkr_ref, cost_ref, sint_ref,
                     costn_ref, sintn_ref, wuv_ref, *rest, pp, t_new):
    c_pages = rest[:pp]
    kpe_pages = rest[pp:2 * pp]
    o_ref = rest[2 * pp]
    lhs_scr, qr_scr, cb_scr, kt_scr, acc_scr, m_scr, l_scr = rest[2 * pp + 1:]
    j = pl.program_id(1)
    nrow = MLA_HEADS * t_new

    @pl.when(j == 0)
    def _():
        lhs_scr[0:MLA_HEADS * NOPE_DIM, :] = wukt_ref[...]
        for h in range(MLA_HEADS):
            q_h = q_ref[:, h * HEAD_PAD:(h + 1) * HEAD_PAD]
            r0 = MLA_HEADS * NOPE_DIM + h * t_new
            lhs_scr[r0:r0 + t_new, :] = _mm(q_h, wukg_ref[h]).astype(bf16)
            qr_scr[h * t_new:(h + 1) * t_new, :] = q_h[:, KPE_LANE:KPE_LANE + ROPE_DIM]
        m_scr[...] = jnp.full((nrow, 1), NEG_INF, f32)
        l_scr[...] = jnp.zeros((nrow, 1), f32)
        acc_scr[...] = jnp.zeros((nrow, KV_RANK), f32)

    gkr = gkr_ref[...]

    def attend(cb, kt, cos_t, sin_t, mask):
        n = cb.shape[0]
        big = _nt(lhs_scr[...], cb)
        kn = big[0:MLA_HEADS * NOPE_DIM]
        ss_nope = jnp.sum((kn * kn).reshape(MLA_HEADS, NOPE_DIM, n), axis=1)
        ss = ss_nope + jnp.sum(kt * kt, axis=0, keepdims=True)
        r = lax.rsqrt(ss * (1.0 / QK_DIM) + EPS)
        kg = kt * gkr
        k1, k2 = kg[0:HALF_ROPE], kg[HALF_ROPE:ROPE_DIM]
        kr = jnp.concatenate([k1 * cos_t - k2 * sin_t, k2 * cos_t + k1 * sin_t], axis=0).astype(bf16)
        s = big[MLA_HEADS * NOPE_DIM:] + _mm(qr_scr[...], kr)
        s = (s.reshape(MLA_HEADS, t_new, n) * r[:, None, :]).reshape(nrow, n)
        if mask is not None:
            s = jnp.where(mask, s, NEG_INF)
        m_old = m_scr[...]
        m_new = jnp.maximum(m_old, jnp.max(s, axis=-1, keepdims=True))
        alpha = jnp.exp(m_old - m_new)
        p = jnp.exp(s - m_new)
        l_scr[...] = l_scr[...] * alpha + jnp.sum(p, axis=-1, keepdims=True)
        acc_scr[...] = acc_scr[...] * alpha + _mm(p.astype(bf16), cb)
        m_scr[...] = m_new

    for i in range(pp):
        cb_scr[i * PAGE_SIZE:(i + 1) * PAGE_SIZE, :] = c_pages[i][...].astype(bf16)
        kt_scr[:, i * PAGE_SIZE:(i + 1) * PAGE_SIZE] = kpe_pages[i][...].T
    attend(cb_scr[...], kt_scr[...], cost_ref[...], sint_ref[...], None)

    @pl.when(j == pl.num_programs(1) - 1)
    def _():
        pad = PAGE_SIZE - t_new
        cn = jnp.concatenate([cnew_ref[...], jnp.zeros((pad, KV_RANK), f32)], axis=0).astype(bf16)
        kn = jnp.concatenate([kpenew_ref[...], jnp.zeros((pad, ROPE_DIM), f32)], axis=0).T
        key = lax.broadcasted_iota(jnp.int32, (nrow, PAGE_SIZE), 1)
        qry = lax.broadcasted_iota(jnp.int32, (nrow, PAGE_SIZE), 0) % t_new
        attend(cn, kn, costn_ref[...], sintn_ref[...], key <= qry)
        lat = acc_scr[...] / l_scr[...]
        for h in range(MLA_HEADS):
            contrib = _mm(lat[h * t_new:(h + 1) * t_new].astype(bf16), wuv_ref[h])
            ps = slice((h // 2) * LANES, (h // 2 + 1) * LANES)
            if h % 2 == 0:
                o_ref[:, ps] = contrib
            else:
                o_ref[:, ps] += contrib


def _mla_sample(layer, q, c_new, kpe_new, cache_ckv, cache_kpe, page_table, w, tabs_t, pp):
    ndb, n_pages = page_table.shape
    t_new = q.shape[0] // ndb
    npg = n_pages // pp
    kb = pp * PAGE_SIZE
    nrow = MLA_HEADS * t_new
    cost, sint, costn, sintn = tabs_t

    def page_spec(i, width):
        return pl.BlockSpec((None, None, PAGE_SIZE, width),
                            lambda b, j, pt: (layer, pt[b * n_pages + j * pp + i], 0, 0))

    fix2 = lambda b, j, pt: (0, 0)
    fix3 = lambda b, j, pt: (0, 0, 0)
    tok = lambda b, j, pt: (b, 0)
    in_specs = [pl.BlockSpec((t_new, QK_PACKED), tok),
                pl.BlockSpec((t_new, KV_RANK), tok),
                pl.BlockSpec((t_new, ROPE_DIM), tok),
                pl.BlockSpec(w['w_uk_t'].shape, fix2),
                pl.BlockSpec(w['w_uk_g'].shape, fix3),
                pl.BlockSpec(w['g_k_rope'].shape, fix2),
                pl.BlockSpec((HALF_ROPE, kb), lambda b, j, pt: (0, j)),
                pl.BlockSpec((HALF_ROPE, kb), lambda b, j, pt: (0, j)),
                pl.BlockSpec(costn.shape, fix2),
                pl.BlockSpec(sintn.shape, fix2),
                pl.BlockSpec(w['w_uv_pairs'].shape, fix3)]
    in_specs += [page_spec(i, KV_RANK) for i in range(pp)] + [page_spec(i, ROPE_DIM) for i in range(pp)]
    return pl.pallas_call(
        functools.partial(_mla_sample_body, pp=pp, t_new=t_new),
        grid_spec=pltpu.PrefetchScalarGridSpec(
            num_scalar_prefetch=1,
            grid=(ndb, npg),
            in_specs=in_specs,
            out_specs=pl.BlockSpec((t_new, MLA_WIDTH), tok),
            scratch_shapes=[pltpu.VMEM((MLA_HEADS * NOPE_DIM + nrow, KV_RANK), bf16),
                            pltpu.VMEM((nrow, ROPE_DIM), bf16),
                            pltpu.VMEM((kb, KV_RANK), bf16),
                            pltpu.VMEM((ROPE_DIM, kb), f32),
                            pltpu.VMEM((nrow, KV_RANK), f32),
                            pltpu.VMEM((nrow, 1), f32),
                            pltpu.VMEM((nrow, 1), f32)]),
        out_shape=jax.ShapeDtypeStruct((ndb * t_new, MLA_WIDTH), f32),
        compiler_params=_cparams("arbitrary", "arbitrary"),
        name="mla_sample",
    )(page_table.reshape(-1), q, c_new, kpe_new, w['w_uk_t'], w['w_uk_g'], w['g_k_rope'], cost, sint, costn, sintn,
      w['w_uv_pairs'], *([cache_ckv] * pp), *([cache_kpe] * pp))


def _split3(x):
    hi = x.astype(bf16)
    r1 = x - hi.astype(f32)
    mid = r1.astype(bf16)
    lo = (r1 - mid.astype(f32)).astype(bf16)
    return hi, mid, lo


def _gla_body(*refs, nb, tt, cs, has_init):
    if has_init:
        q_ref, k_ref, v_ref, g_ref, s0_ref, o_ref, sT_out, st_scr = refs
    else:
        q_ref, k_ref, v_ref, g_ref, o_ref, sT_out, st_scr = refs
    ti = pl.program_id(1)
    nchunk = tt // cs

    @pl.when(ti == 0)
    def _():
        if has_init:
            st_scr[...] = s0_ref[...]
        else:
            st_scr[...] = jnp.zeros(st_scr.shape, f32)

    tri = (lax.broadcasted_iota(jnp.int32, (tt, tt), 1) <= lax.broadcasted_iota(jnp.int32, (tt, tt), 0)).astype(bf16)
    kv_head = (lax.broadcasted_iota(jnp.int32, (GLA_KDIM, GLA_WIDTH), 0) // GLA_DK
               == lax.broadcasted_iota(jnp.int32, (GLA_KDIM, GLA_WIDTH), 1) // GLA_DV)
    block_ones = kv_head.astype(bf16)
    vk_head = (lax.broadcasted_iota(jnp.int32, (GLA_WIDTH, GLA_KDIM), 0) // GLA_DV
               == lax.broadcasted_iota(jnp.int32, (GLA_WIDTH, GLA_KDIM), 1) // GLA_DK)
    sel_t = (lax.broadcasted_iota(jnp.int32, (cs, cs * cs), 1) // cs
             == lax.broadcasted_iota(jnp.int32, (cs, cs * cs), 0)).astype(bf16)
    t_idx = lax.broadcasted_iota(jnp.int32, (cs, cs, GLA_KDIM), 0)
    s_idx = lax.broadcasted_iota(jnp.int32, (cs, cs, GLA_KDIM), 1)
    causal3 = s_idx <= t_idx

    for b in range(nb):
        q, k, v, g = q_ref[b], k_ref[b], v_ref[b], g_ref[b]
        hi, mid, lo = _split3(g)
        cum = _mm(tri, hi) + _mm(tri, mid) + _mm(tri, lo)
        for n in range(nchunk):
            sl = slice(n * cs, (n + 1) * cs)
            base = cum[n * cs - 1:n * cs] if n > 0 else jnp.zeros((1, GLA_KDIM), f32)
            bc = cum[sl] - base
            b_last = bc[cs - 1:cs]
            qc, kc, vc = q[sl], k[sl], v[sl]
            st = st_scr[b]
            o_inter = _nt((qc * jnp.exp(bc)).astype(bf16), st.astype(bf16))
            diff = jnp.where(causal3, bc[:, None, :] - bc[None, :, :], NEG_INF)
            d3 = qc[:, None, :] * kc[None, :, :] * jnp.exp(diff)
            a_exp = _mm(d3.reshape(cs * cs, GLA_KDIM).astype(bf16), block_ones)
            xv = (a_exp.reshape(cs, cs, GLA_WIDTH) * vc[None, :, :]).reshape(cs * cs, GLA_WIDTH)
            o_intra = _mm(sel_t, xv.astype(bf16))
            o_ref[b, sl, :] = o_inter + o_intra
            kd = (kc * jnp.exp(b_last - bc)).astype(bf16)
            upd = _tn(vc.astype(bf16), kd)
            st_scr[b] = st * jnp.exp(b_last) + jnp.where(vk_head, upd, 0.0)

    @pl.when(ti == pl.num_programs(1) - 1)
    def _():
        sT_out[...] = st_scr[...]


def _gla(q, k, v, g, s0t, nb, tt, cs):
    B, L, _ = q.shape
    has_init = s0t is not None
    tok = lambda bi, ti: (bi, ti, 0)
    st = lambda bi, ti: (bi, 0, 0)
    in_specs = [pl.BlockSpec((nb, tt, GLA_KDIM), tok), pl.BlockSpec((nb, tt, GLA_KDIM), tok),
                pl.BlockSpec((nb, tt, GLA_WIDTH), tok), pl.BlockSpec((nb, tt, GLA_KDIM), tok)]
    args = [q, k, v, g]
    if has_init:
        in_specs.append(pl.BlockSpec((nb, GLA_WIDTH, GLA_KDIM), st))
        args.append(s0t)
    return pl.pallas_call(
        functools.partial(_gla_body, nb=nb, tt=tt, cs=cs, has_init=has_init),
        grid=(B // nb, L // tt),
        in_specs=in_specs,
        out_specs=[pl.BlockSpec((nb, tt, GLA_WIDTH), tok), pl.BlockSpec((nb, GLA_WIDTH, GLA_KDIM), st)],
        out_shape=[jax.ShapeDtypeStruct((B, L, GLA_WIDTH), f32), jax.ShapeDtypeStruct((B, GLA_WIDTH, GLA_KDIM), f32)],
        scratch_shapes=[pltpu.VMEM((nb, GLA_WIDTH, GLA_KDIM), f32)],
        compiler_params=_cparams("arbitrary", "arbitrary"),
        name="gla",
    )(*args)


def _memkv_body(mem_ref, gmem_ref, w_ref, gk_ref, mk_out, mv_out):
    xb = _rms_rows(mem_ref[...], gmem_ref[...]).astype(bf16)
    kv = _mm(xb, w_ref[...])
    lane = lax.broadcasted_iota(jnp.int32, (1, LANES), 1)
    gk = gk_ref[...]
    for j in range(MEM_WIDTH // LANES):
        mk_out[:, j * LANES:(j + 1) * LANES] = _half_head_rms(kv[:, j * LANES:(j + 1) * LANES], gk, lane)
    mv_out[...] = kv[:, MEM_WIDTH:2 * MEM_WIDTH]


def _memkv(mem2, w, tm):
    T = mem2.shape[0]
    row = lambda i: (i, 0)
    fix = lambda i: (0, 0)
    return pl.pallas_call(
        _memkv_body,
        grid=(T // tm,),
        in_specs=[pl.BlockSpec((tm, D_MODEL), row), pl.BlockSpec(w['g_mem'].shape, fix),
                  pl.BlockSpec(w['w_mem_kv'].shape, fix), pl.BlockSpec(w['g_mem_k'].shape, fix)],
        out_specs=[pl.BlockSpec((tm, MEM_WIDTH), row)] * 2,
        out_shape=[jax.ShapeDtypeStruct((T, MEM_WIDTH), f32)] * 2,
        compiler_params=_cparams("arbitrary"),
        name="memkv",
    )(mem2, w['g_mem'], w['w_mem_kv'], w['g_mem_k'])


def _memattn_body(q_ref, mk_ref, mv_ref, o_ref, *, nb, tq):
    lane_head = lax.broadcasted_iota(jnp.int32, (1, MEM_WIDTH), 1) // MEM_HEAD_DIM
    for b in range(nb):
        q = q_ref[b]
        mk = mk_ref[b].astype(bf16)
        mv = mv_ref[b].astype(bf16)
        qs = jnp.concatenate([jnp.where(lane_head == h, q, jnp.zeros_like(q)) for h in range(MEM_HEADS)], axis=0)
        s = _nt(qs, mk)
        p = jnp.exp(s - jnp.max(s, axis=-1, keepdims=True))
        pv = _mm(p.astype(bf16), mv) / jnp.sum(p, axis=-1, keepdims=True)
        o = jnp.zeros((tq, MEM_WIDTH), f32)
        for h in range(MEM_HEADS):
            o = o + jnp.where(lane_head == h, pv[h * tq:(h + 1) * tq], 0.0)
        o_ref[b] = o


def _memattn(q3, mk3, mv3, nb, tq):
    B, L, _ = q3.shape
    n_mem = mk3.shape[1]
    return pl.pallas_call(
        functools.partial(_memattn_body, nb=nb, tq=tq),
        grid=(B // nb, L // tq),
        in_specs=[pl.BlockSpec((nb, tq, MEM_WIDTH), lambda b, i: (b, i, 0)),
                  pl.BlockSpec((nb, n_mem, MEM_WIDTH), lambda b, i: (b, 0, 0)),
                  pl.BlockSpec((nb, n_mem, MEM_WIDTH), lambda b, i: (b, 0, 0))],
        out_specs=pl.BlockSpec((nb, tq, MEM_WIDTH), lambda b, i: (b, i, 0)),
        out_shape=jax.ShapeDtypeStruct((B, L, MEM_WIDTH), f32),
        compiler_params=_cparams("arbitrary", "arbitrary"),
        name="memattn",
    )(q3, mk3, mv3)


def _merge_body(x_ref, mla_ref, gla_ref, mem_ref, gates_ref, ggla_ref, wout_ref, y_ref):
    lane = lax.broadcasted_iota(jnp.int32, (1, LANES), 1)
    ggla = ggla_ref[...]
    y = x_ref[...]
    m1 = (gates_ref[:, 0:MLA_WIDTH] * mla_ref[...]).astype(bf16)
    y = y + _mm(m1, wout_ref[0:MLA_WIDTH, :])
    for j in range(GLA_WIDTH // LANES):
        ls = slice(j * LANES, (j + 1) * LANES)
        gn = _half_head_rms(gla_ref[:, ls], ggla, lane)
        gs = slice(MLA_WIDTH + j * LANES, MLA_WIDTH + (j + 1) * LANES)
        y = y + _mm((gates_ref[:, gs] * gn).astype(bf16), wout_ref[gs, :])
    ms = slice(MLA_WIDTH + GLA_WIDTH, D_MIX)
    m3 = (gates_ref[:, ms] * mem_ref[...]).astype(bf16)
    y_ref[...] = y + _mm(m3, wout_ref[ms, :])


def _merge(x2, mla_o, gla_o, mem_o, gates, w, tm):
    T = x2.shape[0]
    row = lambda i: (i, 0)
    fix = lambda i: (0, 0)
    return pl.pallas_call(
        _merge_body,
        grid=(T // tm,),
        in_specs=[pl.BlockSpec((tm, D_MODEL), row), pl.BlockSpec((tm, MLA_WIDTH), row),
                  pl.BlockSpec((tm, GLA_WIDTH), row), pl.BlockSpec((tm, MEM_WIDTH), row),
                  pl.BlockSpec((tm, D_MIX), row), pl.BlockSpec(w['g_gla_o'].shape, fix),
                  pl.BlockSpec(w['w_out'].shape, fix)],
        out_specs=pl.BlockSpec((tm, D_MODEL), row),
        out_shape=jax.ShapeDtypeStruct((T, D_MODEL), f32),
        compiler_params=_cparams("arbitrary"),
        name="merge",
    )(x2, mla_o, gla_o, mem_o, gates, w['g_gla_o'], w['w_out'])


def _pad_head(a):
    return jnp.pad(a, [(0, 0)] * (a.ndim - 1) + [(0, HEAD_PAD - QK_DIM)])


def _prep_layer(l, p):
    w_in = p['w_in'][l]
    cuts, o = [], 0
    for n in (Q_RANK, KV_RANK, ROPE_DIM, MLA_WIDTH, GLA_KDIM, GLA_KDIM, GLA_WIDTH, GLA_GATE_RANK, GLA_WIDTH,
              MEM_WIDTH, MEM_WIDTH):
        cuts.append(w_in[:, o:o + n])
        o += n
    cq, ckv, kpe, gate_mla, gq, gk, gv, gg, gate_gla, mq, gate_mem = cuts
    z = lambda n: jnp.zeros((D_MODEL, n), f32)
    misc = jnp.concatenate([gg, z(KPE_LANE - GLA_GATE_RANK), kpe, z(LANES - KPE_LANE - ROPE_DIM)], axis=1)
    w_in_packed = jnp.concatenate([cq, ckv, gate_mla, gq, gk, gv, gate_gla, mq, gate_mem, misc], axis=1)

    w_uk = p['w_uk'][l]
    g_k = p['g_mla_k'][l]
    wuk_packed = jnp.pad(w_uk, ((0, 0), (0, 0), (0, HEAD_PAD - NOPE_DIM))).reshape(KV_RANK, QK_PACKED)
    w_uk_t = w_uk.reshape(KV_RANK, MLA_HEADS * NOPE_DIM).T
    w_uk_g = jnp.pad((w_uk * g_k[None, None, :NOPE_DIM]).transpose(1, 2, 0),
                     ((0, 0), (0, HEAD_PAD - NOPE_DIM), (0, 0)))
    w_uv = p['w_uv'][l].transpose(1, 0, 2)
    zeros = jnp.zeros_like(w_uv)
    even = jnp.arange(MLA_HEADS)[:, None, None] % 2 == 0
    w_uv_pairs = jnp.concatenate([jnp.where(even, w_uv, zeros), jnp.where(even, zeros, w_uv)], axis=-1)
    w_gk = jnp.pad(p['w_gk'][l], ((0, LANES - GLA_GATE_RANK), (0, 0)))
    row = lambda a: a.reshape(1, -1)
    return {
        'g_pre': row(p['g_pre'][l]), 'w_in': w_in_packed.astype(bf16), 'g_qa': row(p['g_qa'][l]),
        'w_uq': _pad_head(p['w_uq'][l]).reshape(Q_RANK, QK_PACKED).astype(bf16),
        'g_kva': row(p['g_kva'][l]), 'w_uk': wuk_packed.astype(bf16),
        'g_q': row(_pad_head(p['g_mla_q'][l])), 'g_k': row(_pad_head(g_k)),
        'w_gk': w_gk.astype(bf16), 'b_gk': row(p['b_gk'][l]),
        'g_mem_q': row(jnp.tile(p['g_mem_q'][l], 2)), 'g_mem_k': row(jnp.tile(p['g_mem_k'][l], 2)),
        'g_gla_o': row(jnp.tile(p['g_gla_o'][l], 2)),
        'w_uk_t': w_uk_t.astype(bf16), 'w_uk_g': w_uk_g.astype(bf16),
        'g_k_rope': g_k[NOPE_DIM:].reshape(ROPE_DIM, 1),
        'w_uv_pairs': w_uv_pairs.astype(bf16),
        'g_mem': row(p['g_mem'][l]), 'w_mem_kv': p['w_mem_kv'][l].astype(bf16),
        'w_out': p['w_out'][l].astype(bf16),
    }


def _angles(pos):
    inv_freq = ROPE_THETA ** (-jnp.arange(HALF_ROPE, dtype=f32) * (2.0 / ROPE_DIM))
    return pos.astype(f32)[:, None] * inv_freq[None, :]


def _packed_tables(pos):
    ang = _angles(pos)
    n = pos.shape[0]
    cos, sin = jnp.cos(ang), jnp.sin(ang)
    one = jnp.ones((n, NOPE_DIM), f32)
    z = lambda w: jnp.zeros((n, w), f32)
    tail = HEAD_PAD - QK_DIM
    cos_p = jnp.concatenate([one, cos, cos, z(tail)], axis=1)
    sin_up = jnp.concatenate([z(NOPE_DIM + HALF_ROPE), sin, z(tail)], axis=1)
    sin_dn = jnp.concatenate([z(NOPE_DIM), -sin, z(HALF_ROPE + tail)], axis=1)
    return cos_p, sin_up, sin_dn


def _transposed_tables(pos):
    ang = _angles(pos).T
    return jnp.cos(ang), jnp.sin(ang)


def _state_to_blockdiag_t(s):
    B = s.shape[0]
    st = s.transpose(0, 1, 3, 2)
    eye = jnp.eye(GLA_HEADS, dtype=s.dtype)
    return jnp.einsum('bhvk,hg->bhvgk', st, eye).reshape(B, GLA_WIDTH, GLA_KDIM)


def _blockdiag_t_to_state(st):
    B = st.shape[0]
    s5 = st.reshape(B, GLA_HEADS, GLA_DV, GLA_HEADS, GLA_DK)
    diag = jnp.stack([s5[:, h, :, h, :] for h in range(GLA_HEADS)], axis=1)
    return diag.transpose(0, 1, 3, 2)


def kernel(x_prompt, x_sample, mem_prompt, cache_ckv, cache_kpe, page_table, state_gla, cache_mem_k, cache_mem_v, g_pre, w_in, g_qa, w_uq, g_kva, w_uk, w_uv, g_mla_q, g_mla_k, w_gk, b_gk, g_gla_o, g_mem, w_mem_kv, g_mem_q, g_mem_k, w_out):
    params = dict(g_pre=g_pre, w_in=w_in, g_qa=g_qa, w_uq=w_uq, g_kva=g_kva, w_uk=w_uk, w_uv=w_uv,
                  g_mla_q=g_mla_q, g_mla_k=g_mla_k, w_gk=w_gk, b_gk=b_gk, g_gla_o=g_gla_o, g_mem=g_mem,
                  w_mem_kv=w_mem_kv, g_mem_q=g_mem_q, g_mem_k=g_mem_k, w_out=w_out)
    nb, seq, _ = x_prompt.shape
    ndb, t_new, _ = x_sample.shape
    n_mem = mem_prompt.shape[1]
    depth = w_in.shape[0]
    n_pages = page_table.shape[1]
    past_len = n_pages * cache_ckv.shape[2]

    tm_p = min(256, seq)
    tq = min(256, seq)
    tm_s = min(256, ndb * t_new)
    pp = min(8, n_pages)
    gla_tt = min(128, seq)
    gla_cs = min(16, gla_tt)
    nb_s = min(8, ndb)

    tabs_p = _packed_tables(jnp.arange(seq, dtype=jnp.int32))
    pos_s = past_len + jnp.arange(t_new, dtype=jnp.int32)
    tabs_s = tuple(jnp.tile(t, (tm_s // t_new, 1)) for t in _packed_tables(pos_s))
    tabs_t = _transposed_tables(jnp.arange(past_len, dtype=jnp.int32)) \
        + _transposed_tables(past_len + jnp.arange(PAGE_SIZE, dtype=jnp.int32))

    xp = x_prompt.reshape(nb * seq, D_MODEL)
    xs = x_sample.reshape(ndb * t_new, D_MODEL)
    mem2 = mem_prompt.reshape(nb * n_mem, D_MODEL)
    ckv_p, kpe_p, gla_p, mk_p, mv_p, ckv_s, kpe_s, gla_s = ([] for _ in range(8))
    for l in range(depth):
        w = _prep_layer(l, params)
        q, k, c, kpe, gates, gq, gk, gv, gg, mq = _inproj(xp, tabs_p, w, tm_p)
        mla_o = _mla_prompt(q, k, c, w['w_uv_pairs'], nb, seq, tq)
        r3 = lambda a: a.reshape(nb, seq, a.shape[-1])
        gla_o, st = _gla(r3(gq), r3(gk), r3(gv), r3(gg), None, 1, gla_tt, gla_cs)
        mk, mv = _memkv(mem2, w, min(256, nb * n_mem))
        mem_o = _memattn(r3(mq), mk.reshape(nb, n_mem, MEM_WIDTH), mv.reshape(nb, n_mem, MEM_WIDTH), 1, tq)
        xp = _merge(xp, mla_o, gla_o.reshape(nb * seq, GLA_WIDTH), mem_o.reshape(nb * seq, MEM_WIDTH), gates, w, tm_p)
        ckv_p.append(c.reshape(nb, seq, KV_RANK))
        kpe_p.append(kpe.reshape(nb, seq, ROPE_DIM))
        gla_p.append(_blockdiag_t_to_state(st))
        mk_p.append(mk.reshape(nb, n_mem, MEM_HEADS, MEM_HEAD_DIM))
        mv_p.append(mv.reshape(nb, n_mem, MEM_HEADS, MEM_HEAD_DIM))
        q, k, c, kpe, gates, gq, gk, gv, gg, mq = _inproj(xs, tabs_s, w, tm_s)
        mla_o = _mla_sample(l, q, c, kpe, cache_ckv, cache_kpe, page_table, w, tabs_t, pp)
        r3 = lambda a: a.reshape(ndb, t_new, a.shape[-1])
        gla_o, st = _gla(r3(gq), r3(gk), r3(gv), r3(gg), _state_to_blockdiag_t(state_gla[l]), nb_s, t_new, t_new)
        mem_o = _memattn(r3(mq), cache_mem_k[l].reshape(ndb, n_mem, MEM_WIDTH),
                         cache_mem_v[l].reshape(ndb, n_mem, MEM_WIDTH), nb_s, t_new)
        xs = _merge(xs, mla_o, gla_o.reshape(ndb * t_new, GLA_WIDTH), mem_o.reshape(ndb * t_new, MEM_WIDTH), gates, w, tm_s)
        ckv_s.append(c.reshape(ndb, t_new, KV_RANK))
        kpe_s.append(kpe.reshape(ndb, t_new, ROPE_DIM))
        gla_s.append(_blockdiag_t_to_state(st))
    return (xp.reshape(nb, seq, D_MODEL), xs.reshape(ndb, t_new, D_MODEL), jnp.stack(ckv_p), jnp.stack(kpe_p),
            jnp.stack(gla_p), jnp.stack(mk_p), jnp.stack(mv_p), jnp.stack(ckv_s), jnp.stack(kpe_s), jnp.stack(gla_s))
```

```python
import functools

import jax
import jax.numpy as jnp
from jax import lax
from jax.experimental import pallas as pl
from jax.experimental.pallas import tpu as pltpu

f32, bf16 = jnp.float32, jnp.bfloat16

D_MODEL = 1024
PAGE_SIZE = 128
MLA_HEADS = 8
NOPE_DIM = 64
ROPE_DIM = 32
HALF_ROPE = ROPE_DIM // 2
QK_DIM = NOPE_DIM + ROPE_DIM
V_DIM = 64
Q_RANK = 384
KV_RANK = 256
MLA_WIDTH = MLA_HEADS * V_DIM
ROPE_THETA = 10000.0
MLA_SCALE = QK_DIM ** -0.5
GLA_HEADS = 4
GLA_DK = 32
GLA_DV = 64
GLA_KDIM = GLA_HEADS * GLA_DK
GLA_WIDTH = GLA_HEADS * GLA_DV
GLA_GATE_RANK = 16
GLA_NORMALIZER = 16.0
MEM_HEADS = 4
MEM_HEAD_DIM = 64
MEM_WIDTH = MEM_HEADS * MEM_HEAD_DIM
MEM_SCALE = MEM_HEAD_DIM ** -0.5
D_MIX = MLA_WIDTH + GLA_WIDTH + MEM_WIDTH
EPS = 1e-6
NEG_INF = -1e30

LANES = 128
HEAD_PAD = LANES
QK_PACKED = MLA_HEADS * HEAD_PAD

OFF_CQ = 0
OFF_CKV = OFF_CQ + Q_RANK
OFF_GATE_MLA = OFF_CKV + KV_RANK
OFF_GQ = OFF_GATE_MLA + MLA_WIDTH
OFF_GK = OFF_GQ + GLA_KDIM
OFF_GV = OFF_GK + GLA_KDIM
OFF_GATE_GLA = OFF_GV + GLA_WIDTH
OFF_MQ = OFF_GATE_GLA + GLA_WIDTH
OFF_GATE_MEM = OFF_MQ + MEM_WIDTH
OFF_MISC = OFF_GATE_MEM + MEM_WIDTH
D_IN_PACKED = OFF_MISC + LANES
KPE_LANE = NOPE_DIM

VMEM_LIMIT = 56 * 1024 * 1024


def _cparams(*sem):
    return pltpu.CompilerParams(dimension_semantics=sem, vmem_limit_bytes=VMEM_LIMIT)


def _nt(a, b):
    return lax.dot_general(a, b, (((1,), (1,)), ((), ())), preferred_element_type=f32)


def _tn(a, b):
    return lax.dot_general(a, b, (((0,), (0,)), ((), ())), preferred_element_type=f32)


def _mm(a, b):
    return jnp.dot(a, b, preferred_element_type=f32)


def _rms_rows(x, g):
    return x * lax.rsqrt(jnp.mean(x * x, axis=-1, keepdims=True) + EPS) * g


def _half_head_rms(blk, g, lane):
    sq = blk * blk
    lo = lane < 64
    ss_lo = jnp.sum(jnp.where(lo, sq, 0.0), axis=-1, keepdims=True)
    ss_hi = jnp.sum(jnp.where(lo, 0.0, sq), axis=-1, keepdims=True)
    ss = jnp.where(lo, ss_lo, ss_hi)
    return blk * lax.rsqrt(ss * (1.0 / 64.0) + EPS) * g


def _rope_packed(x, cos, sin_up, sin_dn):
    return x * cos + pltpu.roll(x, HALF_ROPE, 1) * sin_up + pltpu.roll(x, LANES - HALF_ROPE, 1) * sin_dn


def _inproj_body(x_ref, gpre_ref, win_ref, gqa_ref, wuq_ref, gkva_ref, wuk_ref, gq_ref, gk_ref,
                 cos_ref, sup_ref, sdn_ref, wgk_ref, bgk_ref, gmq_ref,
                 q_out, k_out, ckv_out, kpe_out, gates_out, glaq_out, glak_out, glav_out, glag_out, memq_out):
    x = x_ref[...]
    xb = _rms_rows(x, gpre_ref[...]).astype(bf16)

    def seg(off, n):
        return _mm(xb, win_ref[:, off:off + n])

    cos, sup, sdn = cos_ref[...], sup_ref[...], sdn_ref[...]
    lane = lax.broadcasted_iota(jnp.int32, (1, LANES), 1)

    cq = _rms_rows(seg(OFF_CQ, Q_RANK), gqa_ref[...]).astype(bf16)
    qf = _mm(cq, wuq_ref[...])
    gq = gq_ref[...]
    for h in range(MLA_HEADS):
        qh = qf[:, h * HEAD_PAD:(h + 1) * HEAD_PAD]
        ss = jnp.sum(qh * qh, axis=-1, keepdims=True)
        qn = qh * lax.rsqrt(ss * (1.0 / QK_DIM) + EPS) * gq
        q_out[:, h * HEAD_PAD:(h + 1) * HEAD_PAD] = (_rope_packed(qn, cos, sup, sdn) * MLA_SCALE).astype(bf16)

    c = _rms_rows(seg(OFF_CKV, KV_RANK), gkva_ref[...])
    ckv_out[...] = c
    misc = seg(OFF_MISC, LANES)
    kpe_wide = jnp.where((lane >= KPE_LANE) & (lane < KPE_LANE + ROPE_DIM), misc, 0.0)
    kpe_out[...] = misc[:, KPE_LANE:KPE_LANE + ROPE_DIM]
    kf = _mm(c.astype(bf16), wuk_ref[...])
    gk = gk_ref[...]
    for h in range(MLA_HEADS):
        kh = kf[:, h * HEAD_PAD:(h + 1) * HEAD_PAD] + kpe_wide
        ss = jnp.sum(kh * kh, axis=-1, keepdims=True)
        kn = kh * lax.rsqrt(ss * (1.0 / QK_DIM) + EPS) * gk
        k_out[:, h * HEAD_PAD:(h + 1) * HEAD_PAD] = _rope_packed(kn, cos, sup, sdn).astype(bf16)

    g1 = seg(OFF_GATE_MLA, MLA_WIDTH)
    gates_out[:, 0:MLA_WIDTH] = g1 * jax.nn.sigmoid(g1)
    g2 = seg(OFF_GATE_GLA, GLA_WIDTH)
    gates_out[:, MLA_WIDTH:MLA_WIDTH + GLA_WIDTH] = g2 * jax.nn.sigmoid(g2)
    g3 = seg(OFF_GATE_MEM, MEM_WIDTH)
    gates_out[:, MLA_WIDTH + GLA_WIDTH:D_MIX] = g3 * jax.nn.sigmoid(g3)

    glaq_out[...] = seg(OFF_GQ, GLA_KDIM) * (GLA_DK ** -0.5)
    glak_out[...] = seg(OFF_GK, GLA_KDIM)
    glav_out[...] = seg(OFF_GV, GLA_WIDTH)
    gl = _mm(misc.astype(bf16), wgk_ref[...]) + bgk_ref[...]
    glag_out[...] = jax.nn.log_sigmoid(gl) * (1.0 / GLA_NORMALIZER)

    mq = seg(OFF_MQ, MEM_WIDTH)
    gmq = gmq_ref[...]
    for j in range(MEM_WIDTH // LANES):
        blk = _half_head_rms(mq[:, j * LANES:(j + 1) * LANES], gmq, lane)
        memq_out[:, j * LANES:(j + 1) * LANES] = (blk * MEM_SCALE).astype(bf16)


def _inproj(x2, tabs, w, tm):
    T = x2.shape[0]
    nt = tabs[0].shape[0] // tm
    row = lambda i: (i, 0)
    fix = lambda i: (0, 0)
    tab = lambda i: (i % nt, 0)

    def full(a):
        return pl.BlockSpec(a.shape, fix)

    consts1 = (w['g_pre'], w['w_in'], w['g_qa'], w['w_uq'], w['g_kva'], w['w_uk'], w['g_q'], w['g_k'])
    consts2 = (w['w_gk'], w['b_gk'], w['g_mem_q'])
    outs = [(QK_PACKED, bf16), (QK_PACKED, bf16), (KV_RANK, f32), (ROPE_DIM, f32), (D_MIX, f32),
            (GLA_KDIM, f32), (GLA_KDIM, f32), (GLA_WIDTH, f32), (GLA_KDIM, f32), (MEM_WIDTH, bf16)]
    return pl.pallas_call(
        _inproj_body,
        grid=(T // tm,),
        in_specs=[pl.BlockSpec((tm, D_MODEL), row)] + [full(a) for a in consts1]
        + [pl.BlockSpec((tm, LANES), tab)] * 3 + [full(a) for a in consts2],
        out_specs=[pl.BlockSpec((tm, n), row) for n, _ in outs],
        out_shape=[jax.ShapeDtypeStruct((T, n), dt) for n, dt in outs],
        compiler_params=_cparams("arbitrary"),
        name="inproj",
    )(x2, *consts1, *tabs, *consts2)


def _mla_prompt_body(q_ref, k_ref, c_ref, wuvt_ref, o_ref, ct_scr, acc_scr, m_scr, l_scr, ot_scr, *, tq):
    i = pl.program_id(1)
    nblk = ct_scr.shape[0]

    @pl.when(i == 0)
    def _():
        for j in range(nblk):
            ct_scr[j] = c_ref[j * tq:(j + 1) * tq, :].T.astype(bf16)

    m_scr[...] = jnp.full(m_scr.shape, NEG_INF, f32)
    l_scr[...] = jnp.zeros(l_scr.shape, f32)
    acc_scr[...] = jnp.zeros(acc_scr.shape, f32)
    key = lax.broadcasted_iota(jnp.int32, (tq, tq), 0)
    qry = lax.broadcasted_iota(jnp.int32, (tq, tq), 1)
    causal = key <= qry

    def step(kb, masked):
        ks = pl.ds(pl.multiple_of(kb * tq, tq), tq)
        ct = ct_scr[kb]
        for h in range(MLA_HEADS):
            hs = slice(h * HEAD_PAD, (h + 1) * HEAD_PAD)
            s = _nt(k_ref[ks, hs], q_ref[:, hs])
            if masked:
                s = jnp.where(causal, s, NEG_INF)
            m_old = m_scr[h:h + 1, :]
            m_new = jnp.maximum(m_old, jnp.max(s, axis=0, keepdims=True))
            alpha = jnp.exp(m_old - m_new)
            p = jnp.exp(s - m_new)
            l_scr[h:h + 1, :] = l_scr[h:h + 1, :] * alpha + jnp.sum(p, axis=0, keepdims=True)
            acc_scr[h] = acc_scr[h] * alpha + _mm(ct, p.astype(bf16))
            m_scr[h:h + 1, :] = m_new

    def body(kb, carry):
        step(kb, False)
        return carry

    lax.fori_loop(0, i, body, 0)
    step(i, True)

    for h in range(MLA_HEADS):
        lat_t = (acc_scr[h] / l_scr[h:h + 1, :]).astype(bf16)
        ot_scr[h * V_DIM:(h + 1) * V_DIM, :] = _mm(wuvt_ref[h], lat_t)
    o_ref[...] = ot_scr[...].T


def _mla_prompt(q, k, c, wuv_t, nb, seq, tq):
    nq = seq // tq
    return pl.pallas_call(
        functools.partial(_mla_prompt_body, tq=tq),
        grid=(nb, nq),
        in_specs=[pl.BlockSpec((tq, QK_PACKED), lambda b, i: (b * nq + i, 0)),
                  pl.BlockSpec((seq, QK_PACKED), lambda b, i: (b, 0)),
                  pl.BlockSpec((seq, KV_RANK), lambda b, i: (b, 0)),
                  pl.BlockSpec(wuv_t.shape, lambda b, i: (0, 0, 0))],
        out_specs=pl.BlockSpec((tq, MLA_WIDTH), lambda b, i: (b * nq + i, 0)),
        out_shape=jax.ShapeDtypeStruct((nb * seq, MLA_WIDTH), f32),
        scratch_shapes=[pltpu.VMEM((nq, KV_RANK, tq), bf16), pltpu.VMEM((MLA_HEADS, KV_RANK, tq), f32),
                        pltpu.VMEM((MLA_HEADS, tq), f32), pltpu.VMEM((MLA_HEADS, tq), f32),
                        pltpu.VMEM((MLA_WIDTH, tq), f32)],
        compiler_params=_cparams("arbitrary", "arbitrary"),
        name="mla_prompt",
    )(q, k, c, wuv_t)


def _mla_sample_body(pt_ref, q_ref, cnew_ref, kpenew_ref, wukt_ref, wukg_ref, gkr_ref, cost_ref, sint_ref,
                     costn_ref, sintn_ref, wuv_ref, *rest, pp, t_new):
    c_pages = rest[:pp]
    kpe_pages = rest[pp:2 * pp]
    o_ref = rest[2 * pp]
    lhs_scr, qr_scr, cb_scr, kt_scr, acc_scr, m_scr, l_scr = rest[2 * pp + 1:]
    j = pl.program_id(1)
    nrow = MLA_HEADS * t_new

    @pl.when(j == 0)
    def _():
        lhs_scr[0:MLA_HEADS * NOPE_DIM, :] = wukt_ref[...]
        for h in range(MLA_HEADS):
            q_h = q_ref[:, h * HEAD_PAD:(h + 1) * HEAD_PAD]
            r0 = MLA_HEADS * NOPE_DIM + h * t_new
            lhs_scr[r0:r0 + t_new, :] = _mm(q_h, wukg_ref[h]).astype(bf16)
            qr_scr[h * t_new:(h + 1) * t_new, :] = q_h[:, KPE_LANE:KPE_LANE + ROPE_DIM]
        m_scr[...] = jnp.full((nrow, 1), NEG_INF, f32)
        l_scr[...] = jnp.zeros((nrow, 1), f32)
        acc_scr[...] = jnp.zeros((nrow, KV_RANK), f32)

    gkr = gkr_ref[...]

    def attend(cb, kt, cos_t, sin_t, mask):
        n = cb.shape[0]
        big = _nt(lhs_scr[...], cb)
        kn = big[0:MLA_HEADS * NOPE_DIM]
        ss_nope = jnp.sum((kn * kn).reshape(MLA_HEADS, NOPE_DIM, n), axis=1)
        ss = ss_nope + jnp.sum(kt * kt, axis=0, keepdims=True)
        r = lax.rsqrt(ss * (1.0 / QK_DIM) + EPS)
        kg = kt * gkr
        k1, k2 = kg[0:HALF_ROPE], kg[HALF_ROPE:ROPE_DIM]
        kr = jnp.concatenate([k1 * cos_t - k2 * sin_t, k2 * cos_t + k1 * sin_t], axis=0).astype(bf16)
        s = big[MLA_HEADS * NOPE_DIM:] + _mm(qr_scr[...], kr)
        s = (s.reshape(MLA_HEADS, t_new, n) * r[:, None, :]).reshape(nrow, n)
        if mask is not None:
            s = jnp.where(mask, s, NEG_INF)
        m_old = m_scr[...]
        m_new = jnp.maximum(m_old, jnp.max(s, axis=-1, keepdims=True))
        alpha = jnp.exp(m_old - m_new)
        p = jnp.exp(s - m_new)
        l_scr[...] = l_scr[...] * alpha + jnp.sum(p, axis=-1, keepdims=True)
        acc_scr[...] = acc_scr[...] * alpha + _mm(p.astype(bf16), cb)
        m_scr[...] = m_new

    for i in range(pp):
        cb_scr[i * PAGE_SIZE:(i + 1) * PAGE_SIZE, :] = c_pages[i][...].astype(bf16)
        kt_scr[:, i * PAGE_SIZE:(i + 1) * PAGE_SIZE] = kpe_pages[i][...]
    attend(cb_scr[...], kt_scr[...], cost_ref[...], sint_ref[...], None)

    @pl.when(j == pl.num_programs(1) - 1)
    def _():
        pad = PAGE_SIZE - t_new
        cn = jnp.concatenate([cnew_ref[...], jnp.zeros((pad, KV_RANK), f32)], axis=0).astype(bf16)
        kn = jnp.concatenate([kpenew_ref[...], jnp.zeros((pad, ROPE_DIM), f32)], axis=0).T
        key = lax.broadcasted_iota(jnp.int32, (nrow, PAGE_SIZE), 1)
        qry = lax.broadcasted_iota(jnp.int32, (nrow, PAGE_SIZE), 0) % t_new
        attend(cn, kn, costn_ref[...], sintn_ref[...], key <= qry)
        lat = acc_scr[...] / l_scr[...]
        for h in range(MLA_HEADS):
            contrib = _mm(lat[h * t_new:(h + 1) * t_new].astype(bf16), wuv_ref[h])
            ps = slice((h // 2) * LANES, (h // 2 + 1) * LANES)
            if h % 2 == 0:
                o_ref[:, ps] = contrib
            else:
                o_ref[:, ps] += contrib


def _mla_sample(layer, q, c_new, kpe_new, cache_ckv, cache_kpe_t, page_table, w, tabs_t, pp):
    ndb, n_pages = page_table.shape
    t_new = q.shape[0] // ndb
    npg = n_pages // pp
    kb = pp * PAGE_SIZE
    nrow = MLA_HEADS * t_new
    cost, sint, costn, sintn = tabs_t

    def page_spec(i, shape):
        return pl.BlockSpec((None, None) + shape, lambda b, j, pt: (layer, pt[b * n_pages + j * pp + i], 0, 0))

    fix2 = lambda b, j, pt: (0, 0)
    fix3 = lambda b, j, pt: (0, 0, 0)
    tok = lambda b, j, pt: (b, 0)
    in_specs = [pl.BlockSpec((t_new, QK_PACKED), tok),
                pl.BlockSpec((t_new, KV_RANK), tok),
                pl.BlockSpec((t_new, ROPE_DIM), tok),
                pl.BlockSpec(w['w_uk_t'].shape, fix2),
                pl.BlockSpec(w['w_uk_g'].shape, fix3),
                pl.BlockSpec(w['g_k_rope'].shape, fix2),
                pl.BlockSpec((HALF_ROPE, kb), lambda b, j, pt: (0, j)),
                pl.BlockSpec((HALF_ROPE, kb), lambda b, j, pt: (0, j)),
                pl.BlockSpec(costn.shape, fix2),
                pl.BlockSpec(sintn.shape, fix2),
                pl.BlockSpec(w['w_uv_pairs'].shape, fix3)]
    in_specs += [page_spec(i, (PAGE_SIZE, KV_RANK)) for i in range(pp)]
    in_specs += [page_spec(i, (ROPE_DIM, PAGE_SIZE)) for i in range(pp)]
    return pl.pallas_call(
        functools.partial(_mla_sample_body, pp=pp, t_new=t_new),
        grid_spec=pltpu.PrefetchScalarGridSpec(
            num_scalar_prefetch=1,
            grid=(ndb, npg),
            in_specs=in_specs,
            out_specs=pl.BlockSpec((t_new, MLA_WIDTH), tok),
            scratch_shapes=[pltpu.VMEM((MLA_HEADS * NOPE_DIM + nrow, KV_RANK), bf16),
                            pltpu.VMEM((nrow, ROPE_DIM), bf16),
                            pltpu.VMEM((kb, KV_RANK), bf16),
                            pltpu.VMEM((ROPE_DIM, kb), f32),
                            pltpu.VMEM((nrow, KV_RANK), f32),
                            pltpu.VMEM((nrow, 1), f32),
                            pltpu.VMEM((nrow, 1), f32)]),
        out_shape=jax.ShapeDtypeStruct((ndb * t_new, MLA_WIDTH), f32),
        compiler_params=_cparams("arbitrary", "arbitrary"),
        name="mla_sample",
    )(page_table.reshape(-1), q, c_new, kpe_new, w['w_uk_t'], w['w_uk_g'], w['g_k_rope'], cost, sint, costn, sintn,
      w['w_uv_pairs'], *([cache_ckv] * pp), *([cache_kpe_t] * pp))


def _split3(x):
    hi = x.astype(bf16)
    r1 = x - hi.astype(f32)
    mid = r1.astype(bf16)
    lo = (r1 - mid.astype(f32)).astype(bf16)
    return hi, mid, lo


def _gla_body(*refs, nb, tt, cs, has_init):
    if has_init:
        q_ref, k_ref, v_ref, g_ref, s0_ref, o_ref, sT_out, st_scr = refs
    else:
        q_ref, k_ref, v_ref, g_ref, o_ref, sT_out, st_scr = refs
    ti = pl.program_id(1)
    nchunk = tt // cs

    @pl.when(ti == 0)
    def _():
        if has_init:
            st_scr[...] = s0_ref[...]
        else:
            st_scr[...] = jnp.zeros(st_scr.shape, f32)

    tri = (lax.broadcasted_iota(jnp.int32, (tt, tt), 1) <= lax.broadcasted_iota(jnp.int32, (tt, tt), 0)).astype(bf16)
    kv_head = (lax.broadcasted_iota(jnp.int32, (GLA_KDIM, GLA_WIDTH), 0) // GLA_DK
               == lax.broadcasted_iota(jnp.int32, (GLA_KDIM, GLA_WIDTH), 1) // GLA_DV)
    block_ones = kv_head.astype(bf16)
    vk_head = (lax.broadcasted_iota(jnp.int32, (GLA_WIDTH, GLA_KDIM), 0) // GLA_DV
               == lax.broadcasted_iota(jnp.int32, (GLA_WIDTH, GLA_KDIM), 1) // GLA_DK)
    sel_t = (lax.broadcasted_iota(jnp.int32, (cs, cs * cs), 1) // cs
             == lax.broadcasted_iota(jnp.int32, (cs, cs * cs), 0)).astype(bf16)
    t_idx = lax.broadcasted_iota(jnp.int32, (cs, cs, GLA_KDIM), 0)
    s_idx = lax.broadcasted_iota(jnp.int32, (cs, cs, GLA_KDIM), 1)
    causal3 = s_idx <= t_idx

    for b in range(nb):
        q, k, v, g = q_ref[b], k_ref[b], v_ref[b], g_ref[b]
        hi, mid, lo = _split3(g)
        cum = _mm(tri, hi) + _mm(tri, mid) + _mm(tri, lo)
        for n in range(nchunk):
            sl = slice(n * cs, (n + 1) * cs)
            base = cum[n * cs - 1:n * cs] if n > 0 else jnp.zeros((1, GLA_KDIM), f32)
            bc = cum[sl] - base
            b_last = bc[cs - 1:cs]
            qc, kc, vc = q[sl], k[sl], v[sl]
            st = st_scr[b]
            o_inter = _nt((qc * jnp.exp(bc)).astype(bf16), st.astype(bf16))
            diff = jnp.where(causal3, bc[:, None, :] - bc[None, :, :], NEG_INF)
            d3 = qc[:, None, :] * kc[None, :, :] * jnp.exp(diff)
            a_exp = _mm(d3.reshape(cs * cs, GLA_KDIM).astype(bf16), block_ones)
            xv = (a_exp.reshape(cs, cs, GLA_WIDTH) * vc[None, :, :]).reshape(cs * cs, GLA_WIDTH)
            o_intra = _mm(sel_t, xv.astype(bf16))
            o_ref[b, sl, :] = o_inter + o_intra
            kd = (kc * jnp.exp(b_last - bc)).astype(bf16)
            upd = _tn(vc.astype(bf16), kd)
            st_scr[b] = st * jnp.exp(b_last) + jnp.where(vk_head, upd, 0.0)

    @pl.when(ti == pl.num_programs(1) - 1)
    def _():
        sT_out[...] = st_scr[...]


def _gla(q, k, v, g, s0t, nb, tt, cs):
    B, L, _ = q.shape
    has_init = s0t is not None
    tok = lambda bi, ti: (bi, ti, 0)
    st = lambda bi, ti: (bi, 0, 0)
    in_specs = [pl.BlockSpec((nb, tt, GLA_KDIM), tok), pl.BlockSpec((nb, tt, GLA_KDIM), tok),
                pl.BlockSpec((nb, tt, GLA_WIDTH), tok), pl.BlockSpec((nb, tt, GLA_KDIM), tok)]
    args = [q, k, v, g]
    if has_init:
        in_specs.append(pl.BlockSpec((nb, GLA_WIDTH, GLA_KDIM), st))
        args.append(s0t)
    return pl.pallas_call(
        functools.partial(_gla_body, nb=nb, tt=tt, cs=cs, has_init=has_init),
        grid=(B // nb, L // tt),
        in_specs=in_specs,
        out_specs=[pl.BlockSpec((nb, tt, GLA_WIDTH), tok), pl.BlockSpec((nb, GLA_WIDTH, GLA_KDIM), st)],
        out_shape=[jax.ShapeDtypeStruct((B, L, GLA_WIDTH), f32), jax.ShapeDtypeStruct((B, GLA_WIDTH, GLA_KDIM), f32)],
        scratch_shapes=[pltpu.VMEM((nb, GLA_WIDTH, GLA_KDIM), f32)],
        compiler_params=_cparams("arbitrary", "arbitrary"),
        name="gla",
    )(*args)


def _memkv_body(mem_ref, gmem_ref, w_ref, gk_ref, mkt_out, mvt_out):
    xb = _rms_rows(mem_ref[...], gmem_ref[...]).astype(bf16)
    kv = _mm(xb, w_ref[...])
    lane = lax.broadcasted_iota(jnp.int32, (1, LANES), 1)
    gk = gk_ref[...]
    for j in range(MEM_WIDTH // LANES):
        blk = _half_head_rms(kv[:, j * LANES:(j + 1) * LANES], gk, lane)
        mkt_out[j * LANES:(j + 1) * LANES, :] = blk.T
        mvt_out[j * LANES:(j + 1) * LANES, :] = kv[:, MEM_WIDTH + j * LANES:MEM_WIDTH + (j + 1) * LANES].T


def _memkv(mem2, w, nb, n_mem):
    row = lambda i: (i, 0)
    fix = lambda i: (0, 0)
    return pl.pallas_call(
        _memkv_body,
        grid=(nb,),
        in_specs=[pl.BlockSpec((n_mem, D_MODEL), row), pl.BlockSpec(w['g_mem'].shape, fix),
                  pl.BlockSpec(w['w_mem_kv'].shape, fix), pl.BlockSpec(w['g_mem_k'].shape, fix)],
        out_specs=[pl.BlockSpec((None, MEM_WIDTH, n_mem), lambda i: (i, 0, 0))] * 2,
        out_shape=[jax.ShapeDtypeStruct((nb, MEM_WIDTH, n_mem), f32)] * 2,
        compiler_params=_cparams("arbitrary"),
        name="memkv",
    )(mem2, w['g_mem'], w['w_mem_kv'], w['g_mem_k'])


def _memattn_body(q_ref, mkt_ref, mvt_ref, o_ref, *, nb, tq):
    lane_head = lax.broadcasted_iota(jnp.int32, (1, MEM_WIDTH), 1) // MEM_HEAD_DIM
    for b in range(nb):
        q = q_ref[b]
        mkt = mkt_ref[b].astype(bf16)
        mvt = mvt_ref[b].astype(bf16)
        qs = jnp.concatenate([jnp.where(lane_head == h, q, jnp.zeros_like(q)) for h in range(MEM_HEADS)], axis=0)
        s = _mm(qs, mkt)
        p = jnp.exp(s - jnp.max(s, axis=-1, keepdims=True))
        pv = _nt(p.astype(bf16), mvt) / jnp.sum(p, axis=-1, keepdims=True)
        o = jnp.zeros((tq, MEM_WIDTH), f32)
        for h in range(MEM_HEADS):
            o = o + jnp.where(lane_head == h, pv[h * tq:(h + 1) * tq], 0.0)
        o_ref[b] = o


def _memattn(layer, q3, mkt4, mvt4, nb, tq):
    B, L, _ = q3.shape
    n_mem = mkt4.shape[-1]
    kv_spec = pl.BlockSpec((None, nb, MEM_WIDTH, n_mem), lambda b, i: (layer, b, 0, 0))
    return pl.pallas_call(
        functools.partial(_memattn_body, nb=nb, tq=tq),
        grid=(B // nb, L // tq),
        in_specs=[pl.BlockSpec((nb, tq, MEM_WIDTH), lambda b, i: (b, i, 0)), kv_spec, kv_spec],
        out_specs=pl.BlockSpec((nb, tq, MEM_WIDTH), lambda b, i: (b, i, 0)),
        out_shape=jax.ShapeDtypeStruct((B, L, MEM_WIDTH), f32),
        compiler_params=_cparams("arbitrary", "arbitrary"),
        name="memattn",
    )(q3, mkt4, mvt4)


def _merge_body(x_ref, mla_ref, gla_ref, mem_ref, gates_ref, ggla_ref, wout_ref, y_ref):
    lane = lax.broadcasted_iota(jnp.int32, (1, LANES), 1)
    ggla = ggla_ref[...]
    y = x_ref[...]
    m1 = (gates_ref[:, 0:MLA_WIDTH] * mla_ref[...]).astype(bf16)
    y = y + _mm(m1, wout_ref[0:MLA_WIDTH, :])
    for j in range(GLA_WIDTH // LANES):
        ls = slice(j * LANES, (j + 1) * LANES)
        gn = _half_head_rms(gla_ref[:, ls], ggla, lane)
        gs = slice(MLA_WIDTH + j * LANES, MLA_WIDTH + (j + 1) * LANES)
        y = y + _mm((gates_ref[:, gs] * gn).astype(bf16), wout_ref[gs, :])
    ms = slice(MLA_WIDTH + GLA_WIDTH, D_MIX)
    m3 = (gates_ref[:, ms] * mem_ref[...]).astype(bf16)
    y_ref[...] = y + _mm(m3, wout_ref[ms, :])


def _merge(x2, mla_o, gla_o, mem_o, gates, w, tm):
    T = x2.shape[0]
    row = lambda i: (i, 0)
    fix = lambda i: (0, 0)
    return pl.pallas_call(
        _merge_body,
        grid=(T // tm,),
        in_specs=[pl.BlockSpec((tm, D_MODEL), row), pl.BlockSpec((tm, MLA_WIDTH), row),
                  pl.BlockSpec((tm, GLA_WIDTH), row), pl.BlockSpec((tm, MEM_WIDTH), row),
                  pl.BlockSpec((tm, D_MIX), row), pl.BlockSpec(w['g_gla_o'].shape, fix),
                  pl.BlockSpec(w['w_out'].shape, fix)],
        out_specs=pl.BlockSpec((tm, D_MODEL), row),
        out_shape=jax.ShapeDtypeStruct((T, D_MODEL), f32),
        compiler_params=_cparams("arbitrary"),
        name="merge",
    )(x2, mla_o, gla_o, mem_o, gates, w['g_gla_o'], w['w_out'])


def _pad_head(a):
    return jnp.pad(a, [(0, 0)] * (a.ndim - 1) + [(0, HEAD_PAD - QK_DIM)])


def _prep_layer(l, p):
    w_in = p['w_in'][l]
    cuts, o = [], 0
    for n in (Q_RANK, KV_RANK, ROPE_DIM, MLA_WIDTH, GLA_KDIM, GLA_KDIM, GLA_WIDTH, GLA_GATE_RANK, GLA_WIDTH,
              MEM_WIDTH, MEM_WIDTH):
        cuts.append(w_in[:, o:o + n])
        o += n
    cq, ckv, kpe, gate_mla, gq, gk, gv, gg, gate_gla, mq, gate_mem = cuts
    z = lambda n: jnp.zeros((D_MODEL, n), f32)
    misc = jnp.concatenate([gg, z(KPE_LANE - GLA_GATE_RANK), kpe, z(LANES - KPE_LANE - ROPE_DIM)], axis=1)
    w_in_packed = jnp.concatenate([cq, ckv, gate_mla, gq, gk, gv, gate_gla, mq, gate_mem, misc], axis=1)

    w_uk = p['w_uk'][l]
    g_k = p['g_mla_k'][l]
    wuk_packed = jnp.pad(w_uk, ((0, 0), (0, 0), (0, HEAD_PAD - NOPE_DIM))).reshape(KV_RANK, QK_PACKED)
    w_uk_t = w_uk.reshape(KV_RANK, MLA_HEADS * NOPE_DIM).T
    w_uk_g = jnp.pad((w_uk * g_k[None, None, :NOPE_DIM]).transpose(1, 2, 0),
                     ((0, 0), (0, HEAD_PAD - NOPE_DIM), (0, 0)))
    w_uv = p['w_uv'][l].transpose(1, 0, 2)
    zeros = jnp.zeros_like(w_uv)
    even = jnp.arange(MLA_HEADS)[:, None, None] % 2 == 0
    w_uv_pairs = jnp.concatenate([jnp.where(even, w_uv, zeros), jnp.where(even, zeros, w_uv)], axis=-1)
    w_gk = jnp.pad(p['w_gk'][l], ((0, LANES - GLA_GATE_RANK), (0, 0)))
    row = lambda a: a.reshape(1, -1)
    return {
        'g_pre': row(p['g_pre'][l]), 'w_in': w_in_packed.astype(bf16), 'g_qa': row(p['g_qa'][l]),
        'w_uq': _pad_head(p['w_uq'][l]).reshape(Q_RANK, QK_PACKED).astype(bf16),
        'g_kva': row(p['g_kva'][l]), 'w_uk': wuk_packed.astype(bf16),
        'g_q': row(_pad_head(p['g_mla_q'][l])), 'g_k': row(_pad_head(g_k)),
        'w_gk': w_gk.astype(bf16), 'b_gk': row(p['b_gk'][l]),
        'g_mem_q': row(jnp.tile(p['g_mem_q'][l], 2)), 'g_mem_k': row(jnp.tile(p['g_mem_k'][l], 2)),
        'g_gla_o': row(jnp.tile(p['g_gla_o'][l], 2)),
        'w_uk_t': w_uk_t.astype(bf16), 'w_uk_g': w_uk_g.astype(bf16),
        'g_k_rope': g_k[NOPE_DIM:].reshape(ROPE_DIM, 1),
        'w_uv_pairs': w_uv_pairs.astype(bf16),
        'w_uv_t': w_uv.transpose(0, 2, 1).astype(bf16),
        'g_mem': row(p['g_mem'][l]), 'w_mem_kv': p['w_mem_kv'][l].astype(bf16),
        'w_out': p['w_out'][l].astype(bf16),
    }


def _angles(pos):
    inv_freq = ROPE_THETA ** (-jnp.arange(HALF_ROPE, dtype=f32) * (2.0 / ROPE_DIM))
    return pos.astype(f32)[:, None] * inv_freq[None, :]


def _packed_tables(pos):
    ang = _angles(pos)
    n = pos.shape[0]
    cos, sin = jnp.cos(ang), jnp.sin(ang)
    one = jnp.ones((n, NOPE_DIM), f32)
    z = lambda w: jnp.zeros((n, w), f32)
    tail = HEAD_PAD - QK_DIM
    cos_p = jnp.concatenate([one, cos, cos, z(tail)], axis=1)
    sin_up = jnp.concatenate([z(NOPE_DIM + HALF_ROPE), sin, z(tail)], axis=1)
    sin_dn = jnp.concatenate([z(NOPE_DIM), -sin, z(HALF_ROPE + tail)], axis=1)
    return cos_p, sin_up, sin_dn


def _transposed_tables(pos):
    ang = _angles(pos).T
    return jnp.cos(ang), jnp.sin(ang)


def _state_to_blockdiag_t(s):
    B = s.shape[0]
    st = s.transpose(0, 1, 3, 2)
    eye = jnp.eye(GLA_HEADS, dtype=s.dtype)
    return jnp.einsum('bhvk,hg->bhvgk', st, eye).reshape(B, GLA_WIDTH, GLA_KDIM)


def _blockdiag_t_to_state(st):
    B = st.shape[0]
    s5 = st.reshape(B, GLA_HEADS, GLA_DV, GLA_HEADS, GLA_DK)
    diag = jnp.stack([s5[:, h, :, h, :] for h in range(GLA_HEADS)], axis=1)
    return diag.transpose(0, 1, 3, 2)


def kernel(x_prompt, x_sample, mem_prompt, cache_ckv, cache_kpe, page_table, state_gla, cache_mem_k, cache_mem_v, g_pre, w_in, g_qa, w_uq, g_kva, w_uk, w_uv, g_mla_q, g_mla_k, w_gk, b_gk, g_gla_o, g_mem, w_mem_kv, g_mem_q, g_mem_k, w_out):
    params = dict(g_pre=g_pre, w_in=w_in, g_qa=g_qa, w_uq=w_uq, g_kva=g_kva, w_uk=w_uk, w_uv=w_uv,
                  g_mla_q=g_mla_q, g_mla_k=g_mla_k, w_gk=w_gk, b_gk=b_gk, g_gla_o=g_gla_o, g_mem=g_mem,
                  w_mem_kv=w_mem_kv, g_mem_q=g_mem_q, g_mem_k=g_mem_k, w_out=w_out)
    nb, seq, _ = x_prompt.shape
    ndb, t_new, _ = x_sample.shape
    n_mem = mem_prompt.shape[1]
    depth = w_in.shape[0]
    n_pages = page_table.shape[1]
    past_len = n_pages * cache_ckv.shape[2]

    tm_p = min(256, seq)
    tq = min(256, seq)
    tm_s = min(256, ndb * t_new)
    pp = min(16, n_pages)
    gla_tt = min(128, seq)
    gla_cs = min(16, gla_tt)
    nb_s = min(8, ndb)

    cache_kpe_t = cache_kpe.transpose(0, 1, 3, 2)
    to_t = lambda a: a.transpose(0, 1, 3, 4, 2).reshape(depth, ndb, MEM_WIDTH, n_mem)
    cache_mkt, cache_mvt = to_t(cache_mem_k), to_t(cache_mem_v)
    from_t = lambda a: a.reshape(nb, MEM_HEADS, MEM_HEAD_DIM, n_mem).transpose(0, 3, 1, 2)

    tabs_p = _packed_tables(jnp.arange(seq, dtype=jnp.int32))
    pos_s = past_len + jnp.arange(t_new, dtype=jnp.int32)
    tabs_s = tuple(jnp.tile(t, (tm_s // t_new, 1)) for t in _packed_tables(pos_s))
    tabs_t = _transposed_tables(jnp.arange(past_len, dtype=jnp.int32)) \
        + _transposed_tables(past_len + jnp.arange(PAGE_SIZE, dtype=jnp.int32))

    xp = x_prompt.reshape(nb * seq, D_MODEL)
    xs = x_sample.reshape(ndb * t_new, D_MODEL)
    mem2 = mem_prompt.reshape(nb * n_mem, D_MODEL)
    ckv_p, kpe_p, gla_p, mk_p, mv_p, ckv_s, kpe_s, gla_s = ([] for _ in range(8))
    for l in range(depth):
        w = _prep_layer(l, params)
        q, k, c, kpe, gates, gq, gk, gv, gg, mq = _inproj(xp, tabs_p, w, tm_p)
        mla_o = _mla_prompt(q, k, c, w['w_uv_t'], nb, seq, tq)
        r3 = lambda a: a.reshape(nb, seq, a.shape[-1])
        gla_o, st = _gla(r3(gq), r3(gk), r3(gv), r3(gg), None, 1, gla_tt, gla_cs)
        mkt, mvt = _memkv(mem2, w, nb, n_mem)
        mem_o = _memattn(0, r3(mq), mkt[None], mvt[None], 1, tq)
        xp = _merge(xp, mla_o, gla_o.reshape(nb * seq, GLA_WIDTH), mem_o.reshape(nb * seq, MEM_WIDTH), gates, w, tm_p)
        ckv_p.append(c.reshape(nb, seq, KV_RANK))
        kpe_p.append(kpe.reshape(nb, seq, ROPE_DIM))
        gla_p.append(_blockdiag_t_to_state(st))
        mk_p.append(from_t(mkt))
        mv_p.append(from_t(mvt))
        q, k, c, kpe, gates, gq, gk, gv, gg, mq = _inproj(xs, tabs_s, w, tm_s)
        mla_o = _mla_sample(l, q, c, kpe, cache_ckv, cache_kpe_t, page_table, w, tabs_t, pp)
        r3 = lambda a: a.reshape(ndb, t_new, a.shape[-1])
        gla_o, st = _gla(r3(gq), r3(gk), r3(gv), r3(gg), _state_to_blockdiag_t(state_gla[l]), nb_s, t_new, t_new)
        mem_o = _memattn(l, r3(mq), cache_mkt, cache_mvt, nb_s, t_new)
        xs = _merge(xs, mla_o, gla_o.reshape(ndb * t_new, GLA_WIDTH), mem_o.reshape(ndb * t_new, MEM_WIDTH), gates, w, tm_s)
        ckv_s.append(c.reshape(ndb, t_new, KV_RANK))
        kpe_s.append(kpe.reshape(ndb, t_new, ROPE_DIM))
        gla_s.append(_blockdiag_t_to_state(st))
    return (xp.reshape(nb, seq, D_MODEL), xs.reshape(ndb, t_new, D_MODEL), jnp.stack(ckv_p), jnp.stack(kpe_p),
            jnp.stack(gla_p), jnp.stack(mk_p), jnp.stack(mv_p), jnp.stack(ckv_s), jnp.stack(kpe_s), jnp.stack(gla_s))
```

```python
import functools

import jax
import jax.numpy as jnp
from jax import lax
from jax.experimental import pallas as pl
from jax.experimental.pallas import tpu as pltpu

f32, bf16 = jnp.float32, jnp.bfloat16

D_MODEL = 1024
PAGE_SIZE = 128
MLA_HEADS = 8
NOPE_DIM = 64
ROPE_DIM = 32
HALF_ROPE = ROPE_DIM // 2
QK_DIM = NOPE_DIM + ROPE_DIM
V_DIM = 64
Q_RANK = 384
KV_RANK = 256
MLA_WIDTH = MLA_HEADS * V_DIM
ROPE_THETA = 10000.0
MLA_SCALE = QK_DIM ** -0.5
GLA_HEADS = 4
GLA_DK = 32
GLA_DV = 64
GLA_KDIM = GLA_HEADS * GLA_DK
GLA_WIDTH = GLA_HEADS * GLA_DV
GLA_GATE_RANK = 16
GLA_NORMALIZER = 16.0
MEM_HEADS = 4
MEM_HEAD_DIM = 64
MEM_WIDTH = MEM_HEADS * MEM_HEAD_DIM
MEM_SCALE = MEM_HEAD_DIM ** -0.5
D_MIX = MLA_WIDTH + GLA_WIDTH + MEM_WIDTH
EPS = 1e-6
NEG_INF = -1e30

LANES = 128
HEAD_PAD = LANES
QK_PACKED = MLA_HEADS * HEAD_PAD

OFF_CQ = 0
OFF_CKV = OFF_CQ + Q_RANK
OFF_GATE_MLA = OFF_CKV + KV_RANK
OFF_GQ = OFF_GATE_MLA + MLA_WIDTH
OFF_GK = OFF_GQ + GLA_KDIM
OFF_GV = OFF_GK + GLA_KDIM
OFF_GATE_GLA = OFF_GV + GLA_WIDTH
OFF_MQ = OFF_GATE_GLA + GLA_WIDTH
OFF_GATE_MEM = OFF_MQ + MEM_WIDTH
OFF_MISC = OFF_GATE_MEM + MEM_WIDTH
D_IN_PACKED = OFF_MISC + LANES
KPE_LANE = NOPE_DIM

VMEM_LIMIT = 56 * 1024 * 1024


def _cparams(*sem):
    return pltpu.CompilerParams(dimension_semantics=sem, vmem_limit_bytes=VMEM_LIMIT)


def _nt(a, b):
    return lax.dot_general(a, b, (((1,), (1,)), ((), ())), preferred_element_type=f32)


def _tn(a, b):
    return lax.dot_general(a, b, (((0,), (0,)), ((), ())), preferred_element_type=f32)


def _mm(a, b):
    return jnp.dot(a, b, preferred_element_type=f32)


def _rms_rows(x, g):
    return x * lax.rsqrt(jnp.mean(x * x, axis=-1, keepdims=True) + EPS) * g


def _half_head_rms(blk, g, lane):
    sq = blk * blk
    lo = lane < 64
    ss_lo = jnp.sum(jnp.where(lo, sq, 0.0), axis=-1, keepdims=True)
    ss_hi = jnp.sum(jnp.where(lo, 0.0, sq), axis=-1, keepdims=True)
    ss = jnp.where(lo, ss_lo, ss_hi)
    return blk * lax.rsqrt(ss * (1.0 / 64.0) + EPS) * g


def _rope_packed(x, cos, sin_up, sin_dn):
    return x * cos + pltpu.roll(x, HALF_ROPE, 1) * sin_up + pltpu.roll(x, LANES - HALF_ROPE, 1) * sin_dn


def _inproj_body(x_ref, gpre_ref, win_ref, gqa_ref, wuq_ref, gkva_ref, wuk_ref, gq_ref, gk_ref,
                 cos_ref, sup_ref, sdn_ref, wgk_ref, bgk_ref, gmq_ref,
                 q_out, k_out, ckv_out, kpe_out, gates_out, glaq_out, glak_out, glav_out, glag_out, memq_out):
    x = x_ref[...]
    xb = _rms_rows(x, gpre_ref[...]).astype(bf16)

    def seg(off, n):
        return _mm(xb, win_ref[:, off:off + n])

    cos, sup, sdn = cos_ref[...], sup_ref[...], sdn_ref[...]
    lane = lax.broadcasted_iota(jnp.int32, (1, LANES), 1)

    cq = _rms_rows(seg(OFF_CQ, Q_RANK), gqa_ref[...]).astype(bf16)
    qf = _mm(cq, wuq_ref[...])
    gq = gq_ref[...]
    for h in range(MLA_HEADS):
        qh = qf[:, h * HEAD_PAD:(h + 1) * HEAD_PAD]
        ss = jnp.sum(qh * qh, axis=-1, keepdims=True)
        qn = qh * lax.rsqrt(ss * (1.0 / QK_DIM) + EPS) * gq
        q_out[:, h * HEAD_PAD:(h + 1) * HEAD_PAD] = (_rope_packed(qn, cos, sup, sdn) * MLA_SCALE).astype(bf16)

    c = _rms_rows(seg(OFF_CKV, KV_RANK), gkva_ref[...])
    ckv_out[...] = c
    misc = seg(OFF_MISC, LANES)
    kpe_wide = jnp.where((lane >= KPE_LANE) & (lane < KPE_LANE + ROPE_DIM), misc, 0.0)
    kpe_out[...] = misc[:, KPE_LANE:KPE_LANE + ROPE_DIM]
    kf = _mm(c.astype(bf16), wuk_ref[...])
    gk = gk_ref[...]
    for h in range(MLA_HEADS):
        kh = kf[:, h * HEAD_PAD:(h + 1) * HEAD_PAD] + kpe_wide
        ss = jnp.sum(kh * kh, axis=-1, keepdims=True)
        kn = kh * lax.rsqrt(ss * (1.0 / QK_DIM) + EPS) * gk
        k_out[:, h * HEAD_PAD:(h + 1) * HEAD_PAD] = _rope_packed(kn, cos, sup, sdn).astype(bf16)

    g1 = seg(OFF_GATE_MLA, MLA_WIDTH)
    gates_out[:, 0:MLA_WIDTH] = g1 * jax.nn.sigmoid(g1)
    g2 = seg(OFF_GATE_GLA, GLA_WIDTH)
    gates_out[:, MLA_WIDTH:MLA_WIDTH + GLA_WIDTH] = g2 * jax.nn.sigmoid(g2)
    g3 = seg(OFF_GATE_MEM, MEM_WIDTH)
    gates_out[:, MLA_WIDTH + GLA_WIDTH:D_MIX] = g3 * jax.nn.sigmoid(g3)

    glaq_out[...] = seg(OFF_GQ, GLA_KDIM) * (GLA_DK ** -0.5)
    glak_out[...] = seg(OFF_GK, GLA_KDIM)
    glav_out[...] = seg(OFF_GV, GLA_WIDTH)
    gl = _mm(misc.astype(bf16), wgk_ref[...]) + bgk_ref[...]
    glag_out[...] = jax.nn.log_sigmoid(gl) * (1.0 / GLA_NORMALIZER)

    mq = seg(OFF_MQ, MEM_WIDTH)
    gmq = gmq_ref[...]
    for j in range(MEM_WIDTH // LANES):
        blk = _half_head_rms(mq[:, j * LANES:(j + 1) * LANES], gmq, lane)
        memq_out[:, j * LANES:(j + 1) * LANES] = (blk * MEM_SCALE).astype(bf16)


def _inproj(x2, tabs, w, tm):
    T = x2.shape[0]
    nt = tabs[0].shape[0] // tm
    row = lambda i: (i, 0)
    fix = lambda i: (0, 0)
    tab = lambda i: (i % nt, 0)

    def full(a):
        return pl.BlockSpec(a.shape, fix)

    consts1 = (w['g_pre'], w['w_in'], w['g_qa'], w['w_uq'], w['g_kva'], w['w_uk'], w['g_q'], w['g_k'])
    consts2 = (w['w_gk'], w['b_gk'], w['g_mem_q'])
    outs = [(QK_PACKED, bf16), (QK_PACKED, bf16), (KV_RANK, f32), (ROPE_DIM, f32), (D_MIX, f32),
            (GLA_KDIM, f32), (GLA_KDIM, f32), (GLA_WIDTH, f32), (GLA_KDIM, f32), (MEM_WIDTH, bf16)]
    return pl.pallas_call(
        _inproj_body,
        grid=(T // tm,),
        in_specs=[pl.BlockSpec((tm, D_MODEL), row)] + [full(a) for a in consts1]
        + [pl.BlockSpec((tm, LANES), tab)] * 3 + [full(a) for a in consts2],
        out_specs=[pl.BlockSpec((tm, n), row) for n, _ in outs],
        out_shape=[jax.ShapeDtypeStruct((T, n), dt) for n, dt in outs],
        compiler_params=_cparams("arbitrary"),
        name="inproj",
    )(x2, *consts1, *tabs, *consts2)


def _mla_prompt_body(q_ref, k_ref, c_ref, wuvt_ref, o_ref, ct_scr, acc_scr, m_scr, l_scr, ot_scr, *, tq):
    i = pl.program_id(1)
    nblk = ct_scr.shape[0]

    @pl.when(i == 0)
    def _():
        for j in range(nblk):
            ct_scr[j] = c_ref[j * tq:(j + 1) * tq, :].T.astype(bf16)

    m_scr[...] = jnp.full(m_scr.shape, NEG_INF, f32)
    l_scr[...] = jnp.zeros(l_scr.shape, f32)
    acc_scr[...] = jnp.zeros(acc_scr.shape, f32)
    key = lax.broadcasted_iota(jnp.int32, (tq, tq), 0)
    qry = lax.broadcasted_iota(jnp.int32, (tq, tq), 1)
    causal = key <= qry

    def step(kb, masked):
        ks = pl.ds(pl.multiple_of(kb * tq, tq), tq)
        ct = ct_scr[kb]
        for h in range(MLA_HEADS):
            hs = slice(h * HEAD_PAD, (h + 1) * HEAD_PAD)
            s = _nt(k_ref[ks, hs], q_ref[:, hs])
            if masked:
                s = jnp.where(causal, s, NEG_INF)
            m_old = m_scr[h:h + 1, :]
            m_new = jnp.maximum(m_old, jnp.max(s, axis=0, keepdims=True))
            alpha = jnp.exp(m_old - m_new)
            p = jnp.exp(s - m_new)
            l_scr[h:h + 1, :] = l_scr[h:h + 1, :] * alpha + jnp.sum(p, axis=0, keepdims=True)
            acc_scr[h] = acc_scr[h] * alpha + _mm(ct, p.astype(bf16))
            m_scr[h:h + 1, :] = m_new

    def body(kb, carry):
        step(kb, False)
        return carry

    lax.fori_loop(0, i, body, 0)
    step(i, True)

    for h in range(MLA_HEADS):
        lat_t = (acc_scr[h] / l_scr[h:h + 1, :]).astype(bf16)
        ot_scr[h * V_DIM:(h + 1) * V_DIM, :] = _mm(wuvt_ref[h], lat_t)
    o_ref[...] = ot_scr[...].T


def _mla_prompt(q, k, c, wuv_t, nb, seq, tq):
    nq = seq // tq
    return pl.pallas_call(
        functools.partial(_mla_prompt_body, tq=tq),
        grid=(nb, nq),
        in_specs=[pl.BlockSpec((tq, QK_PACKED), lambda b, i: (b * nq + i, 0)),
                  pl.BlockSpec((seq, QK_PACKED), lambda b, i: (b, 0)),
                  pl.BlockSpec((seq, KV_RANK), lambda b, i: (b, 0)),
                  pl.BlockSpec(wuv_t.shape, lambda b, i: (0, 0, 0))],
        out_specs=pl.BlockSpec((tq, MLA_WIDTH), lambda b, i: (b * nq + i, 0)),
        out_shape=jax.ShapeDtypeStruct((nb * seq, MLA_WIDTH), f32),
        scratch_shapes=[pltpu.VMEM((nq, KV_RANK, tq), bf16), pltpu.VMEM((MLA_HEADS, KV_RANK, tq), f32),
                        pltpu.VMEM((MLA_HEADS, tq), f32), pltpu.VMEM((MLA_HEADS, tq), f32),
                        pltpu.VMEM((MLA_WIDTH, tq), f32)],
        compiler_params=_cparams("arbitrary", "arbitrary"),
        name="mla_prompt",
    )(q, k, c, wuv_t)


def _mla_sample_body(pt_ref, q_ref, cnew_ref, kpenew_ref, wukt_ref, wukg_ref, gkr_ref, cost_ref, sint_ref,
                     costn_ref, sintn_ref, wuv_ref, ckv_hbm, kpe_hbm, o_ref,
                     cbuf, kbuf, sem, lhs_scr, qr_scr, cb_scr, acc_scr, m_scr, l_scr, *, layer, gp, n_pages):
    ndb, t_new, _ = q_ref.shape
    ng = n_pages // gp
    total = ndb * ng
    nrow = MLA_HEADS * t_new
    n_nope = MLA_HEADS * NOPE_DIM

    def group_copies(t, slot):
        base = (t // ng) * n_pages + (t % ng) * gp
        cps = []
        for i in range(gp):
            page = pt_ref[base + i]
            lanes = pl.ds(i * PAGE_SIZE, PAGE_SIZE)
            cps.append(pltpu.make_async_copy(ckv_hbm.at[layer, page], cbuf.at[slot, lanes, :], sem.at[0, slot]))
            cps.append(pltpu.make_async_copy(kpe_hbm.at[layer, page], kbuf.at[slot, :, lanes], sem.at[1, slot]))
        return cps

    gkr = gkr_ref[...]

    def attend(cb, kt, cos_t, sin_t, mask):
        n = cb.shape[0]
        big = _nt(lhs_scr[...], cb)
        kn = big[0:n_nope]
        ss_nope = jnp.sum((kn * kn).reshape(MLA_HEADS, NOPE_DIM, n), axis=1)
        ss = ss_nope + jnp.sum(kt * kt, axis=0, keepdims=True)
        r = lax.rsqrt(ss * (1.0 / QK_DIM) + EPS)
        kg = kt * gkr
        k1, k2 = kg[0:HALF_ROPE], kg[HALF_ROPE:ROPE_DIM]
        kr = jnp.concatenate([k1 * cos_t - k2 * sin_t, k2 * cos_t + k1 * sin_t], axis=0).astype(bf16)
        s = big[n_nope:] + _mm(qr_scr[...], kr)
        s = (s.reshape(MLA_HEADS, t_new, n) * r[:, None, :]).reshape(nrow, n)
        if mask is not None:
            s = jnp.where(mask, s, NEG_INF)
        m_old = m_scr[...]
        m_new = jnp.maximum(m_old, jnp.max(s, axis=-1, keepdims=True))
        alpha = jnp.exp(m_old - m_new)
        p = jnp.exp(s - m_new)
        l_scr[...] = l_scr[...] * alpha + jnp.sum(p, axis=-1, keepdims=True)
        acc_scr[...] = acc_scr[...] * alpha + _mm(p.astype(bf16), cb)
        m_scr[...] = m_new

    def start_batch(b):
        q_b = q_ref[b]
        for h in range(MLA_HEADS):
            q_h = q_b[:, h * HEAD_PAD:(h + 1) * HEAD_PAD]
            lhs_scr[n_nope + h * t_new:n_nope + (h + 1) * t_new, :] = _mm(q_h, wukg_ref[h]).astype(bf16)
            qr_scr[h * t_new:(h + 1) * t_new, :] = q_h[:, KPE_LANE:KPE_LANE + ROPE_DIM]
        m_scr[...] = jnp.full((nrow, 1), NEG_INF, f32)
        l_scr[...] = jnp.zeros((nrow, 1), f32)
        acc_scr[...] = jnp.zeros((nrow, KV_RANK), f32)

    def finish_batch(b):
        pad = PAGE_SIZE - t_new
        cn = jnp.concatenate([cnew_ref[b], jnp.zeros((pad, KV_RANK), f32)], axis=0).astype(bf16)
        kn = jnp.concatenate([kpenew_ref[b], jnp.zeros((pad, ROPE_DIM), f32)], axis=0).T
        key = lax.broadcasted_iota(jnp.int32, (nrow, PAGE_SIZE), 1)
        qry = lax.broadcasted_iota(jnp.int32, (nrow, PAGE_SIZE), 0) % t_new
        attend(cn, kn, costn_ref[...], sintn_ref[...], key <= qry)
        lat = acc_scr[...] / l_scr[...]
        for h in range(MLA_HEADS):
            contrib = _mm(lat[h * t_new:(h + 1) * t_new].astype(bf16), wuv_ref[h])
            ps = slice((h // 2) * LANES, (h // 2 + 1) * LANES)
            if h % 2 == 0:
                o_ref[b, :, ps] = contrib
            else:
                o_ref[b, :, ps] += contrib

    lhs_scr[0:n_nope, :] = wukt_ref[...]
    for cp in group_copies(0, 0):
        cp.start()

    def body(t, carry):
        b, g, slot = t // ng, t % ng, t % 2

        @pl.when(t + 1 < total)
        def _():
            for cp in group_copies(t + 1, 1 - slot):
                cp.start()

        @pl.when(g == 0)
        def _():
            start_batch(b)

        for cp in group_copies(t, slot):
            cp.wait()
        cb_scr[...] = cbuf[slot].astype(bf16)
        attend(cb_scr[...], kbuf[slot], cost_ref[g], sint_ref[g], None)

        @pl.when(g == ng - 1)
        def _():
            finish_batch(b)

        return carry

    lax.fori_loop(0, total, body, 0)


def _mla_sample(layer, q, c_new, kpe_new, cache_ckv, cache_kpe_t, page_table, w, tabs_t, gp):
    ndb, n_pages = page_table.shape
    t_new = q.shape[0] // ndb
    ng = n_pages // gp
    kb = gp * PAGE_SIZE
    nrow = MLA_HEADS * t_new
    cost, sint, costn, sintn = tabs_t
    by_group = lambda a: a.reshape(HALF_ROPE, ng, kb).transpose(1, 0, 2)
    args = (q.reshape(ndb, t_new, QK_PACKED), c_new.reshape(ndb, t_new, KV_RANK), kpe_new.reshape(ndb, t_new, ROPE_DIM),
            w['w_uk_t'], w['w_uk_g'], w['g_k_rope'], by_group(cost), by_group(sint), costn, sintn, w['w_uv_pairs'])

    def full(a):
        return pl.BlockSpec(a.shape, lambda i, pt, nd=a.ndim: (0,) * nd)

    out = pl.pallas_call(
        functools.partial(_mla_sample_body, layer=layer, gp=gp, n_pages=n_pages),
        grid_spec=pltpu.PrefetchScalarGridSpec(
            num_scalar_prefetch=1,
            grid=(1,),
            in_specs=[full(a) for a in args] + [pl.BlockSpec(memory_space=pl.ANY)] * 2,
            out_specs=pl.BlockSpec((ndb, t_new, MLA_WIDTH), lambda i, pt: (0, 0, 0)),
            scratch_shapes=[pltpu.VMEM((2, kb, KV_RANK), f32),
                            pltpu.VMEM((2, ROPE_DIM, kb), f32),
                            pltpu.SemaphoreType.DMA((2, 2)),
                            pltpu.VMEM((MLA_HEADS * NOPE_DIM + nrow, KV_RANK), bf16),
                            pltpu.VMEM((nrow, ROPE_DIM), bf16),
                            pltpu.VMEM((kb, KV_RANK), bf16),
                            pltpu.VMEM((nrow, KV_RANK), f32),
                            pltpu.VMEM((nrow, 1), f32),
                            pltpu.VMEM((nrow, 1), f32)]),
        out_shape=jax.ShapeDtypeStruct((ndb, t_new, MLA_WIDTH), f32),
        compiler_params=_cparams("arbitrary"),
        name="mla_sample",
    )(page_table.reshape(-1), *args, cache_ckv, cache_kpe_t)
    return out.reshape(ndb * t_new, MLA_WIDTH)


def _split3(x):
    hi = x.astype(bf16)
    r1 = x - hi.astype(f32)
    mid = r1.astype(bf16)
    lo = (r1 - mid.astype(f32)).astype(bf16)
    return hi, mid, lo


def _gla_body(*refs, nb, tt, cs, has_init):
    if has_init:
        q_ref, k_ref, v_ref, g_ref, s0_ref, o_ref, sT_out, st_scr = refs
    else:
        q_ref, k_ref, v_ref, g_ref, o_ref, sT_out, st_scr = refs
    ti = pl.program_id(1)
    nchunk = tt // cs

    @pl.when(ti == 0)
    def _():
        if has_init:
            st_scr[...] = s0_ref[...]
        else:
            st_scr[...] = jnp.zeros(st_scr.shape, f32)

    tri = (lax.broadcasted_iota(jnp.int32, (tt, tt), 1) <= lax.broadcasted_iota(jnp.int32, (tt, tt), 0)).astype(bf16)
    kv_head = (lax.broadcasted_iota(jnp.int32, (GLA_KDIM, GLA_WIDTH), 0) // GLA_DK
               == lax.broadcasted_iota(jnp.int32, (GLA_KDIM, GLA_WIDTH), 1) // GLA_DV)
    block_ones = kv_head.astype(bf16)
    vk_head = (lax.broadcasted_iota(jnp.int32, (GLA_WIDTH, GLA_KDIM), 0) // GLA_DV
               == lax.broadcasted_iota(jnp.int32, (GLA_WIDTH, GLA_KDIM), 1) // GLA_DK)
    sel_t = (lax.broadcasted_iota(jnp.int32, (cs, cs * cs), 1) // cs
             == lax.broadcasted_iota(jnp.int32, (cs, cs * cs), 0)).astype(bf16)
    t_idx = lax.broadcasted_iota(jnp.int32, (cs, cs, GLA_KDIM), 0)
    s_idx = lax.broadcasted_iota(jnp.int32, (cs, cs, GLA_KDIM), 1)
    causal3 = s_idx <= t_idx

    for b in range(nb):
        q, k, v, g = q_ref[b], k_ref[b], v_ref[b], g_ref[b]
        hi, mid, lo = _split3(g)
        cum = _mm(tri, hi) + _mm(tri, mid) + _mm(tri, lo)
        for n in range(nchunk):
            sl = slice(n * cs, (n + 1) * cs)
            base = cum[n * cs - 1:n * cs] if n > 0 else jnp.zeros((1, GLA_KDIM), f32)
            bc = cum[sl] - base
            b_last = bc[cs - 1:cs]
            qc, kc, vc = q[sl], k[sl], v[sl]
            st = st_scr[b]
            o_inter = _nt((qc * jnp.exp(bc)).astype(bf16), st.astype(bf16))
            diff = jnp.where(causal3, bc[:, None, :] - bc[None, :, :], NEG_INF)
            d3 = qc[:, None, :] * kc[None, :, :] * jnp.exp(diff)
            a_exp = _mm(d3.reshape(cs * cs, GLA_KDIM).astype(bf16), block_ones)
            xv = (a_exp.reshape(cs, cs, GLA_WIDTH) * vc[None, :, :]).reshape(cs * cs, GLA_WIDTH)
            o_intra = _mm(sel_t, xv.astype(bf16))
            o_ref[b, sl, :] = o_inter + o_intra
            kd = (kc * jnp.exp(b_last - bc)).astype(bf16)
            upd = _tn(vc.astype(bf16), kd)
            st_scr[b] = st * jnp.exp(b_last) + jnp.where(vk_head, upd, 0.0)

    @pl.when(ti == pl.num_programs(1) - 1)
    def _():
        sT_out[...] = st_scr[...]


def _gla(q, k, v, g, s0t, nb, tt, cs):
    B, L, _ = q.shape
    has_init = s0t is not None
    tok = lambda bi, ti: (bi, ti, 0)
    st = lambda bi, ti: (bi, 0, 0)
    in_specs = [pl.BlockSpec((nb, tt, GLA_KDIM), tok), pl.BlockSpec((nb, tt, GLA_KDIM), tok),
                pl.BlockSpec((nb, tt, GLA_WIDTH), tok), pl.BlockSpec((nb, tt, GLA_KDIM), tok)]
    args = [q, k, v, g]
    if has_init:
        in_specs.append(pl.BlockSpec((nb, GLA_WIDTH, GLA_KDIM), st))
        args.append(s0t)
    return pl.pallas_call(
        functools.partial(_gla_body, nb=nb, tt=tt, cs=cs, has_init=has_init),
        grid=(B // nb, L // tt),
        in_specs=in_specs,
        out_specs=[pl.BlockSpec((nb, tt, GLA_WIDTH), tok), pl.BlockSpec((nb, GLA_WIDTH, GLA_KDIM), st)],
        out_shape=[jax.ShapeDtypeStruct((B, L, GLA_WIDTH), f32), jax.ShapeDtypeStruct((B, GLA_WIDTH, GLA_KDIM), f32)],
        scratch_shapes=[pltpu.VMEM((nb, GLA_WIDTH, GLA_KDIM), f32)],
        compiler_params=_cparams("arbitrary", "arbitrary"),
        name="gla",
    )(*args)


def _memkv_body(mem_ref, gmem_ref, w_ref, gk_ref, mkt_out, mvt_out):
    xb = _rms_rows(mem_ref[...], gmem_ref[...]).astype(bf16)
    kv = _mm(xb, w_ref[...])
    lane = lax.broadcasted_iota(jnp.int32, (1, LANES), 1)
    gk = gk_ref[...]
    for j in range(MEM_WIDTH // LANES):
        blk = _half_head_rms(kv[:, j * LANES:(j + 1) * LANES], gk, lane)
        mkt_out[j * LANES:(j + 1) * LANES, :] = blk.T
        mvt_out[j * LANES:(j + 1) * LANES, :] = kv[:, MEM_WIDTH + j * LANES:MEM_WIDTH + (j + 1) * LANES].T


def _memkv(mem2, w, nb, n_mem):
    row = lambda i: (i, 0)
    fix = lambda i: (0, 0)
    return pl.pallas_call(
        _memkv_body,
        grid=(nb,),
        in_specs=[pl.BlockSpec((n_mem, D_MODEL), row), pl.BlockSpec(w['g_mem'].shape, fix),
                  pl.BlockSpec(w['w_mem_kv'].shape, fix), pl.BlockSpec(w['g_mem_k'].shape, fix)],
        out_specs=[pl.BlockSpec((None, MEM_WIDTH, n_mem), lambda i: (i, 0, 0))] * 2,
        out_shape=[jax.ShapeDtypeStruct((nb, MEM_WIDTH, n_mem), f32)] * 2,
        compiler_params=_cparams("arbitrary"),
        name="memkv",
    )(mem2, w['g_mem'], w['w_mem_kv'], w['g_mem_k'])


def _memattn_body(q_ref, mkt_ref, mvt_ref, o_ref, *, nb, tq):
    lane_head = lax.broadcasted_iota(jnp.int32, (1, MEM_WIDTH), 1) // MEM_HEAD_DIM
    for b in range(nb):
        q = q_ref[b]
        mkt = mkt_ref[b].astype(bf16)
        mvt = mvt_ref[b].astype(bf16)
        qs = jnp.concatenate([jnp.where(lane_head == h, q, jnp.zeros_like(q)) for h in range(MEM_HEADS)], axis=0)
        s = _mm(qs, mkt)
        p = jnp.exp(s - jnp.max(s, axis=-1, keepdims=True))
        pv = _nt(p.astype(bf16), mvt) / jnp.sum(p, axis=-1, keepdims=True)
        o = jnp.zeros((tq, MEM_WIDTH), f32)
        for h in range(MEM_HEADS):
            o = o + jnp.where(lane_head == h, pv[h * tq:(h + 1) * tq], 0.0)
        o_ref[b] = o


def _memattn(layer, q3, mkt4, mvt4, nb, tq):
    B, L, _ = q3.shape
    n_mem = mkt4.shape[-1]
    kv_spec = pl.BlockSpec((None, nb, MEM_WIDTH, n_mem), lambda b, i: (layer, b, 0, 0))
    return pl.pallas_call(
        functools.partial(_memattn_body, nb=nb, tq=tq),
        grid=(B // nb, L // tq),
        in_specs=[pl.BlockSpec((nb, tq, MEM_WIDTH), lambda b, i: (b, i, 0)), kv_spec, kv_spec],
        out_specs=pl.BlockSpec((nb, tq, MEM_WIDTH), lambda b, i: (b, i, 0)),
        out_shape=jax.ShapeDtypeStruct((B, L, MEM_WIDTH), f32),
        compiler_params=_cparams("arbitrary", "arbitrary"),
        name="memattn",
    )(q3, mkt4, mvt4)


def _merge_body(x_ref, mla_ref, gla_ref, mem_ref, gates_ref, ggla_ref, wout_ref, y_ref):
    lane = lax.broadcasted_iota(jnp.int32, (1, LANES), 1)
    ggla = ggla_ref[...]
    y = x_ref[...]
    m1 = (gates_ref[:, 0:MLA_WIDTH] * mla_ref[...]).astype(bf16)
    y = y + _mm(m1, wout_ref[0:MLA_WIDTH, :])
    for j in range(GLA_WIDTH // LANES):
        ls = slice(j * LANES, (j + 1) * LANES)
        gn = _half_head_rms(gla_ref[:, ls], ggla, lane)
        gs = slice(MLA_WIDTH + j * LANES, MLA_WIDTH + (j + 1) * LANES)
        y = y + _mm((gates_ref[:, gs] * gn).astype(bf16), wout_ref[gs, :])
    ms = slice(MLA_WIDTH + GLA_WIDTH, D_MIX)
    m3 = (gates_ref[:, ms] * mem_ref[...]).astype(bf16)
    y_ref[...] = y + _mm(m3, wout_ref[ms, :])


def _merge(x2, mla_o, gla_o, mem_o, gates, w, tm):
    T = x2.shape[0]
    row = lambda i: (i, 0)
    fix = lambda i: (0, 0)
    return pl.pallas_call(
        _merge_body,
        grid=(T // tm,),
        in_specs=[pl.BlockSpec((tm, D_MODEL), row), pl.BlockSpec((tm, MLA_WIDTH), row),
                  pl.BlockSpec((tm, GLA_WIDTH), row), pl.BlockSpec((tm, MEM_WIDTH), row),
                  pl.BlockSpec((tm, D_MIX), row), pl.BlockSpec(w['g_gla_o'].shape, fix),
                  pl.BlockSpec(w['w_out'].shape, fix)],
        out_specs=pl.BlockSpec((tm, D_MODEL), row),
        out_shape=jax.ShapeDtypeStruct((T, D_MODEL), f32),
        compiler_params=_cparams("arbitrary"),
        name="merge",
    )(x2, mla_o, gla_o, mem_o, gates, w['g_gla_o'], w['w_out'])


def _pad_head(a):
    return jnp.pad(a, [(0, 0)] * (a.ndim - 1) + [(0, HEAD_PAD - QK_DIM)])


def _prep_layer(l, p):
    w_in = p['w_in'][l]
    cuts, o = [], 0
    for n in (Q_RANK, KV_RANK, ROPE_DIM, MLA_WIDTH, GLA_KDIM, GLA_KDIM, GLA_WIDTH, GLA_GATE_RANK, GLA_WIDTH,
              MEM_WIDTH, MEM_WIDTH):
        cuts.append(w_in[:, o:o + n])
        o += n
    cq, ckv, kpe, gate_mla, gq, gk, gv, gg, gate_gla, mq, gate_mem = cuts
    z = lambda n: jnp.zeros((D_MODEL, n), f32)
    misc = jnp.concatenate([gg, z(KPE_LANE - GLA_GATE_RANK), kpe, z(LANES - KPE_LANE - ROPE_DIM)], axis=1)
    w_in_packed = jnp.concatenate([cq, ckv, gate_mla, gq, gk, gv, gate_gla, mq, gate_mem, misc], axis=1)

    w_uk = p['w_uk'][l]
    g_k = p['g_mla_k'][l]
    wuk_packed = jnp.pad(w_uk, ((0, 0), (0, 0), (0, HEAD_PAD - NOPE_DIM))).reshape(KV_RANK, QK_PACKED)
    w_uk_t = w_uk.reshape(KV_RANK, MLA_HEADS * NOPE_DIM).T
    w_uk_g = jnp.pad((w_uk * g_k[None, None, :NOPE_DIM]).transpose(1, 2, 0),
                     ((0, 0), (0, HEAD_PAD - NOPE_DIM), (0, 0)))
    w_uv = p['w_uv'][l].transpose(1, 0, 2)
    zeros = jnp.zeros_like(w_uv)
    even = jnp.arange(MLA_HEADS)[:, None, None] % 2 == 0
    w_uv_pairs = jnp.concatenate([jnp.where(even, w_uv, zeros), jnp.where(even, zeros, w_uv)], axis=-1)
    w_gk = jnp.pad(p['w_gk'][l], ((0, LANES - GLA_GATE_RANK), (0, 0)))
    row = lambda a: a.reshape(1, -1)
    return {
        'g_pre': row(p['g_pre'][l]), 'w_in': w_in_packed.astype(bf16), 'g_qa': row(p['g_qa'][l]),
        'w_uq': _pad_head(p['w_uq'][l]).reshape(Q_RANK, QK_PACKED).astype(bf16),
        'g_kva': row(p['g_kva'][l]), 'w_uk': wuk_packed.astype(bf16),
        'g_q': row(_pad_head(p['g_mla_q'][l])), 'g_k': row(_pad_head(g_k)),
        'w_gk': w_gk.astype(bf16), 'b_gk': row(p['b_gk'][l]),
        'g_mem_q': row(jnp.tile(p['g_mem_q'][l], 2)), 'g_mem_k': row(jnp.tile(p['g_mem_k'][l], 2)),
        'g_gla_o': row(jnp.tile(p['g_gla_o'][l], 2)),
        'w_uk_t': w_uk_t.astype(bf16), 'w_uk_g': w_uk_g.astype(bf16),
        'g_k_rope': g_k[NOPE_DIM:].reshape(ROPE_DIM, 1),
        'w_uv_pairs': w_uv_pairs.astype(bf16),
        'w_uv_t': w_uv.transpose(0, 2, 1).astype(bf16),
        'g_mem': row(p['g_mem'][l]), 'w_mem_kv': p['w_mem_kv'][l].astype(bf16),
        'w_out': p['w_out'][l].astype(bf16),
    }


def _angles(pos):
    inv_freq = ROPE_THETA ** (-jnp.arange(HALF_ROPE, dtype=f32) * (2.0 / ROPE_DIM))
    return pos.astype(f32)[:, None] * inv_freq[None, :]


def _packed_tables(pos):
    ang = _angles(pos)
    n = pos.shape[0]
    cos, sin = jnp.cos(ang), jnp.sin(ang)
    one = jnp.ones((n, NOPE_DIM), f32)
    z = lambda w: jnp.zeros((n, w), f32)
    tail = HEAD_PAD - QK_DIM
    cos_p = jnp.concatenate([one, cos, cos, z(tail)], axis=1)
    sin_up = jnp.concatenate([z(NOPE_DIM + HALF_ROPE), sin, z(tail)], axis=1)
    sin_dn = jnp.concatenate([z(NOPE_DIM), -sin, z(HALF_ROPE + tail)], axis=1)
    return cos_p, sin_up, sin_dn


def _transposed_tables(pos):
    ang = _angles(pos).T
    return jnp.cos(ang), jnp.sin(ang)


def _state_to_blockdiag_t(s):
    B = s.shape[0]
    st = s.transpose(0, 1, 3, 2)
    eye = jnp.eye(GLA_HEADS, dtype=s.dtype)
    return jnp.einsum('bhvk,hg->bhvgk', st, eye).reshape(B, GLA_WIDTH, GLA_KDIM)


def _blockdiag_t_to_state(st):
    B = st.shape[0]
    s5 = st.reshape(B, GLA_HEADS, GLA_DV, GLA_HEADS, GLA_DK)
    diag = jnp.stack([s5[:, h, :, h, :] for h in range(GLA_HEADS)], axis=1)
    return diag.transpose(0, 1, 3, 2)


def kernel(x_prompt, x_sample, mem_prompt, cache_ckv, cache_kpe, page_table, state_gla, cache_mem_k, cache_mem_v, g_pre, w_in, g_qa, w_uq, g_kva, w_uk, w_uv, g_mla_q, g_mla_k, w_gk, b_gk, g_gla_o, g_mem, w_mem_kv, g_mem_q, g_mem_k, w_out):
    params = dict(g_pre=g_pre, w_in=w_in, g_qa=g_qa, w_uq=w_uq, g_kva=g_kva, w_uk=w_uk, w_uv=w_uv,
                  g_mla_q=g_mla_q, g_mla_k=g_mla_k, w_gk=w_gk, b_gk=b_gk, g_gla_o=g_gla_o, g_mem=g_mem,
                  w_mem_kv=w_mem_kv, g_mem_q=g_mem_q, g_mem_k=g_mem_k, w_out=w_out)
    nb, seq, _ = x_prompt.shape
    ndb, t_new, _ = x_sample.shape
    n_mem = mem_prompt.shape[1]
    depth = w_in.shape[0]
    n_pages = page_table.shape[1]
    past_len = n_pages * cache_ckv.shape[2]

    tm_p = min(256, seq)
    tq = min(256, seq)
    tm_s = min(256, ndb * t_new)
    pp = min(16, n_pages)
    gla_tt = min(128, seq)
    gla_cs = min(16, gla_tt)
    nb_s = min(8, ndb)

    cache_kpe_t = cache_kpe.transpose(0, 1, 3, 2)
    to_t = lambda a: a.transpose(0, 1, 3, 4, 2).reshape(depth, ndb, MEM_WIDTH, n_mem)
    cache_mkt, cache_mvt = to_t(cache_mem_k), to_t(cache_mem_v)
    from_t = lambda a: a.reshape(nb, MEM_HEADS, MEM_HEAD_DIM, n_mem).transpose(0, 3, 1, 2)

    tabs_p = _packed_tables(jnp.arange(seq, dtype=jnp.int32))
    pos_s = past_len + jnp.arange(t_new, dtype=jnp.int32)
    tabs_s = tuple(jnp.tile(t, (tm_s // t_new, 1)) for t in _packed_tables(pos_s))
    tabs_t = _transposed_tables(jnp.arange(past_len, dtype=jnp.int32)) \
        + _transposed_tables(past_len + jnp.arange(PAGE_SIZE, dtype=jnp.int32))

    xp = x_prompt.reshape(nb * seq, D_MODEL)
    xs = x_sample.reshape(ndb * t_new, D_MODEL)
    mem2 = mem_prompt.reshape(nb * n_mem, D_MODEL)
    ckv_p, kpe_p, gla_p, mk_p, mv_p, ckv_s, kpe_s, gla_s = ([] for _ in range(8))
    for l in range(depth):
        w = _prep_layer(l, params)
        q, k, c, kpe, gates, gq, gk, gv, gg, mq = _inproj(xp, tabs_p, w, tm_p)
        mla_o = _mla_prompt(q, k, c, w['w_uv_t'], nb, seq, tq)
        r3 = lambda a: a.reshape(nb, seq, a.shape[-1])
        gla_o, st = _gla(r3(gq), r3(gk), r3(gv), r3(gg), None, 1, gla_tt, gla_cs)
        mkt, mvt = _memkv(mem2, w, nb, n_mem)
        mem_o = _memattn(0, r3(mq), mkt[None], mvt[None], 1, tq)
        xp = _merge(xp, mla_o, gla_o.reshape(nb * seq, GLA_WIDTH), mem_o.reshape(nb * seq, MEM_WIDTH), gates, w, tm_p)
        ckv_p.append(c.reshape(nb, seq, KV_RANK))
        kpe_p.append(kpe.reshape(nb, seq, ROPE_DIM))
        gla_p.append(_blockdiag_t_to_state(st))
        mk_p.append(from_t(mkt))
        mv_p.append(from_t(mvt))
        q, k, c, kpe, gates, gq, gk, gv, gg, mq = _inproj(xs, tabs_s, w, tm_s)
        mla_o = _mla_sample(l, q, c, kpe, cache_ckv, cache_kpe_t, page_table, w, tabs_t, pp)
        r3 = lambda a: a.reshape(ndb, t_new, a.shape[-1])
        gla_o, st = _gla(r3(gq), r3(gk), r3(gv), r3(gg), _state_to_blockdiag_t(state_gla[l]), nb_s, t_new, t_new)
        mem_o = _memattn(l, r3(mq), cache_mkt, cache_mvt, nb_s, t_new)
        xs = _merge(xs, mla_o, gla_o.reshape(ndb * t_new, GLA_WIDTH), mem_o.reshape(ndb * t_new, MEM_WIDTH), gates, w, tm_s)
        ckv_s.append(c.reshape(ndb, t_new, KV_RANK))
        kpe_s.append(kpe.reshape(ndb, t_new, ROPE_DIM))
        gla_s.append(_blockdiag_t_to_state(st))
    return (xp.reshape(nb, seq, D_MODEL), xs.reshape(ndb, t_new, D_MODEL), jnp.stack(ckv_p), jnp.stack(kpe_p),
            jnp.stack(gla_p), jnp.stack(mk_p), jnp.stack(mv_p), jnp.stack(ckv_s), jnp.stack(kpe_s), jnp.stack(gla_s))
```

```python
import functools

import jax
import jax.numpy as jnp
from jax import lax
from jax.experimental import pallas as pl
from jax.experimental.pallas import tpu as pltpu

f32, bf16 = jnp.float32, jnp.bfloat16

D_MODEL = 1024
PAGE_SIZE = 128
MLA_HEADS = 8
NOPE_DIM = 64
ROPE_DIM = 32
HALF_ROPE = ROPE_DIM // 2
QK_DIM = NOPE_DIM + ROPE_DIM
V_DIM = 64
Q_RANK = 384
KV_RANK = 256
MLA_WIDTH = MLA_HEADS * V_DIM
ROPE_THETA = 10000.0
MLA_SCALE = QK_DIM ** -0.5
GLA_HEADS = 4
GLA_DK = 32
GLA_DV = 64
GLA_KDIM = GLA_HEADS * GLA_DK
GLA_WIDTH = GLA_HEADS * GLA_DV
GLA_GATE_RANK = 16
GLA_NORMALIZER = 16.0
MEM_HEADS = 4
MEM_HEAD_DIM = 64
MEM_WIDTH = MEM_HEADS * MEM_HEAD_DIM
MEM_SCALE = MEM_HEAD_DIM ** -0.5
D_MIX = MLA_WIDTH + GLA_WIDTH + MEM_WIDTH
EPS = 1e-6
NEG_INF = -1e30

LANES = 128
HEAD_PAD = LANES
QK_PACKED = MLA_HEADS * HEAD_PAD

OFF_CQ = 0
OFF_CKV = OFF_CQ + Q_RANK
OFF_GATE_MLA = OFF_CKV + KV_RANK
OFF_GQ = OFF_GATE_MLA + MLA_WIDTH
OFF_GK = OFF_GQ + GLA_KDIM
OFF_GV = OFF_GK + GLA_KDIM
OFF_GATE_GLA = OFF_GV + GLA_WIDTH
OFF_MQ = OFF_GATE_GLA + GLA_WIDTH
OFF_GATE_MEM = OFF_MQ + MEM_WIDTH
OFF_MISC = OFF_GATE_MEM + MEM_WIDTH
D_IN_PACKED = OFF_MISC + LANES
KPE_LANE = NOPE_DIM

VMEM_LIMIT = 56 * 1024 * 1024


def _cparams(*sem):
    return pltpu.CompilerParams(dimension_semantics=sem, vmem_limit_bytes=VMEM_LIMIT)


def _nt(a, b):
    return lax.dot_general(a, b, (((1,), (1,)), ((), ())), preferred_element_type=f32)


def _tn(a, b):
    return lax.dot_general(a, b, (((0,), (0,)), ((), ())), preferred_element_type=f32)


def _mm(a, b):
    return jnp.dot(a, b, preferred_element_type=f32)


def _rms_rows(x, g):
    return x * lax.rsqrt(jnp.mean(x * x, axis=-1, keepdims=True) + EPS) * g


def _half_head_rms(blk, g, lane):
    sq = blk * blk
    lo = lane < 64
    ss_lo = jnp.sum(jnp.where(lo, sq, 0.0), axis=-1, keepdims=True)
    ss_hi = jnp.sum(jnp.where(lo, 0.0, sq), axis=-1, keepdims=True)
    ss = jnp.where(lo, ss_lo, ss_hi)
    return blk * lax.rsqrt(ss * (1.0 / 64.0) + EPS) * g


def _rope_packed(x, cos, sin_up, sin_dn):
    return x * cos + pltpu.roll(x, HALF_ROPE, 1) * sin_up + pltpu.roll(x, LANES - HALF_ROPE, 1) * sin_dn


def _inproj_body(x_ref, gpre_ref, win_ref, gqa_ref, wuq_ref, gkva_ref, wuk_ref, gq_ref, gk_ref,
                 cos_ref, sup_ref, sdn_ref, wgk_ref, bgk_ref, gmq_ref,
                 q_out, k_out, ckv_out, kpe_out, gates_out, glaq_out, glak_out, glav_out, glag_out, memq_out):
    x = x_ref[...]
    xb = _rms_rows(x, gpre_ref[...]).astype(bf16)

    def seg(off, n):
        return _mm(xb, win_ref[:, off:off + n])

    cos, sup, sdn = cos_ref[...], sup_ref[...], sdn_ref[...]
    lane = lax.broadcasted_iota(jnp.int32, (1, LANES), 1)

    cq = _rms_rows(seg(OFF_CQ, Q_RANK), gqa_ref[...]).astype(bf16)
    qf = _mm(cq, wuq_ref[...])
    gq = gq_ref[...]
    for h in range(MLA_HEADS):
        qh = qf[:, h * HEAD_PAD:(h + 1) * HEAD_PAD]
        ss = jnp.sum(qh * qh, axis=-1, keepdims=True)
        qn = qh * lax.rsqrt(ss * (1.0 / QK_DIM) + EPS) * gq
        q_out[:, h * HEAD_PAD:(h + 1) * HEAD_PAD] = (_rope_packed(qn, cos, sup, sdn) * MLA_SCALE).astype(bf16)

    c = _rms_rows(seg(OFF_CKV, KV_RANK), gkva_ref[...])
    ckv_out[...] = c
    misc = seg(OFF_MISC, LANES)
    kpe_wide = jnp.where((lane >= KPE_LANE) & (lane < KPE_LANE + ROPE_DIM), misc, 0.0)
    kpe_out[...] = misc[:, KPE_LANE:KPE_LANE + ROPE_DIM]
    kf = _mm(c.astype(bf16), wuk_ref[...])
    gk = gk_ref[...]
    for h in range(MLA_HEADS):
        kh = kf[:, h * HEAD_PAD:(h + 1) * HEAD_PAD] + kpe_wide
        ss = jnp.sum(kh * kh, axis=-1, keepdims=True)
        kn = kh * lax.rsqrt(ss * (1.0 / QK_DIM) + EPS) * gk
        k_out[:, h * HEAD_PAD:(h + 1) * HEAD_PAD] = _rope_packed(kn, cos, sup, sdn).astype(bf16)

    g1 = seg(OFF_GATE_MLA, MLA_WIDTH)
    gates_out[:, 0:MLA_WIDTH] = g1 * jax.nn.sigmoid(g1)
    g2 = seg(OFF_GATE_GLA, GLA_WIDTH)
    gates_out[:, MLA_WIDTH:MLA_WIDTH + GLA_WIDTH] = g2 * jax.nn.sigmoid(g2)
    g3 = seg(OFF_GATE_MEM, MEM_WIDTH)
    gates_out[:, MLA_WIDTH + GLA_WIDTH:D_MIX] = g3 * jax.nn.sigmoid(g3)

    glaq_out[...] = seg(OFF_GQ, GLA_KDIM) * (GLA_DK ** -0.5)
    glak_out[...] = seg(OFF_GK, GLA_KDIM)
    glav_out[...] = seg(OFF_GV, GLA_WIDTH)
    gl = _mm(misc.astype(bf16), wgk_ref[...]) + bgk_ref[...]
    glag_out[...] = jax.nn.log_sigmoid(gl) * (1.0 / GLA_NORMALIZER)

    mq = seg(OFF_MQ, MEM_WIDTH)
    gmq = gmq_ref[...]
    for j in range(MEM_WIDTH // LANES):
        blk = _half_head_rms(mq[:, j * LANES:(j + 1) * LANES], gmq, lane)
        memq_out[:, j * LANES:(j + 1) * LANES] = (blk * MEM_SCALE).astype(bf16)


def _inproj(x2, tabs, w, tm):
    T = x2.shape[0]
    nt = tabs[0].shape[0] // tm
    row = lambda i: (i, 0)
    fix = lambda i: (0, 0)
    tab = lambda i: (i % nt, 0)

    def full(a):
        return pl.BlockSpec(a.shape, fix)

    consts1 = (w['g_pre'], w['w_in'], w['g_qa'], w['w_uq'], w['g_kva'], w['w_uk'], w['g_q'], w['g_k'])
    consts2 = (w['w_gk'], w['b_gk'], w['g_mem_q'])
    outs = [(QK_PACKED, bf16), (QK_PACKED, bf16), (KV_RANK, f32), (ROPE_DIM, f32), (D_MIX, f32),
            (GLA_KDIM, f32), (GLA_KDIM, f32), (GLA_WIDTH, f32), (GLA_KDIM, f32), (MEM_WIDTH, bf16)]
    return pl.pallas_call(
        _inproj_body,
        grid=(T // tm,),
        in_specs=[pl.BlockSpec((tm, D_MODEL), row)] + [full(a) for a in consts1]
        + [pl.BlockSpec((tm, LANES), tab)] * 3 + [full(a) for a in consts2],
        out_specs=[pl.BlockSpec((tm, n), row) for n, _ in outs],
        out_shape=[jax.ShapeDtypeStruct((T, n), dt) for n, dt in outs],
        compiler_params=_cparams("arbitrary"),
        name="inproj",
    )(x2, *consts1, *tabs, *consts2)


def _mla_prompt_body(q_ref, k_ref, c_ref, wuvt_ref, o_ref, ct_scr, acc_scr, m_scr, l_scr, ot_scr, *, tq):
    i = pl.program_id(1)
    nblk = ct_scr.shape[0]

    @pl.when(i == 0)
    def _():
        for j in range(nblk):
            ct_scr[j] = c_ref[j * tq:(j + 1) * tq, :].T.astype(bf16)

    m_scr[...] = jnp.full(m_scr.shape, NEG_INF, f32)
    l_scr[...] = jnp.zeros(l_scr.shape, f32)
    acc_scr[...] = jnp.zeros(acc_scr.shape, f32)
    key = lax.broadcasted_iota(jnp.int32, (tq, tq), 0)
    qry = lax.broadcasted_iota(jnp.int32, (tq, tq), 1)
    causal = key <= qry

    def step(kb, masked):
        ks = pl.ds(pl.multiple_of(kb * tq, tq), tq)
        ct = ct_scr[kb]
        for h in range(MLA_HEADS):
            hs = slice(h * HEAD_PAD, (h + 1) * HEAD_PAD)
            s = _nt(k_ref[ks, hs], q_ref[:, hs])
            if masked:
                s = jnp.where(causal, s, NEG_INF)
            m_old = m_scr[h:h + 1, :]
            m_new = jnp.maximum(m_old, jnp.max(s, axis=0, keepdims=True))
            alpha = jnp.exp(m_old - m_new)
            p = jnp.exp(s - m_new)
            l_scr[h:h + 1, :] = l_scr[h:h + 1, :] * alpha + jnp.sum(p, axis=0, keepdims=True)
            acc_scr[h] = acc_scr[h] * alpha + _mm(ct, p.astype(bf16))
            m_scr[h:h + 1, :] = m_new

    def body(kb, carry):
        step(kb, False)
        return carry

    lax.fori_loop(0, i, body, 0)
    step(i, True)

    for h in range(MLA_HEADS):
        lat_t = (acc_scr[h] / l_scr[h:h + 1, :]).astype(bf16)
        ot_scr[h * V_DIM:(h + 1) * V_DIM, :] = _mm(wuvt_ref[h], lat_t)
    o_ref[...] = ot_scr[...].T


def _mla_prompt(q, k, c, wuv_t, nb, seq, tq):
    nq = seq // tq
    return pl.pallas_call(
        functools.partial(_mla_prompt_body, tq=tq),
        grid=(nb, nq),
        in_specs=[pl.BlockSpec((tq, QK_PACKED), lambda b, i: (b * nq + i, 0)),
                  pl.BlockSpec((seq, QK_PACKED), lambda b, i: (b, 0)),
                  pl.BlockSpec((seq, KV_RANK), lambda b, i: (b, 0)),
                  pl.BlockSpec(wuv_t.shape, lambda b, i: (0, 0, 0))],
        out_specs=pl.BlockSpec((tq, MLA_WIDTH), lambda b, i: (b * nq + i, 0)),
        out_shape=jax.ShapeDtypeStruct((nb * seq, MLA_WIDTH), f32),
        scratch_shapes=[pltpu.VMEM((nq, KV_RANK, tq), bf16), pltpu.VMEM((MLA_HEADS, KV_RANK, tq), f32),
                        pltpu.VMEM((MLA_HEADS, tq), f32), pltpu.VMEM((MLA_HEADS, tq), f32),
                        pltpu.VMEM((MLA_WIDTH, tq), f32)],
        compiler_params=_cparams("arbitrary", "arbitrary"),
        name="mla_prompt",
    )(q, k, c, wuv_t)


def _mla_sample_body(pt_ref, q_ref, cnew_ref, kpenewt_ref, wukt_ref, wukg_ref, gkr_ref, cost_ref, sint_ref, wuv_ref,
                     ckv_hbm, kpe_hbm, o_ref,
                     cbuf, kbuf, sem, lhs_scr, qr_scr, cb_scr, s_scr, ql_all, qr_all, acc_scr, m_scr, l_scr,
                     *, layer, gp, n_pages, t_new):
    ndb = ql_all.shape[0]
    ng = (n_pages + 1) // gp
    kb = gp * PAGE_SIZE
    total = ndb * ng
    nrow = MLA_HEADS * t_new
    n_nope = MLA_HEADS * NOPE_DIM
    new_rows = pl.ds(kb - PAGE_SIZE, t_new)
    new_lanes = pl.ds(kb - PAGE_SIZE, PAGE_SIZE)

    def group_copies(t, slot):
        b, g = t // ng, t % ng
        cps = []
        for i in range(gp):
            page = pt_ref[b * n_pages + jnp.minimum(g * gp + i, n_pages - 1)]
            lanes = pl.ds(i * PAGE_SIZE, PAGE_SIZE)
            cps.append(pltpu.make_async_copy(ckv_hbm.at[layer, page], cbuf.at[slot, lanes, :], sem.at[0, slot]))
            cps.append(pltpu.make_async_copy(kpe_hbm.at[layer, page], kbuf.at[slot, :, lanes], sem.at[1, slot]))
        return cps

    def reset_state():
        m_scr[...] = jnp.full((nrow, 1), NEG_INF, f32)
        l_scr[...] = jnp.zeros((nrow, 1), f32)
        acc_scr[...] = jnp.zeros((nrow, KV_RANK), f32)

    gkr = gkr_ref[...]
    key_lane = lax.broadcasted_iota(jnp.int32, (nrow, PAGE_SIZE), 1)
    qry_row = lax.broadcasted_iota(jnp.int32, (nrow, PAGE_SIZE), 0) % t_new

    def prepare_group(t, slot):
        b, g = t // ng, t % ng
        lhs_scr[n_nope:, :] = ql_all[b].astype(bf16)
        qr_scr[...] = qr_all[b].astype(bf16)
        for cp in group_copies(t, slot):
            cp.wait()
        is_last = g == ng - 1
        cbuf[slot, new_rows, :] = jnp.where(is_last, cnew_ref[b], cbuf[slot, new_rows, :])
        kbuf[slot, :, new_lanes] = jnp.where(is_last, kpenewt_ref[b], kbuf[slot, :, new_lanes])

    def score_group(t, slot):
        g = t % ng
        is_last = g == ng - 1
        cb = cbuf[slot].astype(bf16)
        cb_scr[slot] = cb
        kt = kbuf[slot]
        cos_t, sin_t = cost_ref[g], sint_ref[g]
        big = _nt(lhs_scr[...], cb)
        kn = big[0:n_nope]
        ss_nope = jnp.sum((kn * kn).reshape(MLA_HEADS, NOPE_DIM, kb), axis=1)
        ss = ss_nope + jnp.sum(kt * kt, axis=0, keepdims=True)
        r = lax.rsqrt(ss * (1.0 / QK_DIM) + EPS)
        kg = kt * gkr
        k1, k2 = kg[0:HALF_ROPE], kg[HALF_ROPE:ROPE_DIM]
        kr = jnp.concatenate([k1 * cos_t - k2 * sin_t, k2 * cos_t + k1 * sin_t], axis=0).astype(bf16)
        s = big[n_nope:] + _mm(qr_scr[...], kr)
        s = (s.reshape(MLA_HEADS, t_new, kb) * r[:, None, :]).reshape(nrow, kb)
        first_dead = jnp.where(is_last, 1, PAGE_SIZE + t_new)
        tail = jnp.where(key_lane >= qry_row + first_dead, NEG_INF, s[:, kb - PAGE_SIZE:])
        s_scr[slot] = jnp.concatenate([s[:, :kb - PAGE_SIZE], tail], axis=1)

    def value_group(slot):
        s = s_scr[slot]
        m_old = m_scr[...]
        m_new = jnp.maximum(m_old, jnp.max(s, axis=-1, keepdims=True))
        alpha = jnp.exp(m_old - m_new)
        p = jnp.exp(s - m_new)
        l_scr[...] = l_scr[...] * alpha + jnp.sum(p, axis=-1, keepdims=True)
        acc_scr[...] = acc_scr[...] * alpha + _mm(p.astype(bf16), cb_scr[slot])
        m_scr[...] = m_new

    def finish_batch(b, closed):
        ql_all[b] = jnp.where(closed, acc_scr[...] / l_scr[...], ql_all[b])
        m_scr[...] = jnp.where(closed, NEG_INF, m_scr[...])
        l_scr[...] = jnp.where(closed, 0.0, l_scr[...])
        acc_scr[...] = jnp.where(closed, 0.0, acc_scr[...])

    lhs_scr[0:n_nope, :] = wukt_ref[...]
    for h in range(MLA_HEADS):
        q_h = q_ref[:, h * HEAD_PAD:(h + 1) * HEAD_PAD]
        rows = slice(h * t_new, (h + 1) * t_new)
        ql_all[:, rows, :] = _mm(q_h, wukg_ref[h]).reshape(ndb, t_new, KV_RANK)
        qr_all[:, rows, :] = q_h[:, KPE_LANE:KPE_LANE + ROPE_DIM].astype(f32).reshape(ndb, t_new, ROPE_DIM)
    reset_state()

    def skewed_step(t, slot):
        for cp in group_copies(t + 1, 1 - slot):
            cp.start()
        prepare_group(t, slot)
        score_group(t, slot)
        value_group(1 - slot)
        finish_batch(jnp.maximum(t // ng - 1, 0), t % ng == 0)

    for cp in group_copies(0, 0):
        cp.start()
    if total > 1:
        for cp in group_copies(1, 1):
            cp.start()
    prepare_group(0, 0)
    score_group(0, 0)

    n_pairs = max(total - 2, 0) // 2

    def body(k, carry):
        skewed_step(2 * k + 1, 1)
        skewed_step(2 * k + 2, 0)
        return carry

    lax.fori_loop(0, n_pairs, body, 0)
    for t in range(2 * n_pairs + 1, total):
        if t + 1 < total:
            skewed_step(t, t % 2)
        else:
            prepare_group(t, t % 2)
            score_group(t, t % 2)
            value_group(1 - t % 2)
            finish_batch(max(t // ng - 1, 0), t % ng == 0)
    value_group((total - 1) % 2)
    finish_batch(ndb - 1, True)

    for h in range(MLA_HEADS):
        lat_h = ql_all[:, h * t_new:(h + 1) * t_new, :].reshape(ndb * t_new, KV_RANK).astype(bf16)
        contrib = _mm(lat_h, wuv_ref[h])
        ps = slice((h // 2) * LANES, (h // 2 + 1) * LANES)
        if h % 2 == 0:
            o_ref[:, ps] = contrib
        else:
            o_ref[:, ps] += contrib


def _mla_sample(layer, q, c_new, kpe_new, cache_ckv, cache_kpe_t, page_table, w, tabs_t, gp):
    ndb, n_pages = page_table.shape
    t_new = q.shape[0] // ndb
    ng = (n_pages + 1) // gp
    kb = gp * PAGE_SIZE
    nrow = MLA_HEADS * t_new
    cost, sint = tabs_t
    by_group = lambda a: a.reshape(HALF_ROPE, ng, kb).transpose(1, 0, 2)
    kpe_new_t = jnp.pad(kpe_new.reshape(ndb, t_new, ROPE_DIM).transpose(0, 2, 1),
                        ((0, 0), (0, 0), (0, PAGE_SIZE - t_new)))
    args = (q, c_new.reshape(ndb, t_new, KV_RANK), kpe_new_t, w['w_uk_t'], w['w_uk_g'], w['g_k_rope'],
            by_group(cost), by_group(sint), w['w_uv_pairs'])

    def full(a):
        return pl.BlockSpec(a.shape, lambda i, pt, nd=a.ndim: (0,) * nd)

    return pl.pallas_call(
        functools.partial(_mla_sample_body, layer=layer, gp=gp, n_pages=n_pages, t_new=t_new),
        grid_spec=pltpu.PrefetchScalarGridSpec(
            num_scalar_prefetch=1,
            grid=(1,),
            in_specs=[full(a) for a in args] + [pl.BlockSpec(memory_space=pl.ANY)] * 2,
            out_specs=pl.BlockSpec((ndb * t_new, MLA_WIDTH), lambda i, pt: (0, 0)),
            scratch_shapes=[pltpu.VMEM((2, kb, KV_RANK), f32),
                            pltpu.VMEM((2, ROPE_DIM, kb), f32),
                            pltpu.SemaphoreType.DMA((2, 2)),
                            pltpu.VMEM((MLA_HEADS * NOPE_DIM + nrow, KV_RANK), bf16),
                            pltpu.VMEM((nrow, ROPE_DIM), bf16),
                            pltpu.VMEM((2, kb, KV_RANK), bf16),
                            pltpu.VMEM((2, nrow, kb), f32),
                            pltpu.VMEM((ndb, nrow, KV_RANK), f32),
                            pltpu.VMEM((ndb, nrow, ROPE_DIM), f32),
                            pltpu.VMEM((nrow, KV_RANK), f32),
                            pltpu.VMEM((nrow, 1), f32),
                            pltpu.VMEM((nrow, 1), f32)]),
        out_shape=jax.ShapeDtypeStruct((ndb * t_new, MLA_WIDTH), f32),
        compiler_params=_cparams("arbitrary"),
        name="mla_sample",
    )(page_table.reshape(-1), *args, cache_ckv, cache_kpe_t)


def _split3(x):
    hi = x.astype(bf16)
    r1 = x - hi.astype(f32)
    mid = r1.astype(bf16)
    lo = (r1 - mid.astype(f32)).astype(bf16)
    return hi, mid, lo


def _gla_body(*refs, nb, tt, cs, has_init):
    if has_init:
        q_ref, k_ref, v_ref, g_ref, s0_ref, o_ref, sT_out, st_scr = refs
    else:
        q_ref, k_ref, v_ref, g_ref, o_ref, sT_out, st_scr = refs
    ti = pl.program_id(1)
    nchunk = tt // cs

    @pl.when(ti == 0)
    def _():
        if has_init:
            st_scr[...] = s0_ref[...]
        else:
            st_scr[...] = jnp.zeros(st_scr.shape, f32)

    tri = (lax.broadcasted_iota(jnp.int32, (tt, tt), 1) <= lax.broadcasted_iota(jnp.int32, (tt, tt), 0)).astype(bf16)
    kv_head = (lax.broadcasted_iota(jnp.int32, (GLA_KDIM, GLA_WIDTH), 0) // GLA_DK
               == lax.broadcasted_iota(jnp.int32, (GLA_KDIM, GLA_WIDTH), 1) // GLA_DV)
    block_ones = kv_head.astype(bf16)
    vk_head = (lax.broadcasted_iota(jnp.int32, (GLA_WIDTH, GLA_KDIM), 0) // GLA_DV
               == lax.broadcasted_iota(jnp.int32, (GLA_WIDTH, GLA_KDIM), 1) // GLA_DK)
    sel_t = (lax.broadcasted_iota(jnp.int32, (cs, cs * cs), 1) // cs
             == lax.broadcasted_iota(jnp.int32, (cs, cs * cs), 0)).astype(bf16)
    t_idx = lax.broadcasted_iota(jnp.int32, (cs, cs, GLA_KDIM), 0)
    s_idx = lax.broadcasted_iota(jnp.int32, (cs, cs, GLA_KDIM), 1)
    causal3 = s_idx <= t_idx

    for b in range(nb):
        q, k, v, g = q_ref[b], k_ref[b], v_ref[b], g_ref[b]
        hi, mid, lo = _split3(g)
        cum = _mm(tri, hi) + _mm(tri, mid) + _mm(tri, lo)
        for n in range(nchunk):
            sl = slice(n * cs, (n + 1) * cs)
            base = cum[n * cs - 1:n * cs] if n > 0 else jnp.zeros((1, GLA_KDIM), f32)
            bc = cum[sl] - base
            b_last = bc[cs - 1:cs]
            qc, kc, vc = q[sl], k[sl], v[sl]
            st = st_scr[b]
            o_inter = _nt((qc * jnp.exp(bc)).astype(bf16), st.astype(bf16))
            diff = jnp.where(causal3, bc[:, None, :] - bc[None, :, :], NEG_INF)
            d3 = qc[:, None, :] * kc[None, :, :] * jnp.exp(diff)
            a_exp = _mm(d3.reshape(cs * cs, GLA_KDIM).astype(bf16), block_ones)
            xv = (a_exp.reshape(cs, cs, GLA_WIDTH) * vc[None, :, :]).reshape(cs * cs, GLA_WIDTH)
            o_intra = _mm(sel_t, xv.astype(bf16))
            o_ref[b, sl, :] = o_inter + o_intra
            kd = (kc * jnp.exp(b_last - bc)).astype(bf16)
            upd = _tn(vc.astype(bf16), kd)
            st_scr[b] = st * jnp.exp(b_last) + jnp.where(vk_head, upd, 0.0)

    @pl.when(ti == pl.num_programs(1) - 1)
    def _():
        sT_out[...] = st_scr[...]


def _gla(q, k, v, g, s0t, nb, tt, cs):
    B, L, _ = q.shape
    has_init = s0t is not None
    tok = lambda bi, ti: (bi, ti, 0)
    st = lambda bi, ti: (bi, 0, 0)
    in_specs = [pl.BlockSpec((nb, tt, GLA_KDIM), tok), pl.BlockSpec((nb, tt, GLA_KDIM), tok),
                pl.BlockSpec((nb, tt, GLA_WIDTH), tok), pl.BlockSpec((nb, tt, GLA_KDIM), tok)]
    args = [q, k, v, g]
    if has_init:
        in_specs.append(pl.BlockSpec((nb, GLA_WIDTH, GLA_KDIM), st))
        args.append(s0t)
    return pl.pallas_call(
        functools.partial(_gla_body, nb=nb, tt=tt, cs=cs, has_init=has_init),
        grid=(B // nb, L // tt),
        in_specs=in_specs,
        out_specs=[pl.BlockSpec((nb, tt, GLA_WIDTH), tok), pl.BlockSpec((nb, GLA_WIDTH, GLA_KDIM), st)],
        out_shape=[jax.ShapeDtypeStruct((B, L, GLA_WIDTH), f32), jax.ShapeDtypeStruct((B, GLA_WIDTH, GLA_KDIM), f32)],
        scratch_shapes=[pltpu.VMEM((nb, GLA_WIDTH, GLA_KDIM), f32)],
        compiler_params=_cparams("arbitrary", "arbitrary"),
        name="gla",
    )(*args)


def _memkv_body(mem_ref, gmem_ref, w_ref, gk_ref, mkt_out, mvt_out):
    xb = _rms_rows(mem_ref[...], gmem_ref[...]).astype(bf16)
    kv = _mm(xb, w_ref[...])
    lane = lax.broadcasted_iota(jnp.int32, (1, LANES), 1)
    gk = gk_ref[...]
    for j in range(MEM_WIDTH // LANES):
        blk = _half_head_rms(kv[:, j * LANES:(j + 1) * LANES], gk, lane)
        mkt_out[j * LANES:(j + 1) * LANES, :] = blk.T
        mvt_out[j * LANES:(j + 1) * LANES, :] = kv[:, MEM_WIDTH + j * LANES:MEM_WIDTH + (j + 1) * LANES].T


def _memkv(mem2, w, nb, n_mem):
    row = lambda i: (i, 0)
    fix = lambda i: (0, 0)
    return pl.pallas_call(
        _memkv_body,
        grid=(nb,),
        in_specs=[pl.BlockSpec((n_mem, D_MODEL), row), pl.BlockSpec(w['g_mem'].shape, fix),
                  pl.BlockSpec(w['w_mem_kv'].shape, fix), pl.BlockSpec(w['g_mem_k'].shape, fix)],
        out_specs=[pl.BlockSpec((None, MEM_WIDTH, n_mem), lambda i: (i, 0, 0))] * 2,
        out_shape=[jax.ShapeDtypeStruct((nb, MEM_WIDTH, n_mem), f32)] * 2,
        compiler_params=_cparams("arbitrary"),
        name="memkv",
    )(mem2, w['g_mem'], w['w_mem_kv'], w['g_mem_k'])


def _memattn_body(q_ref, mkt_ref, mvt_ref, o_ref, *, nb, tq):
    lane_head = lax.broadcasted_iota(jnp.int32, (1, MEM_WIDTH), 1) // MEM_HEAD_DIM
    for b in range(nb):
        q = q_ref[b]
        mkt = mkt_ref[b].astype(bf16)
        mvt = mvt_ref[b].astype(bf16)
        qs = jnp.concatenate([jnp.where(lane_head == h, q, jnp.zeros_like(q)) for h in range(MEM_HEADS)], axis=0)
        s = _mm(qs, mkt)
        p = jnp.exp(s - jnp.max(s, axis=-1, keepdims=True))
        pv = _nt(p.astype(bf16), mvt) / jnp.sum(p, axis=-1, keepdims=True)
        o = jnp.zeros((tq, MEM_WIDTH), f32)
        for h in range(MEM_HEADS):
            o = o + jnp.where(lane_head == h, pv[h * tq:(h + 1) * tq], 0.0)
        o_ref[b] = o


def _memattn(layer, q3, mkt4, mvt4, nb, tq):
    B, L, _ = q3.shape
    n_mem = mkt4.shape[-1]
    kv_spec = pl.BlockSpec((None, nb, MEM_WIDTH, n_mem), lambda b, i: (layer, b, 0, 0))
    return pl.pallas_call(
        functools.partial(_memattn_body, nb=nb, tq=tq),
        grid=(B // nb, L // tq),
        in_specs=[pl.BlockSpec((nb, tq, MEM_WIDTH), lambda b, i: (b, i, 0)), kv_spec, kv_spec],
        out_specs=pl.BlockSpec((nb, tq, MEM_WIDTH), lambda b, i: (b, i, 0)),
        out_shape=jax.ShapeDtypeStruct((B, L, MEM_WIDTH), f32),
        compiler_params=_cparams("arbitrary", "arbitrary"),
        name="memattn",
    )(q3, mkt4, mvt4)


def _merge_body(x_ref, mla_ref, gla_ref, mem_ref, gates_ref, ggla_ref, wout_ref, y_ref):
    lane = lax.broadcasted_iota(jnp.int32, (1, LANES), 1)
    ggla = ggla_ref[...]
    y = x_ref[...]
    m1 = (gates_ref[:, 0:MLA_WIDTH] * mla_ref[...]).astype(bf16)
    y = y + _mm(m1, wout_ref[0:MLA_WIDTH, :])
    for j in range(GLA_WIDTH // LANES):
        ls = slice(j * LANES, (j + 1) * LANES)
        gn = _half_head_rms(gla_ref[:, ls], ggla, lane)
        gs = slice(MLA_WIDTH + j * LANES, MLA_WIDTH + (j + 1) * LANES)
        y = y + _mm((gates_ref[:, gs] * gn).astype(bf16), wout_ref[gs, :])
    ms = slice(MLA_WIDTH + GLA_WIDTH, D_MIX)
    m3 = (gates_ref[:, ms] * mem_ref[...]).astype(bf16)
    y_ref[...] = y + _mm(m3, wout_ref[ms, :])


def _merge(x2, mla_o, gla_o, mem_o, gates, w, tm):
    T = x2.shape[0]
    row = lambda i: (i, 0)
    fix = lambda i: (0, 0)
    return pl.pallas_call(
        _merge_body,
        grid=(T // tm,),
        in_specs=[pl.BlockSpec((tm, D_MODEL), row), pl.BlockSpec((tm, MLA_WIDTH), row),
                  pl.BlockSpec((tm, GLA_WIDTH), row), pl.BlockSpec((tm, MEM_WIDTH), row),
                  pl.BlockSpec((tm, D_MIX), row), pl.BlockSpec(w['g_gla_o'].shape, fix),
                  pl.BlockSpec(w['w_out'].shape, fix)],
        out_specs=pl.BlockSpec((tm, D_MODEL), row),
        out_shape=jax.ShapeDtypeStruct((T, D_MODEL), f32),
        compiler_params=_cparams("arbitrary"),
        name="merge",
    )(x2, mla_o, gla_o, mem_o, gates, w['g_gla_o'], w['w_out'])


def _pad_head(a):
    return jnp.pad(a, [(0, 0)] * (a.ndim - 1) + [(0, HEAD_PAD - QK_DIM)])


def _prep_layer(l, p):
    w_in = p['w_in'][l]
    cuts, o = [], 0
    for n in (Q_RANK, KV_RANK, ROPE_DIM, MLA_WIDTH, GLA_KDIM, GLA_KDIM, GLA_WIDTH, GLA_GATE_RANK, GLA_WIDTH,
              MEM_WIDTH, MEM_WIDTH):
        cuts.append(w_in[:, o:o + n])
        o += n
    cq, ckv, kpe, gate_mla, gq, gk, gv, gg, gate_gla, mq, gate_mem = cuts
    z = lambda n: jnp.zeros((D_MODEL, n), f32)
    misc = jnp.concatenate([gg, z(KPE_LANE - GLA_GATE_RANK), kpe, z(LANES - KPE_LANE - ROPE_DIM)], axis=1)
    w_in_packed = jnp.concatenate([cq, ckv, gate_mla, gq, gk, gv, gate_gla, mq, gate_mem, misc], axis=1)

    w_uk = p['w_uk'][l]
    g_k = p['g_mla_k'][l]
    wuk_packed = jnp.pad(w_uk, ((0, 0), (0, 0), (0, HEAD_PAD - NOPE_DIM))).reshape(KV_RANK, QK_PACKED)
    w_uk_t = w_uk.reshape(KV_RANK, MLA_HEADS * NOPE_DIM).T
    w_uk_g = jnp.pad((w_uk * g_k[None, None, :NOPE_DIM]).transpose(1, 2, 0),
                     ((0, 0), (0, HEAD_PAD - NOPE_DIM), (0, 0)))
    w_uv = p['w_uv'][l].transpose(1, 0, 2)
    zeros = jnp.zeros_like(w_uv)
    even = jnp.arange(MLA_HEADS)[:, None, None] % 2 == 0
    w_uv_pairs = jnp.concatenate([jnp.where(even, w_uv, zeros), jnp.where(even, zeros, w_uv)], axis=-1)
    w_gk = jnp.pad(p['w_gk'][l], ((0, LANES - GLA_GATE_RANK), (0, 0)))
    row = lambda a: a.reshape(1, -1)
    return {
        'g_pre': row(p['g_pre'][l]), 'w_in': w_in_packed.astype(bf16), 'g_qa': row(p['g_qa'][l]),
        'w_uq': _pad_head(p['w_uq'][l]).reshape(Q_RANK, QK_PACKED).astype(bf16),
        'g_kva': row(p['g_kva'][l]), 'w_uk': wuk_packed.astype(bf16),
        'g_q': row(_pad_head(p['g_mla_q'][l])), 'g_k': row(_pad_head(g_k)),
        'w_gk': w_gk.astype(bf16), 'b_gk': row(p['b_gk'][l]),
        'g_mem_q': row(jnp.tile(p['g_mem_q'][l], 2)), 'g_mem_k': row(jnp.tile(p['g_mem_k'][l], 2)),
        'g_gla_o': row(jnp.tile(p['g_gla_o'][l], 2)),
        'w_uk_t': w_uk_t.astype(bf16), 'w_uk_g': w_uk_g.astype(bf16),
        'g_k_rope': g_k[NOPE_DIM:].reshape(ROPE_DIM, 1),
        'w_uv_pairs': w_uv_pairs.astype(bf16),
        'w_uv_t': w_uv.transpose(0, 2, 1).astype(bf16),
        'g_mem': row(p['g_mem'][l]), 'w_mem_kv': p['w_mem_kv'][l].astype(bf16),
        'w_out': p['w_out'][l].astype(bf16),
    }


def _angles(pos):
    inv_freq = ROPE_THETA ** (-jnp.arange(HALF_ROPE, dtype=f32) * (2.0 / ROPE_DIM))
    return pos.astype(f32)[:, None] * inv_freq[None, :]


def _packed_tables(pos):
    ang = _angles(pos)
    n = pos.shape[0]
    cos, sin = jnp.cos(ang), jnp.sin(ang)
    one = jnp.ones((n, NOPE_DIM), f32)
    z = lambda w: jnp.zeros((n, w), f32)
    tail = HEAD_PAD - QK_DIM
    cos_p = jnp.concatenate([one, cos, cos, z(tail)], axis=1)
    sin_up = jnp.concatenate([z(NOPE_DIM + HALF_ROPE), sin, z(tail)], axis=1)
    sin_dn = jnp.concatenate([z(NOPE_DIM), -sin, z(HALF_ROPE + tail)], axis=1)
    return cos_p, sin_up, sin_dn


def _transposed_tables(pos):
    ang = _angles(pos).T
    return jnp.cos(ang), jnp.sin(ang)


def _state_to_blockdiag_t(s):
    B = s.shape[0]
    st = s.transpose(0, 1, 3, 2)
    eye = jnp.eye(GLA_HEADS, dtype=s.dtype)
    return jnp.einsum('bhvk,hg->bhvgk', st, eye).reshape(B, GLA_WIDTH, GLA_KDIM)


def _blockdiag_t_to_state(st):
    B = st.shape[0]
    s5 = st.reshape(B, GLA_HEADS, GLA_DV, GLA_HEADS, GLA_DK)
    diag = jnp.stack([s5[:, h, :, h, :] for h in range(GLA_HEADS)], axis=1)
    return diag.transpose(0, 1, 3, 2)


def kernel(x_prompt, x_sample, mem_prompt, cache_ckv, cache_kpe, page_table, state_gla, cache_mem_k, cache_mem_v, g_pre, w_in, g_qa, w_uq, g_kva, w_uk, w_uv, g_mla_q, g_mla_k, w_gk, b_gk, g_gla_o, g_mem, w_mem_kv, g_mem_q, g_mem_k, w_out):
    params = dict(g_pre=g_pre, w_in=w_in, g_qa=g_qa, w_uq=w_uq, g_kva=g_kva, w_uk=w_uk, w_uv=w_uv,
                  g_mla_q=g_mla_q, g_mla_k=g_mla_k, w_gk=w_gk, b_gk=b_gk, g_gla_o=g_gla_o, g_mem=g_mem,
                  w_mem_kv=w_mem_kv, g_mem_q=g_mem_q, g_mem_k=g_mem_k, w_out=w_out)
    nb, seq, _ = x_prompt.shape
    ndb, t_new, _ = x_sample.shape
    n_mem = mem_prompt.shape[1]
    depth = w_in.shape[0]
    n_pages = page_table.shape[1]
    past_len = n_pages * cache_ckv.shape[2]

    tm_p = min(256, seq)
    tq = min(256, seq)
    tm_s = min(256, ndb * t_new)
    pp = max(d for d in range(1, 17) if (n_pages + 1) % d == 0)
    gla_tt = min(128, seq)
    gla_cs = min(16, gla_tt)
    nb_s = min(8, ndb)

    cache_kpe_t = cache_kpe.transpose(0, 1, 3, 2)
    to_t = lambda a: a.transpose(0, 1, 3, 4, 2).reshape(depth, ndb, MEM_WIDTH, n_mem)
    cache_mkt, cache_mvt = to_t(cache_mem_k), to_t(cache_mem_v)
    from_t = lambda a: a.reshape(nb, MEM_HEADS, MEM_HEAD_DIM, n_mem).transpose(0, 3, 1, 2)

    tabs_p = _packed_tables(jnp.arange(seq, dtype=jnp.int32))
    pos_s = past_len + jnp.arange(t_new, dtype=jnp.int32)
    tabs_s = tuple(jnp.tile(t, (tm_s // t_new, 1)) for t in _packed_tables(pos_s))
    tabs_t = _transposed_tables(jnp.arange(past_len + PAGE_SIZE, dtype=jnp.int32))

    xp = x_prompt.reshape(nb * seq, D_MODEL)
    xs = x_sample.reshape(ndb * t_new, D_MODEL)
    mem2 = mem_prompt.reshape(nb * n_mem, D_MODEL)
    ckv_p, kpe_p, gla_p, mk_p, mv_p, ckv_s, kpe_s, gla_s = ([] for _ in range(8))
    for l in range(depth):
        w = _prep_layer(l, params)
        q, k, c, kpe, gates, gq, gk, gv, gg, mq = _inproj(xp, tabs_p, w, tm_p)
        mla_o = _mla_prompt(q, k, c, w['w_uv_t'], nb, seq, tq)
        r3 = lambda a: a.reshape(nb, seq, a.shape[-1])
        gla_o, st = _gla(r3(gq), r3(gk), r3(gv), r3(gg), None, 1, gla_tt, gla_cs)
        mkt, mvt = _memkv(mem2, w, nb, n_mem)
        mem_o = _memattn(0, r3(mq), mkt[None], mvt[None], 1, tq)
        xp = _merge(xp, mla_o, gla_o.reshape(nb * seq, GLA_WIDTH), mem_o.reshape(nb * seq, MEM_WIDTH), gates, w, tm_p)
        ckv_p.append(c.reshape(nb, seq, KV_RANK))
        kpe_p.append(kpe.reshape(nb, seq, ROPE_DIM))
        gla_p.append(_blockdiag_t_to_state(st))
        mk_p.append(from_t(mkt))
        mv_p.append(from_t(mvt))
        q, k, c, kpe, gates, gq, gk, gv, gg, mq = _inproj(xs, tabs_s, w, tm_s)
        mla_o = _mla_sample(l, q, c, kpe, cache_ckv, cache_kpe_t, page_table, w, tabs_t, pp)
        r3 = lambda a: a.reshape(ndb, t_new, a.shape[-1])
        gla_o, st = _gla(r3(gq), r3(gk), r3(gv), r3(gg), _state_to_blockdiag_t(state_gla[l]), nb_s, t_new, t_new)
        mem_o = _memattn(l, r3(mq), cache_mkt, cache_mvt, nb_s, t_new)
        xs = _merge(xs, mla_o, gla_o.reshape(ndb * t_new, GLA_WIDTH), mem_o.reshape(ndb * t_new, MEM_WIDTH), gates, w, tm_s)
        ckv_s.append(c.reshape(ndb, t_new, KV_RANK))
        kpe_s.append(kpe.reshape(ndb, t_new, ROPE_DIM))
        gla_s.append(_blockdiag_t_to_state(st))
    return (xp.reshape(nb, seq, D_MODEL), xs.reshape(ndb, t_new, D_MODEL), jnp.stack(ckv_p), jnp.stack(kpe_p),
            jnp.stack(gla_p), jnp.stack(mk_p), jnp.stack(mv_p), jnp.stack(ckv_s), jnp.stack(kpe_s), jnp.stack(gla_s))
```

```python
import functools

import jax
import jax.numpy as jnp
from jax import lax
from jax.experimental import pallas as pl
from jax.experimental.pallas import tpu as pltpu

f32, bf16 = jnp.float32, jnp.bfloat16

D_MODEL = 1024
PAGE_SIZE = 128
MLA_HEADS = 8
NOPE_DIM = 64
ROPE_DIM = 32
HALF_ROPE = ROPE_DIM // 2
QK_DIM = NOPE_DIM + ROPE_DIM
V_DIM = 64
Q_RANK = 384
KV_RANK = 256
MLA_WIDTH = MLA_HEADS * V_DIM
ROPE_THETA = 10000.0
MLA_SCALE = QK_DIM ** -0.5
GLA_HEADS = 4
GLA_DK = 32
GLA_DV = 64
GLA_KDIM = GLA_HEADS * GLA_DK
GLA_WIDTH = GLA_HEADS * GLA_DV
GLA_GATE_RANK = 16
GLA_NORMALIZER = 16.0
MEM_HEADS = 4
MEM_HEAD_DIM = 64
MEM_WIDTH = MEM_HEADS * MEM_HEAD_DIM
MEM_SCALE = MEM_HEAD_DIM ** -0.5
D_MIX = MLA_WIDTH + GLA_WIDTH + MEM_WIDTH
EPS = 1e-6
NEG_INF = -1e30

LANES = 128
HEAD_PAD = LANES
QK_PACKED = MLA_HEADS * HEAD_PAD

OFF_CQ = 0
OFF_CKV = OFF_CQ + Q_RANK
OFF_GATE_MLA = OFF_CKV + KV_RANK
OFF_GQ = OFF_GATE_MLA + MLA_WIDTH
OFF_GK = OFF_GQ + GLA_KDIM
OFF_GV = OFF_GK + GLA_KDIM
OFF_GATE_GLA = OFF_GV + GLA_WIDTH
OFF_MQ = OFF_GATE_GLA + GLA_WIDTH
OFF_GATE_MEM = OFF_MQ + MEM_WIDTH
OFF_MISC = OFF_GATE_MEM + MEM_WIDTH
D_IN_PACKED = OFF_MISC + LANES
KPE_LANE = NOPE_DIM

VMEM_LIMIT = 56 * 1024 * 1024
PAGE_BUFFERS = 3


def _cparams(*sem):
    return pltpu.CompilerParams(dimension_semantics=sem, vmem_limit_bytes=VMEM_LIMIT)


def _nt(a, b):
    return lax.dot_general(a, b, (((1,), (1,)), ((), ())), preferred_element_type=f32)


def _tn(a, b):
    return lax.dot_general(a, b, (((0,), (0,)), ((), ())), preferred_element_type=f32)


def _mm(a, b):
    return jnp.dot(a, b, preferred_element_type=f32)


def _rms_rows(x, g):
    return x * lax.rsqrt(jnp.mean(x * x, axis=-1, keepdims=True) + EPS) * g


def _half_head_rms(blk, g, lane):
    sq = blk * blk
    lo = lane < 64
    ss_lo = jnp.sum(jnp.where(lo, sq, 0.0), axis=-1, keepdims=True)
    ss_hi = jnp.sum(jnp.where(lo, 0.0, sq), axis=-1, keepdims=True)
    ss = jnp.where(lo, ss_lo, ss_hi)
    return blk * lax.rsqrt(ss * (1.0 / 64.0) + EPS) * g


def _rope_packed(x, cos, sin_up, sin_dn):
    return x * cos + pltpu.roll(x, HALF_ROPE, 1) * sin_up + pltpu.roll(x, LANES - HALF_ROPE, 1) * sin_dn


def _inproj_body(x_ref, gpre_ref, win_ref, gqa_ref, wuq_ref, gkva_ref, wuk_ref, gq_ref, gk_ref,
                 cos_ref, sup_ref, sdn_ref, wgk_ref, bgk_ref, gmq_ref,
                 q_out, k_out, ckv_out, kpe_out, gates_out, glaq_out, glak_out, glav_out, glag_out, memq_out):
    x = x_ref[...]
    xb = _rms_rows(x, gpre_ref[...]).astype(bf16)

    def seg(off, n):
        return _mm(xb, win_ref[:, off:off + n])

    cos, sup, sdn = cos_ref[...], sup_ref[...], sdn_ref[...]
    lane = lax.broadcasted_iota(jnp.int32, (1, LANES), 1)

    cq = _rms_rows(seg(OFF_CQ, Q_RANK), gqa_ref[...]).astype(bf16)
    qf = _mm(cq, wuq_ref[...])
    gq = gq_ref[...]
    for h in range(MLA_HEADS):
        qh = qf[:, h * HEAD_PAD:(h + 1) * HEAD_PAD]
        ss = jnp.sum(qh * qh, axis=-1, keepdims=True)
        qn = qh * lax.rsqrt(ss * (1.0 / QK_DIM) + EPS) * gq
        q_out[:, h * HEAD_PAD:(h + 1) * HEAD_PAD] = (_rope_packed(qn, cos, sup, sdn) * MLA_SCALE).astype(bf16)

    c = _rms_rows(seg(OFF_CKV, KV_RANK), gkva_ref[...])
    ckv_out[...] = c
    misc = seg(OFF_MISC, LANES)
    kpe_wide = jnp.where((lane >= KPE_LANE) & (lane < KPE_LANE + ROPE_DIM), misc, 0.0)
    kpe_out[...] = misc[:, KPE_LANE:KPE_LANE + ROPE_DIM]
    kf = _mm(c.astype(bf16), wuk_ref[...])
    gk = gk_ref[...]
    for h in range(MLA_HEADS):
        kh = kf[:, h * HEAD_PAD:(h + 1) * HEAD_PAD] + kpe_wide
        ss = jnp.sum(kh * kh, axis=-1, keepdims=True)
        kn = kh * lax.rsqrt(ss * (1.0 / QK_DIM) + EPS) * gk
        k_out[:, h * HEAD_PAD:(h + 1) * HEAD_PAD] = _rope_packed(kn, cos, sup, sdn).astype(bf16)

    g1 = seg(OFF_GATE_MLA, MLA_WIDTH)
    gates_out[:, 0:MLA_WIDTH] = g1 * jax.nn.sigmoid(g1)
    g2 = seg(OFF_GATE_GLA, GLA_WIDTH)
    gates_out[:, MLA_WIDTH:MLA_WIDTH + GLA_WIDTH] = g2 * jax.nn.sigmoid(g2)
    g3 = seg(OFF_GATE_MEM, MEM_WIDTH)
    gates_out[:, MLA_WIDTH + GLA_WIDTH:D_MIX] = g3 * jax.nn.sigmoid(g3)

    glaq_out[...] = seg(OFF_GQ, GLA_KDIM) * (GLA_DK ** -0.5)
    glak_out[...] = seg(OFF_GK, GLA_KDIM)
    glav_out[...] = seg(OFF_GV, GLA_WIDTH)
    gl = _mm(misc.astype(bf16), wgk_ref[...]) + bgk_ref[...]
    glag_out[...] = jax.nn.log_sigmoid(gl) * (1.0 / GLA_NORMALIZER)

    mq = seg(OFF_MQ, MEM_WIDTH)
    gmq = gmq_ref[...]
    for j in range(MEM_WIDTH // LANES):
        blk = _half_head_rms(mq[:, j * LANES:(j + 1) * LANES], gmq, lane)
        memq_out[:, j * LANES:(j + 1) * LANES] = (blk * MEM_SCALE).astype(bf16)


def _inproj(x2, tabs, w, tm):
    T = x2.shape[0]
    nt = tabs[0].shape[0] // tm
    row = lambda i: (i, 0)
    fix = lambda i: (0, 0)
    tab = lambda i: (i % nt, 0)

    def full(a):
        return pl.BlockSpec(a.shape, fix)

    consts1 = (w['g_pre'], w['w_in'], w['g_qa'], w['w_uq'], w['g_kva'], w['w_uk'], w['g_q'], w['g_k'])
    consts2 = (w['w_gk'], w['b_gk'], w['g_mem_q'])
    outs = [(QK_PACKED, bf16), (QK_PACKED, bf16), (KV_RANK, f32), (ROPE_DIM, f32), (D_MIX, f32),
            (GLA_KDIM, f32), (GLA_KDIM, f32), (GLA_WIDTH, f32), (GLA_KDIM, f32), (MEM_WIDTH, bf16)]
    return pl.pallas_call(
        _inproj_body,
        grid=(T // tm,),
        in_specs=[pl.BlockSpec((tm, D_MODEL), row)] + [full(a) for a in consts1]
        + [pl.BlockSpec((tm, LANES), tab)] * 3 + [full(a) for a in consts2],
        out_specs=[pl.BlockSpec((tm, n), row) for n, _ in outs],
        out_shape=[jax.ShapeDtypeStruct((T, n), dt) for n, dt in outs],
        compiler_params=_cparams("arbitrary"),
        name="inproj",
    )(x2, *consts1, *tabs, *consts2)


def _mla_prompt_body(q_ref, k_ref, c_ref, wuvt_ref, o_ref, ct_scr, acc_scr, m_scr, l_scr, ot_scr, *, tq):
    i = pl.program_id(1)
    nblk = ct_scr.shape[0]

    @pl.when(i == 0)
    def _():
        for j in range(nblk):
            ct_scr[j] = c_ref[j * tq:(j + 1) * tq, :].T.astype(bf16)

    m_scr[...] = jnp.full(m_scr.shape, NEG_INF, f32)
    l_scr[...] = jnp.zeros(l_scr.shape, f32)
    acc_scr[...] = jnp.zeros(acc_scr.shape, f32)
    key = lax.broadcasted_iota(jnp.int32, (tq, tq), 0)
    qry = lax.broadcasted_iota(jnp.int32, (tq, tq), 1)
    causal = key <= qry

    def step(kb, masked):
        ks = pl.ds(pl.multiple_of(kb * tq, tq), tq)
        ct = ct_scr[kb]
        for h in range(MLA_HEADS):
            hs = slice(h * HEAD_PAD, (h + 1) * HEAD_PAD)
            s = _nt(k_ref[ks, hs], q_ref[:, hs])
            if masked:
                s = jnp.where(causal, s, NEG_INF)
            m_old = m_scr[h:h + 1, :]
            m_new = jnp.maximum(m_old, jnp.max(s, axis=0, keepdims=True))
            alpha = jnp.exp(m_old - m_new)
            p = jnp.exp(s - m_new)
            l_scr[h:h + 1, :] = l_scr[h:h + 1, :] * alpha + jnp.sum(p, axis=0, keepdims=True)
            acc_scr[h] = acc_scr[h] * alpha + _mm(ct, p.astype(bf16))
            m_scr[h:h + 1, :] = m_new

    def body(kb, carry):
        step(kb, False)
        return carry

    lax.fori_loop(0, i, body, 0)
    step(i, True)

    for h in range(MLA_HEADS):
        lat_t = (acc_scr[h] / l_scr[h:h + 1, :]).astype(bf16)
        ot_scr[h * V_DIM:(h + 1) * V_DIM, :] = _mm(wuvt_ref[h], lat_t)
    o_ref[...] = ot_scr[...].T


def _mla_prompt(q, k, c, wuv_t, nb, seq, tq):
    nq = seq // tq
    return pl.pallas_call(
        functools.partial(_mla_prompt_body, tq=tq),
        grid=(nb, nq),
        in_specs=[pl.BlockSpec((tq, QK_PACKED), lambda b, i: (b * nq + i, 0)),
                  pl.BlockSpec((seq, QK_PACKED), lambda b, i: (b, 0)),
                  pl.BlockSpec((seq, KV_RANK), lambda b, i: (b, 0)),
                  pl.BlockSpec(wuv_t.shape, lambda b, i: (0, 0, 0))],
        out_specs=pl.BlockSpec((tq, MLA_WIDTH), lambda b, i: (b * nq + i, 0)),
        out_shape=jax.ShapeDtypeStruct((nb * seq, MLA_WIDTH), f32),
        scratch_shapes=[pltpu.VMEM((nq, KV_RANK, tq), bf16), pltpu.VMEM((MLA_HEADS, KV_RANK, tq), f32),
                        pltpu.VMEM((MLA_HEADS, tq), f32), pltpu.VMEM((MLA_HEADS, tq), f32),
                        pltpu.VMEM((MLA_WIDTH, tq), f32)],
        compiler_params=_cparams("arbitrary", "arbitrary"),
        name="mla_prompt",
    )(q, k, c, wuv_t)


def _mla_sample_body(pt_ref, q_ref, cnew_ref, kpenewt_ref, wukt_ref, wukg_ref, gkr_ref, cost_ref, sint_ref, wuv_ref,
                     ckv_hbm, kpe_hbm, o_ref,
                     cbuf, kbuf, sem, lhs_scr, qr_scr, cb_scr, s_scr, ql_all, qr_all, acc_scr, m_scr, l_scr,
                     *, layer, gp, n_pages, t_new):
    ndb = ql_all.shape[0]
    ng = (n_pages + 1) // gp
    kb = gp * PAGE_SIZE
    total = ndb * ng
    nrow = MLA_HEADS * t_new
    n_nope = MLA_HEADS * NOPE_DIM
    new_rows = pl.ds(kb - PAGE_SIZE, t_new)
    new_lanes = pl.ds(kb - PAGE_SIZE, PAGE_SIZE)

    def group_copies(t):
        b, g, slot = t // ng, t % ng, t % PAGE_BUFFERS
        cps = []
        for i in range(gp):
            page = pt_ref[b * n_pages + jnp.minimum(g * gp + i, n_pages - 1)]
            lanes = pl.ds(i * PAGE_SIZE, PAGE_SIZE)
            cps.append(pltpu.make_async_copy(ckv_hbm.at[layer, page], cbuf.at[slot, lanes, :], sem.at[0, slot]))
            cps.append(pltpu.make_async_copy(kpe_hbm.at[layer, page], kbuf.at[slot, :, lanes], sem.at[1, slot]))
        return cps

    def reset_state():
        m_scr[...] = jnp.full((nrow, 1), NEG_INF, f32)
        l_scr[...] = jnp.zeros((nrow, 1), f32)
        acc_scr[...] = jnp.zeros((nrow, KV_RANK), f32)

    gkr = gkr_ref[...]
    key_lane = lax.broadcasted_iota(jnp.int32, (nrow, PAGE_SIZE), 1)
    qry_row = lax.broadcasted_iota(jnp.int32, (nrow, PAGE_SIZE), 0) % t_new

    def prepare_group(t):
        b, g, pslot = t // ng, t % ng, t % PAGE_BUFFERS
        lhs_scr[n_nope:, :] = ql_all[b].astype(bf16)
        qr_scr[...] = qr_all[b].astype(bf16)
        for cp in group_copies(t):
            cp.wait()
        is_last = g == ng - 1
        cbuf[pslot, new_rows, :] = jnp.where(is_last, cnew_ref[b], cbuf[pslot, new_rows, :])
        kbuf[pslot, :, new_lanes] = jnp.where(is_last, kpenewt_ref[b], kbuf[pslot, :, new_lanes])

    def score_group(t, slot):
        g, pslot = t % ng, t % PAGE_BUFFERS
        is_last = g == ng - 1
        cb = cbuf[pslot].astype(bf16)
        cb_scr[slot] = cb
        kt = kbuf[pslot]
        cos_t, sin_t = cost_ref[g], sint_ref[g]
        big = _nt(lhs_scr[...], cb)
        kn = big[0:n_nope]
        ss_nope = jnp.sum((kn * kn).reshape(MLA_HEADS, NOPE_DIM, kb), axis=1)
        ss = ss_nope + jnp.sum(kt * kt, axis=0, keepdims=True)
        r = lax.rsqrt(ss * (1.0 / QK_DIM) + EPS)
        kg = kt * gkr
        k1, k2 = kg[0:HALF_ROPE], kg[HALF_ROPE:ROPE_DIM]
        kr = jnp.concatenate([k1 * cos_t - k2 * sin_t, k2 * cos_t + k1 * sin_t], axis=0).astype(bf16)
        s = big[n_nope:] + _mm(qr_scr[...], kr)
        s = (s.reshape(MLA_HEADS, t_new, kb) * r[:, None, :]).reshape(nrow, kb)
        first_dead = jnp.where(is_last, 1, PAGE_SIZE + t_new)
        tail = jnp.where(key_lane >= qry_row + first_dead, NEG_INF, s[:, kb - PAGE_SIZE:])
        s_scr[slot] = jnp.concatenate([s[:, :kb - PAGE_SIZE], tail], axis=1)

    def value_group(slot):
        s = s_scr[slot]
        m_old = m_scr[...]
        m_new = jnp.maximum(m_old, jnp.max(s, axis=-1, keepdims=True))
        alpha = jnp.exp(m_old - m_new)
        p = jnp.exp(s - m_new)
        l_scr[...] = l_scr[...] * alpha + jnp.sum(p, axis=-1, keepdims=True)
        acc_scr[...] = acc_scr[...] * alpha + _mm(p.astype(bf16), cb_scr[slot])
        m_scr[...] = m_new

    def finish_batch(b, closed):
        ql_all[b] = jnp.where(closed, acc_scr[...] / l_scr[...], ql_all[b])
        m_scr[...] = jnp.where(closed, NEG_INF, m_scr[...])
        l_scr[...] = jnp.where(closed, 0.0, l_scr[...])
        acc_scr[...] = jnp.where(closed, 0.0, acc_scr[...])

    lhs_scr[0:n_nope, :] = wukt_ref[...]
    for h in range(MLA_HEADS):
        q_h = q_ref[:, h * HEAD_PAD:(h + 1) * HEAD_PAD]
        rows = slice(h * t_new, (h + 1) * t_new)
        ql_all[:, rows, :] = _mm(q_h, wukg_ref[h]).reshape(ndb, t_new, KV_RANK)
        qr_all[:, rows, :] = q_h[:, KPE_LANE:KPE_LANE + ROPE_DIM].astype(f32).reshape(ndb, t_new, ROPE_DIM)
    reset_state()

    ahead = PAGE_BUFFERS - 1

    def skewed_step(t, slot, prefetch):
        if prefetch:
            for cp in group_copies(t + ahead):
                cp.start()
        prepare_group(t)
        score_group(t, slot)
        value_group(1 - slot)
        closes = t % ng == 0
        finish_batch(jnp.maximum(t // ng - 1, 0), closes)

    for t in range(min(ahead, total)):
        for cp in group_copies(t):
            cp.start()
    if total > ahead:
        for cp in group_copies(ahead):
            cp.start()
    prepare_group(0)
    score_group(0, 0)

    n_pairs = max(total - 1 - ahead, 0) // 2

    def body(k, carry):
        skewed_step(2 * k + 1, 1, True)
        skewed_step(2 * k + 2, 0, True)
        return carry

    lax.fori_loop(0, n_pairs, body, 0)
    for t in range(2 * n_pairs + 1, total):
        skewed_step(t, t % 2, t + ahead < total)
    value_group((total - 1) % 2)
    finish_batch(ndb - 1, True)

    for h in range(MLA_HEADS):
        lat_h = ql_all[:, h * t_new:(h + 1) * t_new, :].reshape(ndb * t_new, KV_RANK).astype(bf16)
        contrib = _mm(lat_h, wuv_ref[h])
        ps = slice((h // 2) * LANES, (h // 2 + 1) * LANES)
        if h % 2 == 0:
            o_ref[:, ps] = contrib
        else:
            o_ref[:, ps] += contrib


def _mla_sample(layer, q, c_new, kpe_new, cache_ckv, cache_kpe_t, page_table, w, tabs_t, gp):
    ndb, n_pages = page_table.shape
    t_new = q.shape[0] // ndb
    ng = (n_pages + 1) // gp
    kb = gp * PAGE_SIZE
    nrow = MLA_HEADS * t_new
    cost, sint = tabs_t
    by_group = lambda a: a.reshape(HALF_ROPE, ng, kb).transpose(1, 0, 2)
    kpe_new_t = jnp.pad(kpe_new.reshape(ndb, t_new, ROPE_DIM).transpose(0, 2, 1),
                        ((0, 0), (0, 0), (0, PAGE_SIZE - t_new)))
    args = (q, c_new.reshape(ndb, t_new, KV_RANK), kpe_new_t, w['w_uk_t'], w['w_uk_g'], w['g_k_rope'],
            by_group(cost), by_group(sint), w['w_uv_pairs'])

    def full(a):
        return pl.BlockSpec(a.shape, lambda i, pt, nd=a.ndim: (0,) * nd)

    return pl.pallas_call(
        functools.partial(_mla_sample_body, layer=layer, gp=gp, n_pages=n_pages, t_new=t_new),
        grid_spec=pltpu.PrefetchScalarGridSpec(
            num_scalar_prefetch=1,
            grid=(1,),
            in_specs=[full(a) for a in args] + [pl.BlockSpec(memory_space=pl.ANY)] * 2,
            out_specs=pl.BlockSpec((ndb * t_new, MLA_WIDTH), lambda i, pt: (0, 0)),
            scratch_shapes=[pltpu.VMEM((PAGE_BUFFERS, kb, KV_RANK), f32),
                            pltpu.VMEM((PAGE_BUFFERS, ROPE_DIM, kb), f32),
                            pltpu.SemaphoreType.DMA((2, PAGE_BUFFERS)),
                            pltpu.VMEM((MLA_HEADS * NOPE_DIM + nrow, KV_RANK), bf16),
                            pltpu.VMEM((nrow, ROPE_DIM), bf16),
                            pltpu.VMEM((2, kb, KV_RANK), bf16),
                            pltpu.VMEM((2, nrow, kb), f32),
                            pltpu.VMEM((ndb, nrow, KV_RANK), f32),
                            pltpu.VMEM((ndb, nrow, ROPE_DIM), f32),
                            pltpu.VMEM((nrow, KV_RANK), f32),
                            pltpu.VMEM((nrow, 1), f32),
                            pltpu.VMEM((nrow, 1), f32)]),
        out_shape=jax.ShapeDtypeStruct((ndb * t_new, MLA_WIDTH), f32),
        compiler_params=_cparams("arbitrary"),
        name="mla_sample",
    )(page_table.reshape(-1), *args, cache_ckv, cache_kpe_t)


def _split3(x):
    hi = x.astype(bf16)
    r1 = x - hi.astype(f32)
    mid = r1.astype(bf16)
    lo = (r1 - mid.astype(f32)).astype(bf16)
    return hi, mid, lo


def _gla_body(*refs, nb, tt, cs, has_init):
    if has_init:
        q_ref, k_ref, v_ref, g_ref, s0_ref, o_ref, sT_out, st_scr = refs
    else:
        q_ref, k_ref, v_ref, g_ref, o_ref, sT_out, st_scr = refs
    ti = pl.program_id(1)
    nchunk = tt // cs

    @pl.when(ti == 0)
    def _():
        if has_init:
            st_scr[...] = s0_ref[...]
        else:
            st_scr[...] = jnp.zeros(st_scr.shape, f32)

    tri = (lax.broadcasted_iota(jnp.int32, (tt, tt), 1) <= lax.broadcasted_iota(jnp.int32, (tt, tt), 0)).astype(bf16)
    kv_head = (lax.broadcasted_iota(jnp.int32, (GLA_KDIM, GLA_WIDTH), 0) // GLA_DK
               == lax.broadcasted_iota(jnp.int32, (GLA_KDIM, GLA_WIDTH), 1) // GLA_DV)
    block_ones = kv_head.astype(bf16)
    vk_head = (lax.broadcasted_iota(jnp.int32, (GLA_WIDTH, GLA_KDIM), 0) // GLA_DV
               == lax.broadcasted_iota(jnp.int32, (GLA_WIDTH, GLA_KDIM), 1) // GLA_DK)
    sel_t = (lax.broadcasted_iota(jnp.int32, (cs, cs * cs), 1) // cs
             == lax.broadcasted_iota(jnp.int32, (cs, cs * cs), 0)).astype(bf16)
    t_idx = lax.broadcasted_iota(jnp.int32, (cs, cs, GLA_KDIM), 0)
    s_idx = lax.broadcasted_iota(jnp.int32, (cs, cs, GLA_KDIM), 1)
    causal3 = s_idx <= t_idx

    cums = []
    for b in range(nb):
        hi, mid, lo = _split3(g_ref[b])
        cums.append(_mm(tri, hi) + _mm(tri, mid) + _mm(tri, lo))
    for n in range(nchunk):
        sl = slice(n * cs, (n + 1) * cs)
        for b in range(nb):
            cum = cums[b]
            base = cum[n * cs - 1:n * cs] if n > 0 else jnp.zeros((1, GLA_KDIM), f32)
            bc = cum[sl] - base
            b_last = bc[cs - 1:cs]
            qc, kc, vc = q_ref[b, sl, :], k_ref[b, sl, :], v_ref[b, sl, :]
            st = st_scr[b]
            o_inter = _nt((qc * jnp.exp(bc)).astype(bf16), st.astype(bf16))
            diff = jnp.where(causal3, bc[:, None, :] - bc[None, :, :], NEG_INF)
            d3 = qc[:, None, :] * kc[None, :, :] * jnp.exp(diff)
            a_exp = _mm(d3.reshape(cs * cs, GLA_KDIM).astype(bf16), block_ones)
            xv = (a_exp.reshape(cs, cs, GLA_WIDTH) * vc[None, :, :]).reshape(cs * cs, GLA_WIDTH)
            o_intra = _mm(sel_t, xv.astype(bf16))
            o_ref[b, sl, :] = o_inter + o_intra
            kd = (kc * jnp.exp(b_last - bc)).astype(bf16)
            upd = _tn(vc.astype(bf16), kd)
            st_scr[b] = st * jnp.exp(b_last) + jnp.where(vk_head, upd, 0.0)

    @pl.when(ti == pl.num_programs(1) - 1)
    def _():
        sT_out[...] = st_scr[...]


def _gla(q, k, v, g, s0t, nb, tt, cs):
    B, L, _ = q.shape
    has_init = s0t is not None
    tok = lambda bi, ti: (bi, ti, 0)
    st = lambda bi, ti: (bi, 0, 0)
    in_specs = [pl.BlockSpec((nb, tt, GLA_KDIM), tok), pl.BlockSpec((nb, tt, GLA_KDIM), tok),
                pl.BlockSpec((nb, tt, GLA_WIDTH), tok), pl.BlockSpec((nb, tt, GLA_KDIM), tok)]
    args = [q, k, v, g]
    if has_init:
        in_specs.append(pl.BlockSpec((nb, GLA_WIDTH, GLA_KDIM), st))
        args.append(s0t)
    return pl.pallas_call(
        functools.partial(_gla_body, nb=nb, tt=tt, cs=cs, has_init=has_init),
        grid=(B // nb, L // tt),
        in_specs=in_specs,
        out_specs=[pl.BlockSpec((nb, tt, GLA_WIDTH), tok), pl.BlockSpec((nb, GLA_WIDTH, GLA_KDIM), st)],
        out_shape=[jax.ShapeDtypeStruct((B, L, GLA_WIDTH), f32), jax.ShapeDtypeStruct((B, GLA_WIDTH, GLA_KDIM), f32)],
        scratch_shapes=[pltpu.VMEM((nb, GLA_WIDTH, GLA_KDIM), f32)],
        compiler_params=_cparams("arbitrary", "arbitrary"),
        name="gla",
    )(*args)


def _memkv_body(mem_ref, gmem_ref, w_ref, gk_ref, mkt_out, mvt_out):
    xb = _rms_rows(mem_ref[...], gmem_ref[...]).astype(bf16)
    kv = _mm(xb, w_ref[...])
    lane = lax.broadcasted_iota(jnp.int32, (1, LANES), 1)
    gk = gk_ref[...]
    for j in range(MEM_WIDTH // LANES):
        blk = _half_head_rms(kv[:, j * LANES:(j + 1) * LANES], gk, lane)
        mkt_out[j * LANES:(j + 1) * LANES, :] = blk.T
        mvt_out[j * LANES:(j + 1) * LANES, :] = kv[:, MEM_WIDTH + j * LANES:MEM_WIDTH + (j + 1) * LANES].T


def _memkv(mem2, w, nb, n_mem):
    row = lambda i: (i, 0)
    fix = lambda i: (0, 0)
    return pl.pallas_call(
        _memkv_body,
        grid=(nb,),
        in_specs=[pl.BlockSpec((n_mem, D_MODEL), row), pl.BlockSpec(w['g_mem'].shape, fix),
                  pl.BlockSpec(w['w_mem_kv'].shape, fix), pl.BlockSpec(w['g_mem_k'].shape, fix)],
        out_specs=[pl.BlockSpec((None, MEM_WIDTH, n_mem), lambda i: (i, 0, 0))] * 2,
        out_shape=[jax.ShapeDtypeStruct((nb, MEM_WIDTH, n_mem), f32)] * 2,
        compiler_params=_cparams("arbitrary"),
        name="memkv",
    )(mem2, w['g_mem'], w['w_mem_kv'], w['g_mem_k'])


def _memattn_body(q_ref, mkt_ref, mvt_ref, o_ref, *, nb, tq):
    lane_head = lax.broadcasted_iota(jnp.int32, (1, MEM_WIDTH), 1) // MEM_HEAD_DIM
    for b in range(nb):
        q = q_ref[b]
        mkt = mkt_ref[b].astype(bf16)
        mvt = mvt_ref[b].astype(bf16)
        qs = jnp.concatenate([jnp.where(lane_head == h, q, jnp.zeros_like(q)) for h in range(MEM_HEADS)], axis=0)
        s = _mm(qs, mkt)
        p = jnp.exp(s - jnp.max(s, axis=-1, keepdims=True))
        pv = _nt(p.astype(bf16), mvt) / jnp.sum(p, axis=-1, keepdims=True)
        o = jnp.zeros((tq, MEM_WIDTH), f32)
        for h in range(MEM_HEADS):
            o = o + jnp.where(lane_head == h, pv[h * tq:(h + 1) * tq], 0.0)
        o_ref[b] = o


def _memattn(layer, q3, mkt4, mvt4, nb, tq):
    B, L, _ = q3.shape
    n_mem = mkt4.shape[-1]
    kv_spec = pl.BlockSpec((None, nb, MEM_WIDTH, n_mem), lambda b, i: (layer, b, 0, 0))
    return pl.pallas_call(
        functools.partial(_memattn_body, nb=nb, tq=tq),
        grid=(B // nb, L // tq),
        in_specs=[pl.BlockSpec((nb, tq, MEM_WIDTH), lambda b, i: (b, i, 0)), kv_spec, kv_spec],
        out_specs=pl.BlockSpec((nb, tq, MEM_WIDTH), lambda b, i: (b, i, 0)),
        out_shape=jax.ShapeDtypeStruct((B, L, MEM_WIDTH), f32),
        compiler_params=_cparams("arbitrary", "arbitrary"),
        name="memattn",
    )(q3, mkt4, mvt4)


def _merge_body(x_ref, mla_ref, gla_ref, mem_ref, gates_ref, ggla_ref, wout_ref, y_ref):
    lane = lax.broadcasted_iota(jnp.int32, (1, LANES), 1)
    ggla = ggla_ref[...]
    y = x_ref[...]
    m1 = (gates_ref[:, 0:MLA_WIDTH] * mla_ref[...]).astype(bf16)
    y = y + _mm(m1, wout_ref[0:MLA_WIDTH, :])
    for j in range(GLA_WIDTH // LANES):
        ls = slice(j * LANES, (j + 1) * LANES)
        gn = _half_head_rms(gla_ref[:, ls], ggla, lane)
        gs = slice(MLA_WIDTH + j * LANES, MLA_WIDTH + (j + 1) * LANES)
        y = y + _mm((gates_ref[:, gs] * gn).astype(bf16), wout_ref[gs, :])
    ms = slice(MLA_WIDTH + GLA_WIDTH, D_MIX)
    m3 = (gates_ref[:, ms] * mem_ref[...]).astype(bf16)
    y_ref[...] = y + _mm(m3, wout_ref[ms, :])


def _merge(x2, mla_o, gla_o, mem_o, gates, w, tm):
    T = x2.shape[0]
    row = lambda i: (i, 0)
    fix = lambda i: (0, 0)
    return pl.pallas_call(
        _merge_body,
        grid=(T // tm,),
        in_specs=[pl.BlockSpec((tm, D_MODEL), row), pl.BlockSpec((tm, MLA_WIDTH), row),
                  pl.BlockSpec((tm, GLA_WIDTH), row), pl.BlockSpec((tm, MEM_WIDTH), row),
                  pl.BlockSpec((tm, D_MIX), row), pl.BlockSpec(w['g_gla_o'].shape, fix),
                  pl.BlockSpec(w['w_out'].shape, fix)],
        out_specs=pl.BlockSpec((tm, D_MODEL), row),
        out_shape=jax.ShapeDtypeStruct((T, D_MODEL), f32),
        compiler_params=_cparams("arbitrary"),
        name="merge",
    )(x2, mla_o, gla_o, mem_o, gates, w['g_gla_o'], w['w_out'])


def _pad_head(a):
    return jnp.pad(a, [(0, 0)] * (a.ndim - 1) + [(0, HEAD_PAD - QK_DIM)])


def _prep_layer(l, p):
    w_in = p['w_in'][l]
    cuts, o = [], 0
    for n in (Q_RANK, KV_RANK, ROPE_DIM, MLA_WIDTH, GLA_KDIM, GLA_KDIM, GLA_WIDTH, GLA_GATE_RANK, GLA_WIDTH,
              MEM_WIDTH, MEM_WIDTH):
        cuts.append(w_in[:, o:o + n])
        o += n
    cq, ckv, kpe, gate_mla, gq, gk, gv, gg, gate_gla, mq, gate_mem = cuts
    z = lambda n: jnp.zeros((D_MODEL, n), f32)
    misc = jnp.concatenate([gg, z(KPE_LANE - GLA_GATE_RANK), kpe, z(LANES - KPE_LANE - ROPE_DIM)], axis=1)
    w_in_packed = jnp.concatenate([cq, ckv, gate_mla, gq, gk, gv, gate_gla, mq, gate_mem, misc], axis=1)

    w_uk = p['w_uk'][l]
    g_k = p['g_mla_k'][l]
    wuk_packed = jnp.pad(w_uk, ((0, 0), (0, 0), (0, HEAD_PAD - NOPE_DIM))).reshape(KV_RANK, QK_PACKED)
    w_uk_t = w_uk.reshape(KV_RANK, MLA_HEADS * NOPE_DIM).T
    w_uk_g = jnp.pad((w_uk * g_k[None, None, :NOPE_DIM]).transpose(1, 2, 0),
                     ((0, 0), (0, HEAD_PAD - NOPE_DIM), (0, 0)))
    w_uv = p['w_uv'][l].transpose(1, 0, 2)
    zeros = jnp.zeros_like(w_uv)
    even = jnp.arange(MLA_HEADS)[:, None, None] % 2 == 0
    w_uv_pairs = jnp.concatenate([jnp.where(even, w_uv, zeros), jnp.where(even, zeros, w_uv)], axis=-1)
    w_gk = jnp.pad(p['w_gk'][l], ((0, LANES - GLA_GATE_RANK), (0, 0)))
    row = lambda a: a.reshape(1, -1)
    return {
        'g_pre': row(p['g_pre'][l]), 'w_in': w_in_packed.astype(bf16), 'g_qa': row(p['g_qa'][l]),
        'w_uq': _pad_head(p['w_uq'][l]).reshape(Q_RANK, QK_PACKED).astype(bf16),
        'g_kva': row(p['g_kva'][l]), 'w_uk': wuk_packed.astype(bf16),
        'g_q': row(_pad_head(p['g_mla_q'][l])), 'g_k': row(_pad_head(g_k)),
        'w_gk': w_gk.astype(bf16), 'b_gk': row(p['b_gk'][l]),
        'g_mem_q': row(jnp.tile(p['g_mem_q'][l], 2)), 'g_mem_k': row(jnp.tile(p['g_mem_k'][l], 2)),
        'g_gla_o': row(jnp.tile(p['g_gla_o'][l], 2)),
        'w_uk_t': w_uk_t.astype(bf16), 'w_uk_g': w_uk_g.astype(bf16),
        'g_k_rope': g_k[NOPE_DIM:].reshape(ROPE_DIM, 1),
        'w_uv_pairs': w_uv_pairs.astype(bf16),
        'w_uv_t': w_uv.transpose(0, 2, 1).astype(bf16),
        'g_mem': row(p['g_mem'][l]), 'w_mem_kv': p['w_mem_kv'][l].astype(bf16),
        'w_out': p['w_out'][l].astype(bf16),
    }


def _angles(pos):
    inv_freq = ROPE_THETA ** (-jnp.arange(HALF_ROPE, dtype=f32) * (2.0 / ROPE_DIM))
    return pos.astype(f32)[:, None] * inv_freq[None, :]


def _packed_tables(pos):
    ang = _angles(pos)
    n = pos.shape[0]
    cos, sin = jnp.cos(ang), jnp.sin(ang)
    one = jnp.ones((n, NOPE_DIM), f32)
    z = lambda w: jnp.zeros((n, w), f32)
    tail = HEAD_PAD - QK_DIM
    cos_p = jnp.concatenate([one, cos, cos, z(tail)], axis=1)
    sin_up = jnp.concatenate([z(NOPE_DIM + HALF_ROPE), sin, z(tail)], axis=1)
    sin_dn = jnp.concatenate([z(NOPE_DIM), -sin, z(HALF_ROPE + tail)], axis=1)
    return cos_p, sin_up, sin_dn


def _transposed_tables(pos):
    ang = _angles(pos).T
    return jnp.cos(ang), jnp.sin(ang)


def _state_to_blockdiag_t(s):
    B = s.shape[0]
    st = s.transpose(0, 1, 3, 2)
    eye = jnp.eye(GLA_HEADS, dtype=s.dtype)
    return jnp.einsum('bhvk,hg->bhvgk', st, eye).reshape(B, GLA_WIDTH, GLA_KDIM)


def _blockdiag_t_to_state(st):
    B = st.shape[0]
    s5 = st.reshape(B, GLA_HEADS, GLA_DV, GLA_HEADS, GLA_DK)
    diag = jnp.stack([s5[:, h, :, h, :] for h in range(GLA_HEADS)], axis=1)
    return diag.transpose(0, 1, 3, 2)


def kernel(x_prompt, x_sample, mem_prompt, cache_ckv, cache_kpe, page_table, state_gla, cache_mem_k, cache_mem_v, g_pre, w_in, g_qa, w_uq, g_kva, w_uk, w_uv, g_mla_q, g_mla_k, w_gk, b_gk, g_gla_o, g_mem, w_mem_kv, g_mem_q, g_mem_k, w_out):
    params = dict(g_pre=g_pre, w_in=w_in, g_qa=g_qa, w_uq=w_uq, g_kva=g_kva, w_uk=w_uk, w_uv=w_uv,
                  g_mla_q=g_mla_q, g_mla_k=g_mla_k, w_gk=w_gk, b_gk=b_gk, g_gla_o=g_gla_o, g_mem=g_mem,
                  w_mem_kv=w_mem_kv, g_mem_q=g_mem_q, g_mem_k=g_mem_k, w_out=w_out)
    nb, seq, _ = x_prompt.shape
    ndb, t_new, _ = x_sample.shape
    n_mem = mem_prompt.shape[1]
    depth = w_in.shape[0]
    n_pages = page_table.shape[1]
    past_len = n_pages * cache_ckv.shape[2]

    tm_p = min(256, seq)
    tm_merge = min(512, seq)
    tq = min(256, seq)
    tm_s = min(256, ndb * t_new)
    pp = max(d for d in range(1, 17) if (n_pages + 1) % d == 0)
    gla_tt = min(128, seq)
    gla_cs = min(16, gla_tt)
    nb_s = min(8, ndb)
    nb_gla = min(8, nb)

    cache_kpe_t = cache_kpe.transpose(0, 1, 3, 2)
    to_t = lambda a: a.transpose(0, 1, 3, 4, 2).reshape(depth, ndb, MEM_WIDTH, n_mem)
    cache_mkt, cache_mvt = to_t(cache_mem_k), to_t(cache_mem_v)
    from_t = lambda a: a.reshape(nb, MEM_HEADS, MEM_HEAD_DIM, n_mem).transpose(0, 3, 1, 2)

    tabs_p = _packed_tables(jnp.arange(seq, dtype=jnp.int32))
    pos_s = past_len + jnp.arange(t_new, dtype=jnp.int32)
    tabs_s = tuple(jnp.tile(t, (tm_s // t_new, 1)) for t in _packed_tables(pos_s))
    tabs_t = _transposed_tables(jnp.arange(past_len + PAGE_SIZE, dtype=jnp.int32))

    xp = x_prompt.reshape(nb * seq, D_MODEL)
    xs = x_sample.reshape(ndb * t_new, D_MODEL)
    mem2 = mem_prompt.reshape(nb * n_mem, D_MODEL)
    ckv_p, kpe_p, gla_p, mk_p, mv_p, ckv_s, kpe_s, gla_s = ([] for _ in range(8))
    for l in range(depth):
        w = _prep_layer(l, params)
        q, k, c, kpe, gates, gq, gk, gv, gg, mq = _inproj(xp, tabs_p, w, tm_p)
        mla_o = _mla_prompt(q, k, c, w['w_uv_t'], nb, seq, tq)
        r3 = lambda a: a.reshape(nb, seq, a.shape[-1])
        gla_o, st = _gla(r3(gq), r3(gk), r3(gv), r3(gg), None, nb_gla, gla_tt, gla_cs)
        mkt, mvt = _memkv(mem2, w, nb, n_mem)
        mem_o = _memattn(0, r3(mq), mkt[None], mvt[None], 1, tq)
        xp = _merge(xp, mla_o, gla_o.reshape(nb * seq, GLA_WIDTH), mem_o.reshape(nb * seq, MEM_WIDTH), gates, w, tm_merge)
        ckv_p.append(c.reshape(nb, seq, KV_RANK))
        kpe_p.append(kpe.reshape(nb, seq, ROPE_DIM))
        gla_p.append(_blockdiag_t_to_state(st))
        mk_p.append(from_t(mkt))
        mv_p.append(from_t(mvt))
        q, k, c, kpe, gates, gq, gk, gv, gg, mq = _inproj(xs, tabs_s, w, tm_s)
        mla_o = _mla_sample(l, q, c, kpe, cache_ckv, cache_kpe_t, page_table, w, tabs_t, pp)
        r3 = lambda a: a.reshape(ndb, t_new, a.shape[-1])
        gla_o, st = _gla(r3(gq), r3(gk), r3(gv), r3(gg), _state_to_blockdiag_t(state_gla[l]), nb_s, t_new, t_new)
        mem_o = _memattn(l, r3(mq), cache_mkt, cache_mvt, nb_s, t_new)
        xs = _merge(xs, mla_o, gla_o.reshape(ndb * t_new, GLA_WIDTH), mem_o.reshape(ndb * t_new, MEM_WIDTH), gates, w, tm_s)
        ckv_s.append(c.reshape(ndb, t_new, KV_RANK))
        kpe_s.append(kpe.reshape(ndb, t_new, ROPE_DIM))
        gla_s.append(_blockdiag_t_to_state(st))
    return (xp.reshape(nb, seq, D_MODEL), xs.reshape(ndb, t_new, D_MODEL), jnp.stack(ckv_p), jnp.stack(kpe_p),
            jnp.stack(gla_p), jnp.stack(mk_p), jnp.stack(mv_p), jnp.stack(ckv_s), jnp.stack(kpe_s), jnp.stack(gla_s))
```

```python
import functools

import jax
import jax.numpy as jnp
from jax import lax
from jax.experimental import pallas as pl
from jax.experimental.pallas import tpu as pltpu

f32, bf16 = jnp.float32, jnp.bfloat16

D_MODEL = 1024
PAGE_SIZE = 128
MLA_HEADS = 8
NOPE_DIM = 64
ROPE_DIM = 32
HALF_ROPE = ROPE_DIM // 2
QK_DIM = NOPE_DIM + ROPE_DIM
V_DIM = 64
Q_RANK = 384
KV_RANK = 256
MLA_WIDTH = MLA_HEADS * V_DIM
ROPE_THETA = 10000.0
MLA_SCALE = QK_DIM ** -0.5
LOG2_E = 1.4426950408889634
GLA_HEADS = 4
GLA_DK = 32
GLA_DV = 64
GLA_KDIM = GLA_HEADS * GLA_DK
GLA_WIDTH = GLA_HEADS * GLA_DV
GLA_GATE_RANK = 16
GLA_NORMALIZER = 16.0
MEM_HEADS = 4
MEM_HEAD_DIM = 64
MEM_WIDTH = MEM_HEADS * MEM_HEAD_DIM
MEM_SCALE = MEM_HEAD_DIM ** -0.5
D_MIX = MLA_WIDTH + GLA_WIDTH + MEM_WIDTH
EPS = 1e-6
NEG_INF = -1e30

LANES = 128
HEAD_PAD = LANES
QK_PACKED = MLA_HEADS * HEAD_PAD

OFF_CQ = 0
OFF_CKV = OFF_CQ + Q_RANK
OFF_GATE_MLA = OFF_CKV + KV_RANK
OFF_GQ = OFF_GATE_MLA + MLA_WIDTH
OFF_GK = OFF_GQ + GLA_KDIM
OFF_GV = OFF_GK + GLA_KDIM
OFF_GATE_GLA = OFF_GV + GLA_WIDTH
OFF_MQ = OFF_GATE_GLA + GLA_WIDTH
OFF_GATE_MEM = OFF_MQ + MEM_WIDTH
OFF_MISC = OFF_GATE_MEM + MEM_WIDTH
D_IN_PACKED = OFF_MISC + LANES
KPE_LANE = NOPE_DIM

VMEM_LIMIT = 56 * 1024 * 1024
PAGE_BUFFERS = 3


def _cparams(*sem):
    return pltpu.CompilerParams(dimension_semantics=sem, vmem_limit_bytes=VMEM_LIMIT)


def _nt(a, b):
    return lax.dot_general(a, b, (((1,), (1,)), ((), ())), preferred_element_type=f32)


def _tn(a, b):
    return lax.dot_general(a, b, (((0,), (0,)), ((), ())), preferred_element_type=f32)


def _mm(a, b):
    return jnp.dot(a, b, preferred_element_type=f32)


def _rms_rows(x, g):
    return x * lax.rsqrt(jnp.mean(x * x, axis=-1, keepdims=True) + EPS) * g


def _half_head_rms(blk, g, lane):
    sq = blk * blk
    lo = lane < 64
    ss_lo = jnp.sum(jnp.where(lo, sq, 0.0), axis=-1, keepdims=True)
    ss_hi = jnp.sum(jnp.where(lo, 0.0, sq), axis=-1, keepdims=True)
    ss = jnp.where(lo, ss_lo, ss_hi)
    return blk * lax.rsqrt(ss * (1.0 / 64.0) + EPS) * g


def _rope_packed(x, cos, sin_up, sin_dn):
    return x * cos + pltpu.roll(x, HALF_ROPE, 1) * sin_up + pltpu.roll(x, LANES - HALF_ROPE, 1) * sin_dn


def _inproj_body(x_ref, gpre_ref, win_ref, gqa_ref, wuq_ref, gkva_ref, wuk_ref, gq_ref, gk_ref,
                 cos_ref, sup_ref, sdn_ref, wgk_ref, bgk_ref, gmq_ref,
                 q_out, k_out, ckv_out, kpe_out, gates_out, glaq_out, glak_out, glav_out, glag_out, memq_out):
    x = x_ref[...]
    xb = _rms_rows(x, gpre_ref[...]).astype(bf16)

    def seg(off, n):
        return _mm(xb, win_ref[:, off:off + n])

    cos, sup, sdn = cos_ref[...], sup_ref[...], sdn_ref[...]
    lane = lax.broadcasted_iota(jnp.int32, (1, LANES), 1)

    cq = _rms_rows(seg(OFF_CQ, Q_RANK), gqa_ref[...]).astype(bf16)
    qf = _mm(cq, wuq_ref[...])
    gq = gq_ref[...]
    for h in range(MLA_HEADS):
        qh = qf[:, h * HEAD_PAD:(h + 1) * HEAD_PAD]
        ss = jnp.sum(qh * qh, axis=-1, keepdims=True)
        qn = qh * lax.rsqrt(ss * (1.0 / QK_DIM) + EPS) * gq
        q_out[:, h * HEAD_PAD:(h + 1) * HEAD_PAD] = (_rope_packed(qn, cos, sup, sdn) * (MLA_SCALE * LOG2_E)).astype(bf16)

    c = _rms_rows(seg(OFF_CKV, KV_RANK), gkva_ref[...])
    ckv_out[...] = c
    misc = seg(OFF_MISC, LANES)
    kpe_wide = jnp.where((lane >= KPE_LANE) & (lane < KPE_LANE + ROPE_DIM), misc, 0.0)
    kpe_out[...] = misc[:, KPE_LANE:KPE_LANE + ROPE_DIM]
    kf = _mm(c.astype(bf16), wuk_ref[...])
    gk = gk_ref[...]
    for h in range(MLA_HEADS):
        kh = kf[:, h * HEAD_PAD:(h + 1) * HEAD_PAD] + kpe_wide
        ss = jnp.sum(kh * kh, axis=-1, keepdims=True)
        kn = kh * lax.rsqrt(ss * (1.0 / QK_DIM) + EPS) * gk
        k_out[:, h * HEAD_PAD:(h + 1) * HEAD_PAD] = _rope_packed(kn, cos, sup, sdn).astype(bf16)

    g1 = seg(OFF_GATE_MLA, MLA_WIDTH)
    gates_out[:, 0:MLA_WIDTH] = g1 * jax.nn.sigmoid(g1)
    g2 = seg(OFF_GATE_GLA, GLA_WIDTH)
    gates_out[:, MLA_WIDTH:MLA_WIDTH + GLA_WIDTH] = g2 * jax.nn.sigmoid(g2)
    g3 = seg(OFF_GATE_MEM, MEM_WIDTH)
    gates_out[:, MLA_WIDTH + GLA_WIDTH:D_MIX] = g3 * jax.nn.sigmoid(g3)

    glaq_out[...] = seg(OFF_GQ, GLA_KDIM) * (GLA_DK ** -0.5)
    glak_out[...] = seg(OFF_GK, GLA_KDIM)
    glav_out[...] = seg(OFF_GV, GLA_WIDTH)
    gl = _mm(misc.astype(bf16), wgk_ref[...]) + bgk_ref[...]
    glag_out[...] = jax.nn.log_sigmoid(gl) * (LOG2_E / GLA_NORMALIZER)

    mq = seg(OFF_MQ, MEM_WIDTH)
    gmq = gmq_ref[...]
    for j in range(MEM_WIDTH // LANES):
        blk = _half_head_rms(mq[:, j * LANES:(j + 1) * LANES], gmq, lane)
        memq_out[:, j * LANES:(j + 1) * LANES] = (blk * (MEM_SCALE * LOG2_E)).astype(bf16)


def _inproj(x2, tabs, w, tm):
    T = x2.shape[0]
    nt = tabs[0].shape[0] // tm
    row = lambda i: (i, 0)
    fix = lambda i: (0, 0)
    tab = lambda i: (i % nt, 0)

    def full(a):
        return pl.BlockSpec(a.shape, fix)

    consts1 = (w['g_pre'], w['w_in'], w['g_qa'], w['w_uq'], w['g_kva'], w['w_uk'], w['g_q'], w['g_k'])
    consts2 = (w['w_gk'], w['b_gk'], w['g_mem_q'])
    outs = [(QK_PACKED, bf16), (QK_PACKED, bf16), (KV_RANK, f32), (ROPE_DIM, f32), (D_MIX, f32),
            (GLA_KDIM, f32), (GLA_KDIM, f32), (GLA_WIDTH, f32), (GLA_KDIM, f32), (MEM_WIDTH, bf16)]
    return pl.pallas_call(
        _inproj_body,
        grid=(T // tm,),
        in_specs=[pl.BlockSpec((tm, D_MODEL), row)] + [full(a) for a in consts1]
        + [pl.BlockSpec((tm, LANES), tab)] * 3 + [full(a) for a in consts2],
        out_specs=[pl.BlockSpec((tm, n), row) for n, _ in outs],
        out_shape=[jax.ShapeDtypeStruct((T, n), dt) for n, dt in outs],
        compiler_params=_cparams("arbitrary"),
        name="inproj",
    )(x2, *consts1, *tabs, *consts2)


def _mla_prompt_body(q_ref, k_ref, c_ref, wuvt_ref, o_ref, ct_scr, acc_scr, m_scr, l_scr, ot_scr, *, tq, nbp):
    i = pl.program_id(1)
    nblk = ct_scr.shape[1]

    @pl.when(i == 0)
    def _():
        for bb in range(nbp):
            for j in range(nblk):
                ct_scr[bb, j] = c_ref[bb, j * tq:(j + 1) * tq, :].T.astype(bf16)

    m_scr[...] = jnp.full(m_scr.shape, NEG_INF, f32)
    l_scr[...] = jnp.zeros(l_scr.shape, f32)
    acc_scr[...] = jnp.zeros(acc_scr.shape, f32)
    key = lax.broadcasted_iota(jnp.int32, (tq, tq), 0)
    qry = lax.broadcasted_iota(jnp.int32, (tq, tq), 1)
    causal = key <= qry

    def step(kb, masked):
        ks = pl.ds(pl.multiple_of(kb * tq, tq), tq)
        for h in range(MLA_HEADS):
            hs = slice(h * HEAD_PAD, (h + 1) * HEAD_PAD)
            for bb in range(nbp):
                s = _nt(k_ref[bb, ks, hs], q_ref[bb, :, hs])
                if masked:
                    s = jnp.where(causal, s, NEG_INF)
                m_old = m_scr[bb, h:h + 1, :]
                m_new = jnp.maximum(m_old, jnp.max(s, axis=0, keepdims=True))
                alpha = jnp.exp2(m_old - m_new)
                p = jnp.exp2(s - m_new)
                l_scr[bb, h:h + 1, :] = l_scr[bb, h:h + 1, :] * alpha + jnp.sum(p, axis=0, keepdims=True)
                acc_scr[bb, h] = acc_scr[bb, h] * alpha + _mm(ct_scr[bb, kb], p.astype(bf16))
                m_scr[bb, h:h + 1, :] = m_new

    def body(kb, carry):
        step(kb, False)
        return carry

    lax.fori_loop(0, i, body, 0)
    step(i, True)

    for bb in range(nbp):
        for h in range(MLA_HEADS):
            lat_t = (acc_scr[bb, h] / l_scr[bb, h:h + 1, :]).astype(bf16)
            ot_scr[h * V_DIM:(h + 1) * V_DIM, :] = _mm(wuvt_ref[h], lat_t)
        o_ref[bb] = ot_scr[...].T


def _mla_prompt(q, k, c, wuv_t, nb, seq, tq, nbp):
    nq = seq // tq
    return pl.pallas_call(
        functools.partial(_mla_prompt_body, tq=tq, nbp=nbp),
        grid=(nb // nbp, nq),
        in_specs=[pl.BlockSpec((nbp, tq, QK_PACKED), lambda b, i: (b, i, 0)),
                  pl.BlockSpec((nbp, seq, QK_PACKED), lambda b, i: (b, 0, 0)),
                  pl.BlockSpec((nbp, seq, KV_RANK), lambda b, i: (b, 0, 0)),
                  pl.BlockSpec(wuv_t.shape, lambda b, i: (0, 0, 0))],
        out_specs=pl.BlockSpec((nbp, tq, MLA_WIDTH), lambda b, i: (b, i, 0)),
        out_shape=jax.ShapeDtypeStruct((nb, seq, MLA_WIDTH), f32),
        scratch_shapes=[pltpu.VMEM((nbp, nq, KV_RANK, tq), bf16), pltpu.VMEM((nbp, MLA_HEADS, KV_RANK, tq), f32),
                        pltpu.VMEM((nbp, MLA_HEADS, tq), f32), pltpu.VMEM((nbp, MLA_HEADS, tq), f32),
                        pltpu.VMEM((MLA_WIDTH, tq), f32)],
        compiler_params=_cparams("arbitrary", "arbitrary"),
        name="mla_prompt",
    )(q, k, c, wuv_t)


def _mla_sample_body(pt_ref, q_ref, cnew_ref, kpenewt_ref, wukt_ref, wukg_ref, gkr_ref, cost_ref, sint_ref, wuv_ref,
                     ckv_hbm, kpe_hbm, o_ref,
                     cbuf, kbuf, sem, lhs_scr, qr_scr, cb_scr, s_scr, ql_all, qr_all, acc_scr, m_scr, l_scr,
                     *, layer, gp, n_pages, t_new):
    ndb = ql_all.shape[0]
    ng = (n_pages + 1) // gp
    kb = gp * PAGE_SIZE
    total = ndb * ng
    nrow = MLA_HEADS * t_new
    n_nope = MLA_HEADS * NOPE_DIM
    new_rows = pl.ds(kb - PAGE_SIZE, t_new)
    new_lanes = pl.ds(kb - PAGE_SIZE, PAGE_SIZE)

    def group_copies(t):
        b, g, slot = t // ng, t % ng, t % PAGE_BUFFERS
        cps = []
        for i in range(gp):
            page = pt_ref[b * n_pages + jnp.minimum(g * gp + i, n_pages - 1)]
            lanes = pl.ds(i * PAGE_SIZE, PAGE_SIZE)
            cps.append(pltpu.make_async_copy(ckv_hbm.at[layer, page], cbuf.at[slot, lanes, :], sem.at[0, slot]))
            cps.append(pltpu.make_async_copy(kpe_hbm.at[layer, page], kbuf.at[slot, :, lanes], sem.at[1, slot]))
        return cps

    def reset_state():
        m_scr[...] = jnp.full((nrow, 1), NEG_INF, f32)
        l_scr[...] = jnp.zeros((nrow, 1), f32)
        acc_scr[...] = jnp.zeros((nrow, KV_RANK), f32)

    gkr = gkr_ref[...]
    key_lane = lax.broadcasted_iota(jnp.int32, (nrow, PAGE_SIZE), 1)
    qry_row = lax.broadcasted_iota(jnp.int32, (nrow, PAGE_SIZE), 0) % t_new

    def prepare_group(t):
        b, g, pslot = t // ng, t % ng, t % PAGE_BUFFERS
        lhs_scr[n_nope:, :] = ql_all[b].astype(bf16)
        qr_scr[...] = qr_all[b].astype(bf16)
        for cp in group_copies(t):
            cp.wait()
        is_last = g == ng - 1
        cbuf[pslot, new_rows, :] = jnp.where(is_last, cnew_ref[b], cbuf[pslot, new_rows, :])
        kbuf[pslot, :, new_lanes] = jnp.where(is_last, kpenewt_ref[b], kbuf[pslot, :, new_lanes])

    def score_group(t, slot):
        g, pslot = t % ng, t % PAGE_BUFFERS
        is_last = g == ng - 1
        cb = cbuf[pslot].astype(bf16)
        cb_scr[slot] = cb
        kt = kbuf[pslot]
        cos_t, sin_t = cost_ref[g], sint_ref[g]
        big = _nt(lhs_scr[...], cb)
        kn = big[0:n_nope]
        ss_nope = jnp.sum((kn * kn).reshape(MLA_HEADS, NOPE_DIM, kb), axis=1)
        ss = ss_nope + jnp.sum(kt * kt, axis=0, keepdims=True)
        r = lax.rsqrt(ss * (1.0 / QK_DIM) + EPS)
        kg = kt * gkr
        k1, k2 = kg[0:HALF_ROPE], kg[HALF_ROPE:ROPE_DIM]
        kr = jnp.concatenate([k1 * cos_t - k2 * sin_t, k2 * cos_t + k1 * sin_t], axis=0).astype(bf16)
        s = big[n_nope:] + _mm(qr_scr[...], kr)
        s = (s.reshape(MLA_HEADS, t_new, kb) * r[:, None, :]).reshape(nrow, kb)
        first_dead = jnp.where(is_last, 1, PAGE_SIZE + t_new)
        tail = jnp.where(key_lane >= qry_row + first_dead, NEG_INF, s[:, kb - PAGE_SIZE:])
        s_scr[slot] = jnp.concatenate([s[:, :kb - PAGE_SIZE], tail], axis=1)

    def value_group(slot):
        s = s_scr[slot]
        m_old = m_scr[...]
        m_new = jnp.maximum(m_old, jnp.max(s, axis=-1, keepdims=True))
        alpha = jnp.exp2(m_old - m_new)
        p = jnp.exp2(s - m_new)
        l_scr[...] = l_scr[...] * alpha + jnp.sum(p, axis=-1, keepdims=True)
        acc_scr[...] = acc_scr[...] * alpha + _mm(p.astype(bf16), cb_scr[slot])
        m_scr[...] = m_new

    def finish_batch(b, closed):
        ql_all[b] = jnp.where(closed, acc_scr[...] / l_scr[...], ql_all[b])
        m_scr[...] = jnp.where(closed, NEG_INF, m_scr[...])
        l_scr[...] = jnp.where(closed, 0.0, l_scr[...])
        acc_scr[...] = jnp.where(closed, 0.0, acc_scr[...])

    lhs_scr[0:n_nope, :] = wukt_ref[...]
    for h in range(MLA_HEADS):
        q_h = q_ref[:, h * HEAD_PAD:(h + 1) * HEAD_PAD]
        rows = slice(h * t_new, (h + 1) * t_new)
        ql_all[:, rows, :] = _mm(q_h, wukg_ref[h]).reshape(ndb, t_new, KV_RANK)
        qr_all[:, rows, :] = q_h[:, KPE_LANE:KPE_LANE + ROPE_DIM].astype(f32).reshape(ndb, t_new, ROPE_DIM)
    reset_state()

    ahead = PAGE_BUFFERS - 1

    def skewed_step(t, slot, prefetch):
        if prefetch:
            for cp in group_copies(t + ahead):
                cp.start()
        prepare_group(t)
        score_group(t, slot)
        value_group(1 - slot)
        closes = t % ng == 0
        finish_batch(jnp.maximum(t // ng - 1, 0), closes)

    for t in range(min(ahead, total)):
        for cp in group_copies(t):
            cp.start()
    if total > ahead:
        for cp in group_copies(ahead):
            cp.start()
    prepare_group(0)
    score_group(0, 0)

    n_pairs = max(total - 1 - ahead, 0) // 2

    def body(k, carry):
        skewed_step(2 * k + 1, 1, True)
        skewed_step(2 * k + 2, 0, True)
        return carry

    lax.fori_loop(0, n_pairs, body, 0)
    for t in range(2 * n_pairs + 1, total):
        skewed_step(t, t % 2, t + ahead < total)
    value_group((total - 1) % 2)
    finish_batch(ndb - 1, True)

    for h in range(MLA_HEADS):
        lat_h = ql_all[:, h * t_new:(h + 1) * t_new, :].reshape(ndb * t_new, KV_RANK).astype(bf16)
        contrib = _mm(lat_h, wuv_ref[h])
        ps = slice((h // 2) * LANES, (h // 2 + 1) * LANES)
        if h % 2 == 0:
            o_ref[:, ps] = contrib
        else:
            o_ref[:, ps] += contrib


def _mla_sample(layer, q, c_new, kpe_new, cache_ckv, cache_kpe_t, page_table, w, tabs_t, gp):
    ndb, n_pages = page_table.shape
    t_new = q.shape[0] // ndb
    ng = (n_pages + 1) // gp
    kb = gp * PAGE_SIZE
    nrow = MLA_HEADS * t_new
    cost, sint = tabs_t
    by_group = lambda a: a.reshape(HALF_ROPE, ng, kb).transpose(1, 0, 2)
    kpe_new_t = jnp.pad(kpe_new.reshape(ndb, t_new, ROPE_DIM).transpose(0, 2, 1),
                        ((0, 0), (0, 0), (0, PAGE_SIZE - t_new)))
    args = (q, c_new.reshape(ndb, t_new, KV_RANK), kpe_new_t, w['w_uk_t'], w['w_uk_g'], w['g_k_rope'],
            by_group(cost), by_group(sint), w['w_uv_pairs'])

    def full(a):
        return pl.BlockSpec(a.shape, lambda i, pt, nd=a.ndim: (0,) * nd)

    return pl.pallas_call(
        functools.partial(_mla_sample_body, layer=layer, gp=gp, n_pages=n_pages, t_new=t_new),
        grid_spec=pltpu.PrefetchScalarGridSpec(
            num_scalar_prefetch=1,
            grid=(1,),
            in_specs=[full(a) for a in args] + [pl.BlockSpec(memory_space=pl.ANY)] * 2,
            out_specs=pl.BlockSpec((ndb * t_new, MLA_WIDTH), lambda i, pt: (0, 0)),
            scratch_shapes=[pltpu.VMEM((PAGE_BUFFERS, kb, KV_RANK), f32),
                            pltpu.VMEM((PAGE_BUFFERS, ROPE_DIM, kb), f32),
                            pltpu.SemaphoreType.DMA((2, PAGE_BUFFERS)),
                            pltpu.VMEM((MLA_HEADS * NOPE_DIM + nrow, KV_RANK), bf16),
                            pltpu.VMEM((nrow, ROPE_DIM), bf16),
                            pltpu.VMEM((2, kb, KV_RANK), bf16),
                            pltpu.VMEM((2, nrow, kb), f32),
                            pltpu.VMEM((ndb, nrow, KV_RANK), f32),
                            pltpu.VMEM((ndb, nrow, ROPE_DIM), f32),
                            pltpu.VMEM((nrow, KV_RANK), f32),
                            pltpu.VMEM((nrow, 1), f32),
                            pltpu.VMEM((nrow, 1), f32)]),
        out_shape=jax.ShapeDtypeStruct((ndb * t_new, MLA_WIDTH), f32),
        compiler_params=_cparams("arbitrary"),
        name="mla_sample",
    )(page_table.reshape(-1), *args, cache_ckv, cache_kpe_t)


def _split3(x):
    hi = x.astype(bf16)
    r1 = x - hi.astype(f32)
    mid = r1.astype(bf16)
    lo = (r1 - mid.astype(f32)).astype(bf16)
    return hi, mid, lo


def _gla_body(*refs, nb, tt, cs, has_init):
    if has_init:
        q_ref, k_ref, v_ref, g_ref, s0_ref, o_ref, sT_out, st_scr = refs
    else:
        q_ref, k_ref, v_ref, g_ref, o_ref, sT_out, st_scr = refs
    ti = pl.program_id(1)
    nchunk = tt // cs

    @pl.when(ti == 0)
    def _():
        if has_init:
            st_scr[...] = s0_ref[...]
        else:
            st_scr[...] = jnp.zeros(st_scr.shape, f32)

    tri = (lax.broadcasted_iota(jnp.int32, (tt, tt), 1) <= lax.broadcasted_iota(jnp.int32, (tt, tt), 0)).astype(bf16)
    kv_head = (lax.broadcasted_iota(jnp.int32, (GLA_KDIM, GLA_WIDTH), 0) // GLA_DK
               == lax.broadcasted_iota(jnp.int32, (GLA_KDIM, GLA_WIDTH), 1) // GLA_DV)
    block_ones = kv_head.astype(bf16)
    vk_head = (lax.broadcasted_iota(jnp.int32, (GLA_WIDTH, GLA_KDIM), 0) // GLA_DV
               == lax.broadcasted_iota(jnp.int32, (GLA_WIDTH, GLA_KDIM), 1) // GLA_DK)
    sel_t = (lax.broadcasted_iota(jnp.int32, (cs, cs * cs), 1) // cs
             == lax.broadcasted_iota(jnp.int32, (cs, cs * cs), 0)).astype(bf16)
    t_idx = lax.broadcasted_iota(jnp.int32, (cs, cs, GLA_KDIM), 0)
    s_idx = lax.broadcasted_iota(jnp.int32, (cs, cs, GLA_KDIM), 1)
    causal3 = s_idx <= t_idx

    cums = []
    for b in range(nb):
        hi, mid, lo = _split3(g_ref[b])
        cums.append(_mm(tri, hi) + _mm(tri, mid) + _mm(tri, lo))
    for n in range(nchunk):
        sl = slice(n * cs, (n + 1) * cs)
        for b in range(nb):
            cum = cums[b]
            base = cum[n * cs - 1:n * cs] if n > 0 else jnp.zeros((1, GLA_KDIM), f32)
            bc = cum[sl] - base
            b_last = bc[cs - 1:cs]
            qc, kc, vc = q_ref[b, sl, :], k_ref[b, sl, :], v_ref[b, sl, :]
            st = st_scr[b]
            o_inter = _nt((qc * jnp.exp2(bc)).astype(bf16), st.astype(bf16))
            diff = jnp.where(causal3, bc[:, None, :] - bc[None, :, :], NEG_INF)
            d3 = qc[:, None, :] * kc[None, :, :] * jnp.exp2(diff)
            a_exp = _mm(d3.reshape(cs * cs, GLA_KDIM).astype(bf16), block_ones)
            xv = (a_exp.reshape(cs, cs, GLA_WIDTH) * vc[None, :, :]).reshape(cs * cs, GLA_WIDTH)
            o_intra = _mm(sel_t, xv.astype(bf16))
            o_ref[b, sl, :] = o_inter + o_intra
            kd = (kc * jnp.exp2(b_last - bc)).astype(bf16)
            upd = _tn(vc.astype(bf16), kd)
            st_scr[b] = st * jnp.exp2(b_last) + jnp.where(vk_head, upd, 0.0)

    @pl.when(ti == pl.num_programs(1) - 1)
    def _():
        sT_out[...] = st_scr[...]


def _gla(q, k, v, g, s0t, nb, tt, cs):
    B, L, _ = q.shape
    has_init = s0t is not None
    tok = lambda bi, ti: (bi, ti, 0)
    st = lambda bi, ti: (bi, 0, 0)
    in_specs = [pl.BlockSpec((nb, tt, GLA_KDIM), tok), pl.BlockSpec((nb, tt, GLA_KDIM), tok),
                pl.BlockSpec((nb, tt, GLA_WIDTH), tok), pl.BlockSpec((nb, tt, GLA_KDIM), tok)]
    args = [q, k, v, g]
    if has_init:
        in_specs.append(pl.BlockSpec((nb, GLA_WIDTH, GLA_KDIM), st))
        args.append(s0t)
    return pl.pallas_call(
        functools.partial(_gla_body, nb=nb, tt=tt, cs=cs, has_init=has_init),
        grid=(B // nb, L // tt),
        in_specs=in_specs,
        out_specs=[pl.BlockSpec((nb, tt, GLA_WIDTH), tok), pl.BlockSpec((nb, GLA_WIDTH, GLA_KDIM), st)],
        out_shape=[jax.ShapeDtypeStruct((B, L, GLA_WIDTH), f32), jax.ShapeDtypeStruct((B, GLA_WIDTH, GLA_KDIM), f32)],
        scratch_shapes=[pltpu.VMEM((nb, GLA_WIDTH, GLA_KDIM), f32)],
        compiler_params=_cparams("arbitrary", "arbitrary"),
        name="gla",
    )(*args)


def _memkv_body(mem_ref, gmem_ref, w_ref, gk_ref, mkt_out, mvt_out):
    xb = _rms_rows(mem_ref[...], gmem_ref[...]).astype(bf16)
    kv = _mm(xb, w_ref[...])
    lane = lax.broadcasted_iota(jnp.int32, (1, LANES), 1)
    gk = gk_ref[...]
    for j in range(MEM_WIDTH // LANES):
        blk = _half_head_rms(kv[:, j * LANES:(j + 1) * LANES], gk, lane)
        mkt_out[j * LANES:(j + 1) * LANES, :] = blk.T
        mvt_out[j * LANES:(j + 1) * LANES, :] = kv[:, MEM_WIDTH + j * LANES:MEM_WIDTH + (j + 1) * LANES].T


def _memkv(mem2, w, nb, n_mem):
    row = lambda i: (i, 0)
    fix = lambda i: (0, 0)
    return pl.pallas_call(
        _memkv_body,
        grid=(nb,),
        in_specs=[pl.BlockSpec((n_mem, D_MODEL), row), pl.BlockSpec(w['g_mem'].shape, fix),
                  pl.BlockSpec(w['w_mem_kv'].shape, fix), pl.BlockSpec(w['g_mem_k'].shape, fix)],
        out_specs=[pl.BlockSpec((None, MEM_WIDTH, n_mem), lambda i: (i, 0, 0))] * 2,
        out_shape=[jax.ShapeDtypeStruct((nb, MEM_WIDTH, n_mem), f32)] * 2,
        compiler_params=_cparams("arbitrary"),
        name="memkv",
    )(mem2, w['g_mem'], w['w_mem_kv'], w['g_mem_k'])


def _memattn_body(q_ref, mkt_ref, mvt_ref, o_ref, *, nb, tq):
    lane_head = lax.broadcasted_iota(jnp.int32, (1, MEM_WIDTH), 1) // MEM_HEAD_DIM
    for b in range(nb):
        q = q_ref[b]
        mkt = mkt_ref[b].astype(bf16)
        mvt = mvt_ref[b].astype(bf16)
        qs = jnp.concatenate([jnp.where(lane_head == h, q, jnp.zeros_like(q)) for h in range(MEM_HEADS)], axis=0)
        s = _mm(qs, mkt)
        p = jnp.exp2(s - jnp.max(s, axis=-1, keepdims=True))
        pv = _nt(p.astype(bf16), mvt) / jnp.sum(p, axis=-1, keepdims=True)
        o = jnp.zeros((tq, MEM_WIDTH), f32)
        for h in range(MEM_HEADS):
            o = o + jnp.where(lane_head == h, pv[h * tq:(h + 1) * tq], 0.0)
        o_ref[b] = o


def _memattn(layer, q3, mkt4, mvt4, nb, tq):
    B, L, _ = q3.shape
    n_mem = mkt4.shape[-1]
    kv_spec = pl.BlockSpec((None, nb, MEM_WIDTH, n_mem), lambda b, i: (layer, b, 0, 0))
    return pl.pallas_call(
        functools.partial(_memattn_body, nb=nb, tq=tq),
        grid=(B // nb, L // tq),
        in_specs=[pl.BlockSpec((nb, tq, MEM_WIDTH), lambda b, i: (b, i, 0)), kv_spec, kv_spec],
        out_specs=pl.BlockSpec((nb, tq, MEM_WIDTH), lambda b, i: (b, i, 0)),
        out_shape=jax.ShapeDtypeStruct((B, L, MEM_WIDTH), f32),
        compiler_params=_cparams("arbitrary", "arbitrary"),
        name="memattn",
    )(q3, mkt4, mvt4)


def _merge_body(x_ref, mla_ref, gla_ref, mem_ref, gates_ref, ggla_ref, wout_ref, y_ref):
    lane = lax.broadcasted_iota(jnp.int32, (1, LANES), 1)
    ggla = ggla_ref[...]
    y = x_ref[...]
    m1 = (gates_ref[:, 0:MLA_WIDTH] * mla_ref[...]).astype(bf16)
    y = y + _mm(m1, wout_ref[0:MLA_WIDTH, :])
    for j in range(GLA_WIDTH // LANES):
        ls = slice(j * LANES, (j + 1) * LANES)
        gn = _half_head_rms(gla_ref[:, ls], ggla, lane)
        gs = slice(MLA_WIDTH + j * LANES, MLA_WIDTH + (j + 1) * LANES)
        y = y + _mm((gates_ref[:, gs] * gn).astype(bf16), wout_ref[gs, :])
    ms = slice(MLA_WIDTH + GLA_WIDTH, D_MIX)
    m3 = (gates_ref[:, ms] * mem_ref[...]).astype(bf16)
    y_ref[...] = y + _mm(m3, wout_ref[ms, :])


def _merge(x2, mla_o, gla_o, mem_o, gates, w, tm):
    T = x2.shape[0]
    row = lambda i: (i, 0)
    fix = lambda i: (0, 0)
    return pl.pallas_call(
        _merge_body,
        grid=(T // tm,),
        in_specs=[pl.BlockSpec((tm, D_MODEL), row), pl.BlockSpec((tm, MLA_WIDTH), row),
                  pl.BlockSpec((tm, GLA_WIDTH), row), pl.BlockSpec((tm, MEM_WIDTH), row),
                  pl.BlockSpec((tm, D_MIX), row), pl.BlockSpec(w['g_gla_o'].shape, fix),
                  pl.BlockSpec(w['w_out'].shape, fix)],
        out_specs=pl.BlockSpec((tm, D_MODEL), row),
        out_shape=jax.ShapeDtypeStruct((T, D_MODEL), f32),
        compiler_params=_cparams("arbitrary"),
        name="merge",
    )(x2, mla_o, gla_o, mem_o, gates, w['g_gla_o'], w['w_out'])


def _pad_head(a):
    return jnp.pad(a, [(0, 0)] * (a.ndim - 1) + [(0, HEAD_PAD - QK_DIM)])


def _prep_layer(l, p):
    w_in = p['w_in'][l]
    cuts, o = [], 0
    for n in (Q_RANK, KV_RANK, ROPE_DIM, MLA_WIDTH, GLA_KDIM, GLA_KDIM, GLA_WIDTH, GLA_GATE_RANK, GLA_WIDTH,
              MEM_WIDTH, MEM_WIDTH):
        cuts.append(w_in[:, o:o + n])
        o += n
    cq, ckv, kpe, gate_mla, gq, gk, gv, gg, gate_gla, mq, gate_mem = cuts
    z = lambda n: jnp.zeros((D_MODEL, n), f32)
    misc = jnp.concatenate([gg, z(KPE_LANE - GLA_GATE_RANK), kpe, z(LANES - KPE_LANE - ROPE_DIM)], axis=1)
    w_in_packed = jnp.concatenate([cq, ckv, gate_mla, gq, gk, gv, gate_gla, mq, gate_mem, misc], axis=1)

    w_uk = p['w_uk'][l]
    g_k = p['g_mla_k'][l]
    wuk_packed = jnp.pad(w_uk, ((0, 0), (0, 0), (0, HEAD_PAD - NOPE_DIM))).reshape(KV_RANK, QK_PACKED)
    w_uk_t = w_uk.reshape(KV_RANK, MLA_HEADS * NOPE_DIM).T
    w_uk_g = jnp.pad((w_uk * g_k[None, None, :NOPE_DIM]).transpose(1, 2, 0),
                     ((0, 0), (0, HEAD_PAD - NOPE_DIM), (0, 0)))
    w_uv = p['w_uv'][l].transpose(1, 0, 2)
    zeros = jnp.zeros_like(w_uv)
    even = jnp.arange(MLA_HEADS)[:, None, None] % 2 == 0
    w_uv_pairs = jnp.concatenate([jnp.where(even, w_uv, zeros), jnp.where(even, zeros, w_uv)], axis=-1)
    w_gk = jnp.pad(p['w_gk'][l], ((0, LANES - GLA_GATE_RANK), (0, 0)))
    row = lambda a: a.reshape(1, -1)
    return {
        'g_pre': row(p['g_pre'][l]), 'w_in': w_in_packed.astype(bf16), 'g_qa': row(p['g_qa'][l]),
        'w_uq': _pad_head(p['w_uq'][l]).reshape(Q_RANK, QK_PACKED).astype(bf16),
        'g_kva': row(p['g_kva'][l]), 'w_uk': wuk_packed.astype(bf16),
        'g_q': row(_pad_head(p['g_mla_q'][l])), 'g_k': row(_pad_head(g_k)),
        'w_gk': w_gk.astype(bf16), 'b_gk': row(p['b_gk'][l]),
        'g_mem_q': row(jnp.tile(p['g_mem_q'][l], 2)), 'g_mem_k': row(jnp.tile(p['g_mem_k'][l], 2)),
        'g_gla_o': row(jnp.tile(p['g_gla_o'][l], 2)),
        'w_uk_t': w_uk_t.astype(bf16), 'w_uk_g': w_uk_g.astype(bf16),
        'g_k_rope': g_k[NOPE_DIM:].reshape(ROPE_DIM, 1),
        'w_uv_pairs': w_uv_pairs.astype(bf16),
        'w_uv_t': w_uv.transpose(0, 2, 1).astype(bf16),
        'g_mem': row(p['g_mem'][l]), 'w_mem_kv': p['w_mem_kv'][l].astype(bf16),
        'w_out': p['w_out'][l].astype(bf16),
    }


def _angles(pos):
    inv_freq = ROPE_THETA ** (-jnp.arange(HALF_ROPE, dtype=f32) * (2.0 / ROPE_DIM))
    return pos.astype(f32)[:, None] * inv_freq[None, :]


def _packed_tables(pos):
    ang = _angles(pos)
    n = pos.shape[0]
    cos, sin = jnp.cos(ang), jnp.sin(ang)
    one = jnp.ones((n, NOPE_DIM), f32)
    z = lambda w: jnp.zeros((n, w), f32)
    tail = HEAD_PAD - QK_DIM
    cos_p = jnp.concatenate([one, cos, cos, z(tail)], axis=1)
    sin_up = jnp.concatenate([z(NOPE_DIM + HALF_ROPE), sin, z(tail)], axis=1)
    sin_dn = jnp.concatenate([z(NOPE_DIM), -sin, z(HALF_ROPE + tail)], axis=1)
    return cos_p, sin_up, sin_dn


def _transposed_tables(pos):
    ang = _angles(pos).T
    return jnp.cos(ang), jnp.sin(ang)


def _state_to_t(s):
    B = s.shape[0]
    st = s.transpose(0, 1, 3, 2)
    eye = jnp.eye(GLA_HEADS, dtype=s.dtype)
    return jnp.einsum('bhvk,hg->bhvgk', st, eye).reshape(B, GLA_WIDTH, GLA_KDIM)


def _t_to_state(st):
    B = st.shape[0]
    s5 = st.reshape(B, GLA_HEADS, GLA_DV, GLA_HEADS, GLA_DK)
    diag = jnp.stack([s5[:, h, :, h, :] for h in range(GLA_HEADS)], axis=1)
    return diag.transpose(0, 1, 3, 2)


def kernel(x_prompt, x_sample, mem_prompt, cache_ckv, cache_kpe, page_table, state_gla, cache_mem_k, cache_mem_v, g_pre, w_in, g_qa, w_uq, g_kva, w_uk, w_uv, g_mla_q, g_mla_k, w_gk, b_gk, g_gla_o, g_mem, w_mem_kv, g_mem_q, g_mem_k, w_out):
    params = dict(g_pre=g_pre, w_in=w_in, g_qa=g_qa, w_uq=w_uq, g_kva=g_kva, w_uk=w_uk, w_uv=w_uv,
                  g_mla_q=g_mla_q, g_mla_k=g_mla_k, w_gk=w_gk, b_gk=b_gk, g_gla_o=g_gla_o, g_mem=g_mem,
                  w_mem_kv=w_mem_kv, g_mem_q=g_mem_q, g_mem_k=g_mem_k, w_out=w_out)
    nb, seq, _ = x_prompt.shape
    ndb, t_new, _ = x_sample.shape
    n_mem = mem_prompt.shape[1]
    depth = w_in.shape[0]
    n_pages = page_table.shape[1]
    past_len = n_pages * cache_ckv.shape[2]

    tm_p = min(256, seq)
    tm_merge = min(512, seq)
    tq_mem = min(512, seq)
    tq = min(256, seq)
    tm_s = min(256, ndb * t_new)
    pp = max(d for d in range(1, 17) if (n_pages + 1) % d == 0)
    gla_tt = min(128, seq)
    gla_cs = min(16, gla_tt)
    nb_s = min(8, ndb)
    nbp = min(2, nb)
    nb_gla = min(8, nb)

    cache_kpe_t = cache_kpe.transpose(0, 1, 3, 2)
    to_t = lambda a: a.transpose(0, 1, 3, 4, 2).reshape(depth, ndb, MEM_WIDTH, n_mem)
    cache_mkt, cache_mvt = to_t(cache_mem_k), to_t(cache_mem_v)
    from_t = lambda a: a.reshape(nb, MEM_HEADS, MEM_HEAD_DIM, n_mem).transpose(0, 3, 1, 2)

    tabs_p = _packed_tables(jnp.arange(seq, dtype=jnp.int32))
    pos_s = past_len + jnp.arange(t_new, dtype=jnp.int32)
    tabs_s = tuple(jnp.tile(t, (tm_s // t_new, 1)) for t in _packed_tables(pos_s))
    tabs_t = _transposed_tables(jnp.arange(past_len + PAGE_SIZE, dtype=jnp.int32))

    xp = x_prompt.reshape(nb * seq, D_MODEL)
    xs = x_sample.reshape(ndb * t_new, D_MODEL)
    mem2 = mem_prompt.reshape(nb * n_mem, D_MODEL)
    ckv_p, kpe_p, gla_p, mk_p, mv_p, ckv_s, kpe_s, gla_s = ([] for _ in range(8))
    for l in range(depth):
        w = _prep_layer(l, params)
        q, k, c, kpe, gates, gq, gk, gv, gg, mq = _inproj(xp, tabs_p, w, tm_p)
        r3 = lambda a: a.reshape(nb, seq, a.shape[-1])
        mla_o = _mla_prompt(r3(q), r3(k), r3(c), w['w_uv_t'], nb, seq, tq, nbp).reshape(nb * seq, MLA_WIDTH)
        gla_o, st = _gla(r3(gq), r3(gk), r3(gv), r3(gg), None, nb_gla, gla_tt, gla_cs)
        mkt, mvt = _memkv(mem2, w, nb, n_mem)
        mem_o = _memattn(0, r3(mq), mkt[None], mvt[None], 1, tq_mem)
        xp = _merge(xp, mla_o, gla_o.reshape(nb * seq, GLA_WIDTH), mem_o.reshape(nb * seq, MEM_WIDTH), gates, w, tm_merge)
        ckv_p.append(c.reshape(nb, seq, KV_RANK))
        kpe_p.append(kpe.reshape(nb, seq, ROPE_DIM))
        gla_p.append(_t_to_state(st))
        mk_p.append(from_t(mkt))
        mv_p.append(from_t(mvt))
        q, k, c, kpe, gates, gq, gk, gv, gg, mq = _inproj(xs, tabs_s, w, tm_s)
        mla_o = _mla_sample(l, q, c, kpe, cache_ckv, cache_kpe_t, page_table, w, tabs_t, pp)
        r3 = lambda a: a.reshape(ndb, t_new, a.shape[-1])
        gla_o, st = _gla(r3(gq), r3(gk), r3(gv), r3(gg), _state_to_t(state_gla[l]), nb_s, t_new, t_new)
        mem_o = _memattn(l, r3(mq), cache_mkt, cache_mvt, nb_s, t_new)
        xs = _merge(xs, mla_o, gla_o.reshape(ndb * t_new, GLA_WIDTH), mem_o.reshape(ndb * t_new, MEM_WIDTH), gates, w, tm_s)
        ckv_s.append(c.reshape(ndb, t_new, KV_RANK))
        kpe_s.append(kpe.reshape(ndb, t_new, ROPE_DIM))
        gla_s.append(_t_to_state(st))
    return (xp.reshape(nb, seq, D_MODEL), xs.reshape(ndb, t_new, D_MODEL), jnp.stack(ckv_p), jnp.stack(kpe_p),
            jnp.stack(gla_p), jnp.stack(mk_p), jnp.stack(mv_p), jnp.stack(ckv_s), jnp.stack(kpe_s), jnp.stack(gla_s))
```

```python
import functools

import jax
import jax.numpy as jnp
from jax import lax
from jax.experimental import pallas as pl
from jax.experimental.pallas import tpu as pltpu

f32, bf16 = jnp.float32, jnp.bfloat16

D_MODEL = 1024
PAGE_SIZE = 128
MLA_HEADS = 8
NOPE_DIM = 64
ROPE_DIM = 32
HALF_ROPE = ROPE_DIM // 2
QK_DIM = NOPE_DIM + ROPE_DIM
V_DIM = 64
Q_RANK = 384
KV_RANK = 256
MLA_WIDTH = MLA_HEADS * V_DIM
ROPE_THETA = 10000.0
MLA_SCALE = QK_DIM ** -0.5
LOG2_E = 1.4426950408889634
GLA_HEADS = 4
GLA_DK = 32
GLA_DV = 64
GLA_KDIM = GLA_HEADS * GLA_DK
GLA_WIDTH = GLA_HEADS * GLA_DV
GLA_GATE_RANK = 16
GLA_NORMALIZER = 16.0
MEM_HEADS = 4
MEM_HEAD_DIM = 64
MEM_WIDTH = MEM_HEADS * MEM_HEAD_DIM
MEM_SCALE = MEM_HEAD_DIM ** -0.5
D_MIX = MLA_WIDTH + GLA_WIDTH + MEM_WIDTH
EPS = 1e-6
NEG_INF = -1e30

LANES = 128
HEAD_PAD = LANES
QK_PACKED = MLA_HEADS * HEAD_PAD

OFF_CQ = 0
OFF_CKV = OFF_CQ + Q_RANK
OFF_GATE_MLA = OFF_CKV + KV_RANK
OFF_GQ = OFF_GATE_MLA + MLA_WIDTH
OFF_GK = OFF_GQ + GLA_KDIM
OFF_GV = OFF_GK + GLA_KDIM
OFF_GATE_GLA = OFF_GV + GLA_WIDTH
OFF_MQ = OFF_GATE_GLA + GLA_WIDTH
OFF_GATE_MEM = OFF_MQ + MEM_WIDTH
OFF_MISC = OFF_GATE_MEM + MEM_WIDTH
D_IN_PACKED = OFF_MISC + LANES
KPE_LANE = NOPE_DIM

VMEM_LIMIT = 56 * 1024 * 1024
PAGE_BUFFERS = 3


def _cparams(*sem):
    return pltpu.CompilerParams(dimension_semantics=sem, vmem_limit_bytes=VMEM_LIMIT)


def _nt(a, b):
    return lax.dot_general(a, b, (((1,), (1,)), ((), ())), preferred_element_type=f32)


def _tn(a, b):
    return lax.dot_general(a, b, (((0,), (0,)), ((), ())), preferred_element_type=f32)


def _mm(a, b):
    return jnp.dot(a, b, preferred_element_type=f32)


def _rms_rows(x, g):
    return x * lax.rsqrt(jnp.mean(x * x, axis=-1, keepdims=True) + EPS) * g


def _half_head_rms(blk, g, lane):
    sq = blk * blk
    lo = lane < 64
    ss_lo = jnp.sum(jnp.where(lo, sq, 0.0), axis=-1, keepdims=True)
    ss_hi = jnp.sum(jnp.where(lo, 0.0, sq), axis=-1, keepdims=True)
    ss = jnp.where(lo, ss_lo, ss_hi)
    return blk * lax.rsqrt(ss * (1.0 / 64.0) + EPS) * g


def _rope_packed(x, cos, sin_up, sin_dn):
    return x * cos + pltpu.roll(x, HALF_ROPE, 1) * sin_up + pltpu.roll(x, LANES - HALF_ROPE, 1) * sin_dn


def _inproj_body(x_ref, gpre_ref, win_ref, gqa_ref, wuq_ref, gkva_ref, wuk_ref, gq_ref, gk_ref,
                 cos_ref, sup_ref, sdn_ref, wgk_ref, bgk_ref, gmq_ref,
                 q_out, k_out, ckv_out, kpe_out, gates_out, glaq_out, glak_out, glav_out, glag_out, memq_out):
    x = x_ref[...]
    xb = _rms_rows(x, gpre_ref[...]).astype(bf16)

    def seg(off, n):
        return _mm(xb, win_ref[:, off:off + n])

    cos, sup, sdn = cos_ref[...], sup_ref[...], sdn_ref[...]
    lane = lax.broadcasted_iota(jnp.int32, (1, LANES), 1)

    cq = _rms_rows(seg(OFF_CQ, Q_RANK), gqa_ref[...]).astype(bf16)
    qf = _mm(cq, wuq_ref[...])
    gq = gq_ref[...]
    for h in range(MLA_HEADS):
        qh = qf[:, h * HEAD_PAD:(h + 1) * HEAD_PAD]
        ss = jnp.sum(qh * qh, axis=-1, keepdims=True)
        qn = qh * lax.rsqrt(ss * (1.0 / QK_DIM) + EPS) * gq
        q_out[:, h * HEAD_PAD:(h + 1) * HEAD_PAD] = (_rope_packed(qn, cos, sup, sdn) * (MLA_SCALE * LOG2_E)).astype(bf16)

    c = _rms_rows(seg(OFF_CKV, KV_RANK), gkva_ref[...])
    ckv_out[...] = c
    misc = seg(OFF_MISC, LANES)
    kpe_wide = jnp.where((lane >= KPE_LANE) & (lane < KPE_LANE + ROPE_DIM), misc, 0.0)
    kpe_out[...] = misc[:, KPE_LANE:KPE_LANE + ROPE_DIM]
    kf = _mm(c.astype(bf16), wuk_ref[...])
    gk = gk_ref[...]
    for h in range(MLA_HEADS):
        kh = kf[:, h * HEAD_PAD:(h + 1) * HEAD_PAD] + kpe_wide
        ss = jnp.sum(kh * kh, axis=-1, keepdims=True)
        kn = kh * lax.rsqrt(ss * (1.0 / QK_DIM) + EPS) * gk
        k_out[:, h * HEAD_PAD:(h + 1) * HEAD_PAD] = _rope_packed(kn, cos, sup, sdn).astype(bf16)

    g1 = seg(OFF_GATE_MLA, MLA_WIDTH)
    gates_out[:, 0:MLA_WIDTH] = g1 * jax.nn.sigmoid(g1)
    g2 = seg(OFF_GATE_GLA, GLA_WIDTH)
    gates_out[:, MLA_WIDTH:MLA_WIDTH + GLA_WIDTH] = g2 * jax.nn.sigmoid(g2)
    g3 = seg(OFF_GATE_MEM, MEM_WIDTH)
    gates_out[:, MLA_WIDTH + GLA_WIDTH:D_MIX] = g3 * jax.nn.sigmoid(g3)

    glaq_out[...] = seg(OFF_GQ, GLA_KDIM) * (GLA_DK ** -0.5)
    glak_out[...] = seg(OFF_GK, GLA_KDIM)
    glav_out[...] = seg(OFF_GV, GLA_WIDTH)
    gl = _mm(misc.astype(bf16), wgk_ref[...]) + bgk_ref[...]
    glag_out[...] = jax.nn.log_sigmoid(gl) * (LOG2_E / GLA_NORMALIZER)

    mq = seg(OFF_MQ, MEM_WIDTH)
    gmq = gmq_ref[...]
    for j in range(MEM_WIDTH // LANES):
        blk = _half_head_rms(mq[:, j * LANES:(j + 1) * LANES], gmq, lane)
        memq_out[:, j * LANES:(j + 1) * LANES] = (blk * (MEM_SCALE * LOG2_E)).astype(bf16)


def _inproj(x2, tabs, w, tm):
    T = x2.shape[0]
    nt = tabs[0].shape[0] // tm
    row = lambda i: (i, 0)
    fix = lambda i: (0, 0)
    tab = lambda i: (i % nt, 0)

    def full(a):
        return pl.BlockSpec(a.shape, fix)

    consts1 = (w['g_pre'], w['w_in'], w['g_qa'], w['w_uq'], w['g_kva'], w['w_uk'], w['g_q'], w['g_k'])
    consts2 = (w['w_gk'], w['b_gk'], w['g_mem_q'])
    outs = [(QK_PACKED, bf16), (QK_PACKED, bf16), (KV_RANK, f32), (ROPE_DIM, f32), (D_MIX, f32),
            (GLA_KDIM, f32), (GLA_KDIM, f32), (GLA_WIDTH, f32), (GLA_KDIM, f32), (MEM_WIDTH, bf16)]
    return pl.pallas_call(
        _inproj_body,
        grid=(T // tm,),
        in_specs=[pl.BlockSpec((tm, D_MODEL), row)] + [full(a) for a in consts1]
        + [pl.BlockSpec((tm, LANES), tab)] * 3 + [full(a) for a in consts2],
        out_specs=[pl.BlockSpec((tm, n), row) for n, _ in outs],
        out_shape=[jax.ShapeDtypeStruct((T, n), dt) for n, dt in outs],
        compiler_params=_cparams("arbitrary"),
        name="inproj",
    )(x2, *consts1, *tabs, *consts2)


def _mla_prompt_body(q_ref, k_ref, c_ref, wuvt_ref, o_ref, ct_scr, acc_scr, m_scr, l_scr, ot_scr, *, tq, nbp):
    i = pl.program_id(1)
    nblk = ct_scr.shape[1]

    @pl.when(i == 0)
    def _():
        for bb in range(nbp):
            for j in range(nblk):
                ct_scr[bb, j] = c_ref[bb, j * tq:(j + 1) * tq, :].T.astype(bf16)

    m_scr[...] = jnp.full(m_scr.shape, NEG_INF, f32)
    l_scr[...] = jnp.zeros(l_scr.shape, f32)
    acc_scr[...] = jnp.zeros(acc_scr.shape, f32)
    key = lax.broadcasted_iota(jnp.int32, (tq, tq), 0)
    qry = lax.broadcasted_iota(jnp.int32, (tq, tq), 1)
    causal = key <= qry

    def step(kb, masked):
        ks = pl.ds(pl.multiple_of(kb * tq, tq), tq)
        for h in range(MLA_HEADS):
            hs = slice(h * HEAD_PAD, (h + 1) * HEAD_PAD)
            for bb in range(nbp):
                s = _nt(k_ref[bb, ks, hs], q_ref[bb, :, hs])
                if masked:
                    s = jnp.where(causal, s, NEG_INF)
                m_old = m_scr[bb, h:h + 1, :]
                m_new = jnp.maximum(m_old, jnp.max(s, axis=0, keepdims=True))
                alpha = jnp.exp2(m_old - m_new)
                p = jnp.exp2(s - m_new)
                l_scr[bb, h:h + 1, :] = l_scr[bb, h:h + 1, :] * alpha + jnp.sum(p, axis=0, keepdims=True)
                acc_scr[bb, h] = acc_scr[bb, h] * alpha + _mm(ct_scr[bb, kb], p.astype(bf16))
                m_scr[bb, h:h + 1, :] = m_new

    def body(kb, carry):
        step(kb, False)
        return carry

    lax.fori_loop(0, i, body, 0)
    step(i, True)

    for bb in range(nbp):
        for h in range(MLA_HEADS):
            lat_t = (acc_scr[bb, h] * (1.0 / l_scr[bb, h:h + 1, :])).astype(bf16)
            ot_scr[h * V_DIM:(h + 1) * V_DIM, :] = _mm(wuvt_ref[h], lat_t)
        o_ref[bb] = ot_scr[...].T


def _mla_prompt(q, k, c, wuv_t, nb, seq, tq, nbp):
    nq = seq // tq
    return pl.pallas_call(
        functools.partial(_mla_prompt_body, tq=tq, nbp=nbp),
        grid=(nb // nbp, nq),
        in_specs=[pl.BlockSpec((nbp, tq, QK_PACKED), lambda b, i: (b, i, 0)),
                  pl.BlockSpec((nbp, seq, QK_PACKED), lambda b, i: (b, 0, 0), pipeline_mode=pl.Buffered(1)),
                  pl.BlockSpec((nbp, seq, KV_RANK), lambda b, i: (b, 0, 0), pipeline_mode=pl.Buffered(1)),
                  pl.BlockSpec(wuv_t.shape, lambda b, i: (0, 0, 0))],
        out_specs=pl.BlockSpec((nbp, tq, MLA_WIDTH), lambda b, i: (b, i, 0)),
        out_shape=jax.ShapeDtypeStruct((nb, seq, MLA_WIDTH), f32),
        scratch_shapes=[pltpu.VMEM((nbp, nq, KV_RANK, tq), bf16), pltpu.VMEM((nbp, MLA_HEADS, KV_RANK, tq), f32),
                        pltpu.VMEM((nbp, MLA_HEADS, tq), f32), pltpu.VMEM((nbp, MLA_HEADS, tq), f32),
                        pltpu.VMEM((MLA_WIDTH, tq), f32)],
        compiler_params=_cparams("arbitrary", "arbitrary"),
        name="mla_prompt",
    )(q, k, c, wuv_t)


def _mla_sample_body(pt_ref, q_ref, cnew_ref, kpenewt_ref, wukt_ref, wukg_ref, gkr_ref, cost_ref, sint_ref, wuv_ref,
                     ckv_hbm, kpe_hbm, o_ref,
                     cbuf, kbuf, sem, lhs_scr, qr_scr, cb_scr, s_scr, ql_all, qr_all, acc_scr, m_scr, l_scr,
                     *, layer, gp, n_pages, t_new):
    ndb = ql_all.shape[0]
    ng = (n_pages + 1) // gp
    kb = gp * PAGE_SIZE
    total = ndb * ng
    nrow = MLA_HEADS * t_new
    n_nope = MLA_HEADS * NOPE_DIM
    new_rows = pl.ds(kb - PAGE_SIZE, t_new)
    new_lanes = pl.ds(kb - PAGE_SIZE, PAGE_SIZE)

    def group_copies(t):
        b, g, slot = t // ng, t % ng, t % PAGE_BUFFERS
        cps = []
        for i in range(gp):
            page = pt_ref[b * n_pages + jnp.minimum(g * gp + i, n_pages - 1)]
            lanes = pl.ds(i * PAGE_SIZE, PAGE_SIZE)
            cps.append(pltpu.make_async_copy(ckv_hbm.at[layer, page], cbuf.at[slot, lanes, :], sem.at[0, slot]))
            cps.append(pltpu.make_async_copy(kpe_hbm.at[layer, page], kbuf.at[slot, :, lanes], sem.at[1, slot]))
        return cps

    def reset_state():
        m_scr[...] = jnp.full((nrow, 1), NEG_INF, f32)
        l_scr[...] = jnp.zeros((nrow, 1), f32)
        acc_scr[...] = jnp.zeros((nrow, KV_RANK), f32)

    gkr = gkr_ref[...]
    key_lane = lax.broadcasted_iota(jnp.int32, (nrow, PAGE_SIZE), 1)
    qry_row = lax.broadcasted_iota(jnp.int32, (nrow, PAGE_SIZE), 0) % t_new

    def prepare_group(t):
        b, g, pslot = t // ng, t % ng, t % PAGE_BUFFERS
        lhs_scr[n_nope:, :] = ql_all[b].astype(bf16)
        qr_scr[...] = qr_all[b].astype(bf16)
        for cp in group_copies(t):
            cp.wait()
        is_last = g == ng - 1
        cbuf[pslot, new_rows, :] = jnp.where(is_last, cnew_ref[b], cbuf[pslot, new_rows, :])
        kbuf[pslot, :, new_lanes] = jnp.where(is_last, kpenewt_ref[b], kbuf[pslot, :, new_lanes])

    def score_group(t, slot):
        g, pslot = t % ng, t % PAGE_BUFFERS
        is_last = g == ng - 1
        cb = cbuf[pslot].astype(bf16)
        cb_scr[slot] = cb
        kt = kbuf[pslot]
        cos_t, sin_t = cost_ref[g], sint_ref[g]
        big = _nt(lhs_scr[...], cb)
        kn = big[0:n_nope]
        ss_nope = jnp.sum((kn * kn).reshape(MLA_HEADS, NOPE_DIM, kb), axis=1)
        ss = ss_nope + jnp.sum(kt * kt, axis=0, keepdims=True)
        r = lax.rsqrt(ss * (1.0 / QK_DIM) + EPS)
        kg = kt * gkr
        k1, k2 = kg[0:HALF_ROPE], kg[HALF_ROPE:ROPE_DIM]
        kr = jnp.concatenate([k1 * cos_t - k2 * sin_t, k2 * cos_t + k1 * sin_t], axis=0).astype(bf16)
        s = big[n_nope:] + _mm(qr_scr[...], kr)
        s = (s.reshape(MLA_HEADS, t_new, kb) * r[:, None, :]).reshape(nrow, kb)
        first_dead = jnp.where(is_last, 1, PAGE_SIZE + t_new)
        tail = jnp.where(key_lane >= qry_row + first_dead, NEG_INF, s[:, kb - PAGE_SIZE:])
        s_scr[slot] = jnp.concatenate([s[:, :kb - PAGE_SIZE], tail], axis=1)

    def value_group(slot):
        s = s_scr[slot]
        m_old = m_scr[...]
        m_new = jnp.maximum(m_old, jnp.max(s, axis=-1, keepdims=True))
        alpha = jnp.exp2(m_old - m_new)
        p = jnp.exp2(s - m_new)
        l_scr[...] = l_scr[...] * alpha + jnp.sum(p, axis=-1, keepdims=True)
        acc_scr[...] = acc_scr[...] * alpha + _mm(p.astype(bf16), cb_scr[slot])
        m_scr[...] = m_new

    def finish_batch(b, closed):
        ql_all[b] = jnp.where(closed, acc_scr[...] * (1.0 / l_scr[...]), ql_all[b])
        m_scr[...] = jnp.where(closed, NEG_INF, m_scr[...])
        l_scr[...] = jnp.where(closed, 0.0, l_scr[...])
        acc_scr[...] = jnp.where(closed, 0.0, acc_scr[...])

    lhs_scr[0:n_nope, :] = wukt_ref[...]
    for h in range(MLA_HEADS):
        q_h = q_ref[:, h * HEAD_PAD:(h + 1) * HEAD_PAD]
        rows = slice(h * t_new, (h + 1) * t_new)
        ql_all[:, rows, :] = _mm(q_h, wukg_ref[h]).reshape(ndb, t_new, KV_RANK)
        qr_all[:, rows, :] = q_h[:, KPE_LANE:KPE_LANE + ROPE_DIM].astype(f32).reshape(ndb, t_new, ROPE_DIM)
    reset_state()

    ahead = PAGE_BUFFERS - 1

    def skewed_step(t, slot, prefetch):
        if prefetch:
            for cp in group_copies(t + ahead):
                cp.start()
        prepare_group(t)
        score_group(t, slot)
        value_group(1 - slot)
        closes = t % ng == 0
        finish_batch(jnp.maximum(t // ng - 1, 0), closes)

    for t in range(min(ahead, total)):
        for cp in group_copies(t):
            cp.start()
    if total > ahead:
        for cp in group_copies(ahead):
            cp.start()
    prepare_group(0)
    score_group(0, 0)

    n_pairs = max(total - 1 - ahead, 0) // 2

    def body(k, carry):
        skewed_step(2 * k + 1, 1, True)
        skewed_step(2 * k + 2, 0, True)
        return carry

    lax.fori_loop(0, n_pairs, body, 0)
    for t in range(2 * n_pairs + 1, total):
        skewed_step(t, t % 2, t + ahead < total)
    value_group((total - 1) % 2)
    finish_batch(ndb - 1, True)

    for h in range(MLA_HEADS):
        lat_h = ql_all[:, h * t_new:(h + 1) * t_new, :].reshape(ndb * t_new, KV_RANK).astype(bf16)
        contrib = _mm(lat_h, wuv_ref[h])
        ps = slice((h // 2) * LANES, (h // 2 + 1) * LANES)
        if h % 2 == 0:
            o_ref[:, ps] = contrib
        else:
            o_ref[:, ps] += contrib


def _mla_sample(layer, q, c_new, kpe_new, cache_ckv, cache_kpe_t, page_table, w, tabs_t, gp):
    ndb, n_pages = page_table.shape
    t_new = q.shape[0] // ndb
    ng = (n_pages + 1) // gp
    kb = gp * PAGE_SIZE
    nrow = MLA_HEADS * t_new
    cost, sint = tabs_t
    by_group = lambda a: a.reshape(HALF_ROPE, ng, kb).transpose(1, 0, 2)
    kpe_new_t = jnp.pad(kpe_new.reshape(ndb, t_new, ROPE_DIM).transpose(0, 2, 1),
                        ((0, 0), (0, 0), (0, PAGE_SIZE - t_new)))
    args = (q, c_new.reshape(ndb, t_new, KV_RANK), kpe_new_t, w['w_uk_t'], w['w_uk_g'], w['g_k_rope'],
            by_group(cost), by_group(sint), w['w_uv_pairs'])

    def full(a):
        return pl.BlockSpec(a.shape, lambda i, pt, nd=a.ndim: (0,) * nd)

    return pl.pallas_call(
        functools.partial(_mla_sample_body, layer=layer, gp=gp, n_pages=n_pages, t_new=t_new),
        grid_spec=pltpu.PrefetchScalarGridSpec(
            num_scalar_prefetch=1,
            grid=(1,),
            in_specs=[full(a) for a in args] + [pl.BlockSpec(memory_space=pl.ANY)] * 2,
            out_specs=pl.BlockSpec((ndb * t_new, MLA_WIDTH), lambda i, pt: (0, 0)),
            scratch_shapes=[pltpu.VMEM((PAGE_BUFFERS, kb, KV_RANK), f32),
                            pltpu.VMEM((PAGE_BUFFERS, ROPE_DIM, kb), f32),
                            pltpu.SemaphoreType.DMA((2, PAGE_BUFFERS)),
                            pltpu.VMEM((MLA_HEADS * NOPE_DIM + nrow, KV_RANK), bf16),
                            pltpu.VMEM((nrow, ROPE_DIM), bf16),
                            pltpu.VMEM((2, kb, KV_RANK), bf16),
                            pltpu.VMEM((2, nrow, kb), f32),
                            pltpu.VMEM((ndb, nrow, KV_RANK), f32),
                            pltpu.VMEM((ndb, nrow, ROPE_DIM), f32),
                            pltpu.VMEM((nrow, KV_RANK), f32),
                            pltpu.VMEM((nrow, 1), f32),
                            pltpu.VMEM((nrow, 1), f32)]),
        out_shape=jax.ShapeDtypeStruct((ndb * t_new, MLA_WIDTH), f32),
        compiler_params=_cparams("arbitrary"),
        name="mla_sample",
    )(page_table.reshape(-1), *args, cache_ckv, cache_kpe_t)


def _split3(x):
    hi = x.astype(bf16)
    r1 = x - hi.astype(f32)
    mid = r1.astype(bf16)
    lo = (r1 - mid.astype(f32)).astype(bf16)
    return hi, mid, lo


def _gla_body(*refs, nb, tt, cs, has_init):
    if has_init:
        q_ref, k_ref, v_ref, g_ref, s0_ref, o_ref, sT_out, st_scr = refs
    else:
        q_ref, k_ref, v_ref, g_ref, o_ref, sT_out, st_scr = refs
    ti = pl.program_id(1)
    nchunk = tt // cs

    @pl.when(ti == 0)
    def _():
        if has_init:
            st_scr[...] = s0_ref[...]
        else:
            st_scr[...] = jnp.zeros(st_scr.shape, f32)

    tri = (lax.broadcasted_iota(jnp.int32, (tt, tt), 1) <= lax.broadcasted_iota(jnp.int32, (tt, tt), 0)).astype(bf16)
    kv_head = (lax.broadcasted_iota(jnp.int32, (GLA_KDIM, GLA_WIDTH), 0) // GLA_DK
               == lax.broadcasted_iota(jnp.int32, (GLA_KDIM, GLA_WIDTH), 1) // GLA_DV)
    block_ones = kv_head.astype(bf16)
    vk_head = (lax.broadcasted_iota(jnp.int32, (GLA_WIDTH, GLA_KDIM), 0) // GLA_DV
               == lax.broadcasted_iota(jnp.int32, (GLA_WIDTH, GLA_KDIM), 1) // GLA_DK)
    sel_t = (lax.broadcasted_iota(jnp.int32, (cs, cs * cs), 1) // cs
             == lax.broadcasted_iota(jnp.int32, (cs, cs * cs), 0)).astype(bf16)
    t_idx = lax.broadcasted_iota(jnp.int32, (cs, cs, GLA_KDIM), 0)
    s_idx = lax.broadcasted_iota(jnp.int32, (cs, cs, GLA_KDIM), 1)
    causal3 = s_idx <= t_idx

    cums = []
    for b in range(nb):
        hi, mid, lo = _split3(g_ref[b])
        cums.append(_mm(tri, hi) + _mm(tri, mid) + _mm(tri, lo))
    for n in range(nchunk):
        sl = slice(n * cs, (n + 1) * cs)
        for b in range(nb):
            cum = cums[b]
            base = cum[n * cs - 1:n * cs] if n > 0 else jnp.zeros((1, GLA_KDIM), f32)
            bc = cum[sl] - base
            b_last = bc[cs - 1:cs]
            qc, kc, vc = q_ref[b, sl, :], k_ref[b, sl, :], v_ref[b, sl, :]
            st = st_scr[b]
            o_inter = _nt((qc * jnp.exp2(bc)).astype(bf16), st.astype(bf16))
            diff = jnp.where(causal3, bc[:, None, :] - bc[None, :, :], NEG_INF)
            d3 = qc[:, None, :] * kc[None, :, :] * jnp.exp2(diff)
            a_exp = _mm(d3.reshape(cs * cs, GLA_KDIM).astype(bf16), block_ones)
            xv = (a_exp.reshape(cs, cs, GLA_WIDTH) * vc[None, :, :]).reshape(cs * cs, GLA_WIDTH)
            o_intra = _mm(sel_t, xv.astype(bf16))
            o_ref[b, sl, :] = o_inter + o_intra
            kd = (kc * jnp.exp2(b_last - bc)).astype(bf16)
            upd = _tn(vc.astype(bf16), kd)
            st_scr[b] = st * jnp.exp2(b_last) + jnp.where(vk_head, upd, 0.0)

    @pl.when(ti == pl.num_programs(1) - 1)
    def _():
        sT_out[...] = st_scr[...]


def _gla(q, k, v, g, s0t, nb, tt, cs):
    B, L, _ = q.shape
    has_init = s0t is not None
    tok = lambda bi, ti: (bi, ti, 0)
    st = lambda bi, ti: (bi, 0, 0)
    in_specs = [pl.BlockSpec((nb, tt, GLA_KDIM), tok), pl.BlockSpec((nb, tt, GLA_KDIM), tok),
                pl.BlockSpec((nb, tt, GLA_WIDTH), tok), pl.BlockSpec((nb, tt, GLA_KDIM), tok)]
    args = [q, k, v, g]
    if has_init:
        in_specs.append(pl.BlockSpec((nb, GLA_WIDTH, GLA_KDIM), st))
        args.append(s0t)
    return pl.pallas_call(
        functools.partial(_gla_body, nb=nb, tt=tt, cs=cs, has_init=has_init),
        grid=(B // nb, L // tt),
        in_specs=in_specs,
        out_specs=[pl.BlockSpec((nb, tt, GLA_WIDTH), tok), pl.BlockSpec((nb, GLA_WIDTH, GLA_KDIM), st)],
        out_shape=[jax.ShapeDtypeStruct((B, L, GLA_WIDTH), f32), jax.ShapeDtypeStruct((B, GLA_WIDTH, GLA_KDIM), f32)],
        scratch_shapes=[pltpu.VMEM((nb, GLA_WIDTH, GLA_KDIM), f32)],
        compiler_params=_cparams("arbitrary", "arbitrary"),
        name="gla",
    )(*args)


def _memkv_body(mem_ref, gmem_ref, w_ref, gk_ref, mkt_out, mvt_out):
    xb = _rms_rows(mem_ref[...], gmem_ref[...]).astype(bf16)
    kv = _mm(xb, w_ref[...])
    lane = lax.broadcasted_iota(jnp.int32, (1, LANES), 1)
    gk = gk_ref[...]
    for j in range(MEM_WIDTH // LANES):
        blk = _half_head_rms(kv[:, j * LANES:(j + 1) * LANES], gk, lane)
        mkt_out[j * LANES:(j + 1) * LANES, :] = blk.T
        mvt_out[j * LANES:(j + 1) * LANES, :] = kv[:, MEM_WIDTH + j * LANES:MEM_WIDTH + (j + 1) * LANES].T


def _memkv(mem2, w, nb, n_mem):
    row = lambda i: (i, 0)
    fix = lambda i: (0, 0)
    return pl.pallas_call(
        _memkv_body,
        grid=(nb,),
        in_specs=[pl.BlockSpec((n_mem, D_MODEL), row), pl.BlockSpec(w['g_mem'].shape, fix),
                  pl.BlockSpec(w['w_mem_kv'].shape, fix), pl.BlockSpec(w['g_mem_k'].shape, fix)],
        out_specs=[pl.BlockSpec((None, MEM_WIDTH, n_mem), lambda i: (i, 0, 0))] * 2,
        out_shape=[jax.ShapeDtypeStruct((nb, MEM_WIDTH, n_mem), f32)] * 2,
        compiler_params=_cparams("arbitrary"),
        name="memkv",
    )(mem2, w['g_mem'], w['w_mem_kv'], w['g_mem_k'])


def _memattn_body(q_ref, mkt_ref, mvt_ref, o_ref, *, nb, tq):
    lane_head = lax.broadcasted_iota(jnp.int32, (1, MEM_WIDTH), 1) // MEM_HEAD_DIM
    for b in range(nb):
        q = q_ref[b]
        mkt = mkt_ref[b].astype(bf16)
        mvt = mvt_ref[b].astype(bf16)
        qs = jnp.concatenate([jnp.where(lane_head == h, q, jnp.zeros_like(q)) for h in range(MEM_HEADS)], axis=0)
        s = _mm(qs, mkt)
        p = jnp.exp2(s - jnp.max(s, axis=-1, keepdims=True))
        pv = _nt(p.astype(bf16), mvt) * (1.0 / jnp.sum(p, axis=-1, keepdims=True))
        o = jnp.zeros((tq, MEM_WIDTH), f32)
        for h in range(MEM_HEADS):
            o = o + jnp.where(lane_head == h, pv[h * tq:(h + 1) * tq], 0.0)
        o_ref[b] = o


def _memattn(layer, q3, mkt4, mvt4, nb, tq):
    B, L, _ = q3.shape
    n_mem = mkt4.shape[-1]
    kv_spec = pl.BlockSpec((None, nb, MEM_WIDTH, n_mem), lambda b, i: (layer, b, 0, 0))
    return pl.pallas_call(
        functools.partial(_memattn_body, nb=nb, tq=tq),
        grid=(B // nb, L // tq),
        in_specs=[pl.BlockSpec((nb, tq, MEM_WIDTH), lambda b, i: (b, i, 0)), kv_spec, kv_spec],
        out_specs=pl.BlockSpec((nb, tq, MEM_WIDTH), lambda b, i: (b, i, 0)),
        out_shape=jax.ShapeDtypeStruct((B, L, MEM_WIDTH), f32),
        compiler_params=_cparams("arbitrary", "arbitrary"),
        name="memattn",
    )(q3, mkt4, mvt4)


def _merge_body(x_ref, mla_ref, gla_ref, mem_ref, gates_ref, ggla_ref, wout_ref, y_ref):
    lane = lax.broadcasted_iota(jnp.int32, (1, LANES), 1)
    ggla = ggla_ref[...]
    y = x_ref[...]
    m1 = (gates_ref[:, 0:MLA_WIDTH] * mla_ref[...]).astype(bf16)
    y = y + _mm(m1, wout_ref[0:MLA_WIDTH, :])
    for j in range(GLA_WIDTH // LANES):
        ls = slice(j * LANES, (j + 1) * LANES)
        gn = _half_head_rms(gla_ref[:, ls], ggla, lane)
        gs = slice(MLA_WIDTH + j * LANES, MLA_WIDTH + (j + 1) * LANES)
        y = y + _mm((gates_ref[:, gs] * gn).astype(bf16), wout_ref[gs, :])
    ms = slice(MLA_WIDTH + GLA_WIDTH, D_MIX)
    m3 = (gates_ref[:, ms] * mem_ref[...]).astype(bf16)
    y_ref[...] = y + _mm(m3, wout_ref[ms, :])


def _merge(x2, mla_o, gla_o, mem_o, gates, w, tm):
    T = x2.shape[0]
    row = lambda i: (i, 0)
    fix = lambda i: (0, 0)
    return pl.pallas_call(
        _merge_body,
        grid=(T // tm,),
        in_specs=[pl.BlockSpec((tm, D_MODEL), row), pl.BlockSpec((tm, MLA_WIDTH), row),
                  pl.BlockSpec((tm, GLA_WIDTH), row), pl.BlockSpec((tm, MEM_WIDTH), row),
                  pl.BlockSpec((tm, D_MIX), row), pl.BlockSpec(w['g_gla_o'].shape, fix),
                  pl.BlockSpec(w['w_out'].shape, fix)],
        out_specs=pl.BlockSpec((tm, D_MODEL), row),
        out_shape=jax.ShapeDtypeStruct((T, D_MODEL), f32),
        compiler_params=_cparams("arbitrary"),
        name="merge",
    )(x2, mla_o, gla_o, mem_o, gates, w['g_gla_o'], w['w_out'])


def _pad_head(a):
    return jnp.pad(a, [(0, 0)] * (a.ndim - 1) + [(0, HEAD_PAD - QK_DIM)])


def _prep_layer(l, p):
    w_in = p['w_in'][l]
    cuts, o = [], 0
    for n in (Q_RANK, KV_RANK, ROPE_DIM, MLA_WIDTH, GLA_KDIM, GLA_KDIM, GLA_WIDTH, GLA_GATE_RANK, GLA_WIDTH,
              MEM_WIDTH, MEM_WIDTH):
        cuts.append(w_in[:, o:o + n])
        o += n
    cq, ckv, kpe, gate_mla, gq, gk, gv, gg, gate_gla, mq, gate_mem = cuts
    z = lambda n: jnp.zeros((D_MODEL, n), f32)
    misc = jnp.concatenate([gg, z(KPE_LANE - GLA_GATE_RANK), kpe, z(LANES - KPE_LANE - ROPE_DIM)], axis=1)
    w_in_packed = jnp.concatenate([cq, ckv, gate_mla, gq, gk, gv, gate_gla, mq, gate_mem, misc], axis=1)

    w_uk = p['w_uk'][l]
    g_k = p['g_mla_k'][l]
    wuk_packed = jnp.pad(w_uk, ((0, 0), (0, 0), (0, HEAD_PAD - NOPE_DIM))).reshape(KV_RANK, QK_PACKED)
    w_uk_t = w_uk.reshape(KV_RANK, MLA_HEADS * NOPE_DIM).T
    w_uk_g = jnp.pad((w_uk * g_k[None, None, :NOPE_DIM]).transpose(1, 2, 0),
                     ((0, 0), (0, HEAD_PAD - NOPE_DIM), (0, 0)))
    w_uv = p['w_uv'][l].transpose(1, 0, 2)
    zeros = jnp.zeros_like(w_uv)
    even = jnp.arange(MLA_HEADS)[:, None, None] % 2 == 0
    w_uv_pairs = jnp.concatenate([jnp.where(even, w_uv, zeros), jnp.where(even, zeros, w_uv)], axis=-1)
    w_gk = jnp.pad(p['w_gk'][l], ((0, LANES - GLA_GATE_RANK), (0, 0)))
    row = lambda a: a.reshape(1, -1)
    return {
        'g_pre': row(p['g_pre'][l]), 'w_in': w_in_packed.astype(bf16), 'g_qa': row(p['g_qa'][l]),
        'w_uq': _pad_head(p['w_uq'][l]).reshape(Q_RANK, QK_PACKED).astype(bf16),
        'g_kva': row(p['g_kva'][l]), 'w_uk': wuk_packed.astype(bf16),
        'g_q': row(_pad_head(p['g_mla_q'][l])), 'g_k': row(_pad_head(g_k)),
        'w_gk': w_gk.astype(bf16), 'b_gk': row(p['b_gk'][l]),
        'g_mem_q': row(jnp.tile(p['g_mem_q'][l], 2)), 'g_mem_k': row(jnp.tile(p['g_mem_k'][l], 2)),
        'g_gla_o': row(jnp.tile(p['g_gla_o'][l], 2)),
        'w_uk_t': w_uk_t.astype(bf16), 'w_uk_g': w_uk_g.astype(bf16),
        'g_k_rope': g_k[NOPE_DIM:].reshape(ROPE_DIM, 1),
        'w_uv_pairs': w_uv_pairs.astype(bf16),
        'w_uv_t': w_uv.transpose(0, 2, 1).astype(bf16),
        'g_mem': row(p['g_mem'][l]), 'w_mem_kv': p['w_mem_kv'][l].astype(bf16),
        'w_out': p['w_out'][l].astype(bf16),
    }


def _angles(pos):
    inv_freq = ROPE_THETA ** (-jnp.arange(HALF_ROPE, dtype=f32) * (2.0 / ROPE_DIM))
    return pos.astype(f32)[:, None] * inv_freq[None, :]


def _packed_tables(pos):
    ang = _angles(pos)
    n = pos.shape[0]
    cos, sin = jnp.cos(ang), jnp.sin(ang)
    one = jnp.ones((n, NOPE_DIM), f32)
    z = lambda w: jnp.zeros((n, w), f32)
    tail = HEAD_PAD - QK_DIM
    cos_p = jnp.concatenate([one, cos, cos, z(tail)], axis=1)
    sin_up = jnp.concatenate([z(NOPE_DIM + HALF_ROPE), sin, z(tail)], axis=1)
    sin_dn = jnp.concatenate([z(NOPE_DIM), -sin, z(HALF_ROPE + tail)], axis=1)
    return cos_p, sin_up, sin_dn


def _transposed_tables(pos):
    ang = _angles(pos).T
    return jnp.cos(ang), jnp.sin(ang)


def _state_to_t(s):
    B = s.shape[0]
    st = s.transpose(0, 1, 3, 2)
    eye = jnp.eye(GLA_HEADS, dtype=s.dtype)
    return jnp.einsum('bhvk,hg->bhvgk', st, eye).reshape(B, GLA_WIDTH, GLA_KDIM)


def _t_to_state(st):
    B = st.shape[0]
    s5 = st.reshape(B, GLA_HEADS, GLA_DV, GLA_HEADS, GLA_DK)
    diag = jnp.stack([s5[:, h, :, h, :] for h in range(GLA_HEADS)], axis=1)
    return diag.transpose(0, 1, 3, 2)


def kernel(x_prompt, x_sample, mem_prompt, cache_ckv, cache_kpe, page_table, state_gla, cache_mem_k, cache_mem_v, g_pre, w_in, g_qa, w_uq, g_kva, w_uk, w_uv, g_mla_q, g_mla_k, w_gk, b_gk, g_gla_o, g_mem, w_mem_kv, g_mem_q, g_mem_k, w_out):
    params = dict(g_pre=g_pre, w_in=w_in, g_qa=g_qa, w_uq=w_uq, g_kva=g_kva, w_uk=w_uk, w_uv=w_uv,
                  g_mla_q=g_mla_q, g_mla_k=g_mla_k, w_gk=w_gk, b_gk=b_gk, g_gla_o=g_gla_o, g_mem=g_mem,
                  w_mem_kv=w_mem_kv, g_mem_q=g_mem_q, g_mem_k=g_mem_k, w_out=w_out)
    nb, seq, _ = x_prompt.shape
    ndb, t_new, _ = x_sample.shape
    n_mem = mem_prompt.shape[1]
    depth = w_in.shape[0]
    n_pages = page_table.shape[1]
    past_len = n_pages * cache_ckv.shape[2]

    tm_p = min(256, seq)
    tm_merge = min(512, seq)
    tq_mem = min(1024, seq)
    tq = min(256, seq)
    tm_s = min(256, ndb * t_new)
    pp = max(d for d in range(1, 17) if (n_pages + 1) % d == 0)
    gla_tt = min(128, seq)
    gla_cs = min(16, gla_tt)
    nb_s = min(8, ndb)
    nbp = min(4, nb)
    nb_gla = min(8, nb)

    cache_kpe_t = cache_kpe.transpose(0, 1, 3, 2)
    to_t = lambda a: a.transpose(0, 1, 3, 4, 2).reshape(depth, ndb, MEM_WIDTH, n_mem)
    cache_mkt, cache_mvt = to_t(cache_mem_k), to_t(cache_mem_v)
    from_t = lambda a: a.reshape(nb, MEM_HEADS, MEM_HEAD_DIM, n_mem).transpose(0, 3, 1, 2)

    tabs_p = _packed_tables(jnp.arange(seq, dtype=jnp.int32))
    pos_s = past_len + jnp.arange(t_new, dtype=jnp.int32)
    tabs_s = tuple(jnp.tile(t, (tm_s // t_new, 1)) for t in _packed_tables(pos_s))
    tabs_t = _transposed_tables(jnp.arange(past_len + PAGE_SIZE, dtype=jnp.int32))

    xp = x_prompt.reshape(nb * seq, D_MODEL)
    xs = x_sample.reshape(ndb * t_new, D_MODEL)
    mem2 = mem_prompt.reshape(nb * n_mem, D_MODEL)
    ckv_p, kpe_p, gla_p, mk_p, mv_p, ckv_s, kpe_s, gla_s = ([] for _ in range(8))
    for l in range(depth):
        w = _prep_layer(l, params)
        q, k, c, kpe, gates, gq, gk, gv, gg, mq = _inproj(xp, tabs_p, w, tm_p)
        r3 = lambda a: a.reshape(nb, seq, a.shape[-1])
        mla_o = _mla_prompt(r3(q), r3(k), r3(c), w['w_uv_t'], nb, seq, tq, nbp).reshape(nb * seq, MLA_WIDTH)
        gla_o, st = _gla(r3(gq), r3(gk), r3(gv), r3(gg), None, nb_gla, gla_tt, gla_cs)
        mkt, mvt = _memkv(mem2, w, nb, n_mem)
        mem_o = _memattn(0, r3(mq), mkt[None], mvt[None], 1, tq_mem)
        xp = _merge(xp, mla_o, gla_o.reshape(nb * seq, GLA_WIDTH), mem_o.reshape(nb * seq, MEM_WIDTH), gates, w, tm_merge)
        ckv_p.append(c.reshape(nb, seq, KV_RANK))
        kpe_p.append(kpe.reshape(nb, seq, ROPE_DIM))
        gla_p.append(_t_to_state(st))
        mk_p.append(from_t(mkt))
        mv_p.append(from_t(mvt))
        q, k, c, kpe, gates, gq, gk, gv, gg, mq = _inproj(xs, tabs_s, w, tm_s)
        mla_o = _mla_sample(l, q, c, kpe, cache_ckv, cache_kpe_t, page_table, w, tabs_t, pp)
        r3 = lambda a: a.reshape(ndb, t_new, a.shape[-1])
        gla_o, st = _gla(r3(gq), r3(gk), r3(gv), r3(gg), _state_to_t(state_gla[l]), nb_s, t_new, t_new)
        mem_o = _memattn(l, r3(mq), cache_mkt, cache_mvt, nb_s, t_new)
        xs = _merge(xs, mla_o, gla_o.reshape(ndb * t_new, GLA_WIDTH), mem_o.reshape(ndb * t_new, MEM_WIDTH), gates, w, tm_s)
        ckv_s.append(c.reshape(ndb, t_new, KV_RANK))
        kpe_s.append(kpe.reshape(ndb, t_new, ROPE_DIM))
        gla_s.append(_t_to_state(st))
    return (xp.reshape(nb, seq, D_MODEL), xs.reshape(ndb, t_new, D_MODEL), jnp.stack(ckv_p), jnp.stack(kpe_p),
            jnp.stack(gla_p), jnp.stack(mk_p), jnp.stack(mv_p), jnp.stack(ckv_s), jnp.stack(kpe_s), jnp.stack(gla_s))
```

```python
import functools

import jax
import jax.numpy as jnp
from jax import lax
from jax.experimental import pallas as pl
from jax.experimental.pallas import tpu as pltpu

f32, bf16 = jnp.float32, jnp.bfloat16

D_MODEL = 1024
PAGE_SIZE = 128
MLA_HEADS = 8
NOPE_DIM = 64
ROPE_DIM = 32
HALF_ROPE = ROPE_DIM // 2
QK_DIM = NOPE_DIM + ROPE_DIM
V_DIM = 64
Q_RANK = 384
KV_RANK = 256
MLA_WIDTH = MLA_HEADS * V_DIM
ROPE_THETA = 10000.0
MLA_SCALE = QK_DIM ** -0.5
LOG2_E = 1.4426950408889634
GLA_HEADS = 4
GLA_DK = 32
GLA_DV = 64
GLA_KDIM = GLA_HEADS * GLA_DK
GLA_WIDTH = GLA_HEADS * GLA_DV
GLA_GATE_RANK = 16
GLA_NORMALIZER = 16.0
MEM_HEADS = 4
MEM_HEAD_DIM = 64
MEM_WIDTH = MEM_HEADS * MEM_HEAD_DIM
MEM_SCALE = MEM_HEAD_DIM ** -0.5
D_MIX = MLA_WIDTH + GLA_WIDTH + MEM_WIDTH
EPS = 1e-6
NEG_INF = -1e30

LANES = 128
HEAD_PAD = LANES
QK_PACKED = MLA_HEADS * HEAD_PAD

OFF_CQ = 0
OFF_CKV = OFF_CQ + Q_RANK
OFF_GATE_MLA = OFF_CKV + KV_RANK
OFF_GQ = OFF_GATE_MLA + MLA_WIDTH
OFF_GK = OFF_GQ + GLA_KDIM
OFF_GV = OFF_GK + GLA_KDIM
OFF_GATE_GLA = OFF_GV + GLA_WIDTH
OFF_MQ = OFF_GATE_GLA + GLA_WIDTH
OFF_GATE_MEM = OFF_MQ + MEM_WIDTH
OFF_MISC = OFF_GATE_MEM + MEM_WIDTH
D_IN_PACKED = OFF_MISC + LANES
KPE_LANE = NOPE_DIM

VMEM_LIMIT = 56 * 1024 * 1024
PAGE_BUFFERS = 3


def _cparams(*sem):
    return pltpu.CompilerParams(dimension_semantics=sem, vmem_limit_bytes=VMEM_LIMIT)


def _nt(a, b):
    return lax.dot_general(a, b, (((1,), (1,)), ((), ())), preferred_element_type=f32)


def _tn(a, b):
    return lax.dot_general(a, b, (((0,), (0,)), ((), ())), preferred_element_type=f32)


def _mm(a, b):
    return jnp.dot(a, b, preferred_element_type=f32)


def _rms_rows(x, g):
    return x * lax.rsqrt(jnp.mean(x * x, axis=-1, keepdims=True) + EPS) * g


def _half_head_rms(blk, g, lane):
    sq = blk * blk
    lo = lane < 64
    ss_lo = jnp.sum(jnp.where(lo, sq, 0.0), axis=-1, keepdims=True)
    ss_hi = jnp.sum(jnp.where(lo, 0.0, sq), axis=-1, keepdims=True)
    ss = jnp.where(lo, ss_lo, ss_hi)
    return blk * lax.rsqrt(ss * (1.0 / 64.0) + EPS) * g


def _rope_packed(x, cos, sin_up, sin_dn):
    return x * cos + pltpu.roll(x, HALF_ROPE, 1) * sin_up + pltpu.roll(x, LANES - HALF_ROPE, 1) * sin_dn


def _inproj_body(x_ref, gpre_ref, win_ref, gqa_ref, wuq_ref, gkva_ref, wuk_ref, gq_ref, gk_ref,
                 cos_ref, sup_ref, sdn_ref, wgk_ref, bgk_ref, gmq_ref,
                 q_out, k_out, ckv_out, kpe_out, gates_out, glaq_out, glak_out, glav_out, glag_out, memq_out):
    x = x_ref[...]
    xb = _rms_rows(x, gpre_ref[...]).astype(bf16)

    def seg(off, n):
        return _mm(xb, win_ref[:, off:off + n])

    cos, sup, sdn = cos_ref[...], sup_ref[...], sdn_ref[...]
    lane = lax.broadcasted_iota(jnp.int32, (1, LANES), 1)

    cq = _rms_rows(seg(OFF_CQ, Q_RANK), gqa_ref[...]).astype(bf16)
    qf = _mm(cq, wuq_ref[...])
    gq = gq_ref[...]
    for h in range(MLA_HEADS):
        qh = qf[:, h * HEAD_PAD:(h + 1) * HEAD_PAD]
        ss = jnp.sum(qh * qh, axis=-1, keepdims=True)
        qn = qh * lax.rsqrt(ss * (1.0 / QK_DIM) + EPS) * gq
        q_out[:, h * HEAD_PAD:(h + 1) * HEAD_PAD] = (_rope_packed(qn, cos, sup, sdn) * (MLA_SCALE * LOG2_E)).astype(bf16)

    c = _rms_rows(seg(OFF_CKV, KV_RANK), gkva_ref[...])
    ckv_out[...] = c
    misc = seg(OFF_MISC, LANES)
    kpe_wide = jnp.where((lane >= KPE_LANE) & (lane < KPE_LANE + ROPE_DIM), misc, 0.0)
    kpe_out[...] = misc[:, KPE_LANE:KPE_LANE + ROPE_DIM]
    kf = _mm(c.astype(bf16), wuk_ref[...])
    gk = gk_ref[...]
    for h in range(MLA_HEADS):
        kh = kf[:, h * HEAD_PAD:(h + 1) * HEAD_PAD] + kpe_wide
        ss = jnp.sum(kh * kh, axis=-1, keepdims=True)
        kn = kh * lax.rsqrt(ss * (1.0 / QK_DIM) + EPS) * gk
        k_out[:, h * HEAD_PAD:(h + 1) * HEAD_PAD] = _rope_packed(kn, cos, sup, sdn).astype(bf16)

    g1 = seg(OFF_GATE_MLA, MLA_WIDTH)
    gates_out[:, 0:MLA_WIDTH] = g1 * jax.nn.sigmoid(g1)
    g2 = seg(OFF_GATE_GLA, GLA_WIDTH)
    gates_out[:, MLA_WIDTH:MLA_WIDTH + GLA_WIDTH] = g2 * jax.nn.sigmoid(g2)
    g3 = seg(OFF_GATE_MEM, MEM_WIDTH)
    gates_out[:, MLA_WIDTH + GLA_WIDTH:D_MIX] = g3 * jax.nn.sigmoid(g3)

    glaq_out[...] = seg(OFF_GQ, GLA_KDIM) * (GLA_DK ** -0.5)
    glak_out[...] = seg(OFF_GK, GLA_KDIM)
    glav_out[...] = seg(OFF_GV, GLA_WIDTH)
    gl = _mm(misc.astype(bf16), wgk_ref[...]) + bgk_ref[...]
    glag_out[...] = jax.nn.log_sigmoid(gl) * (LOG2_E / GLA_NORMALIZER)

    mq = seg(OFF_MQ, MEM_WIDTH)
    gmq = gmq_ref[...]
    for j in range(MEM_WIDTH // LANES):
        blk = _half_head_rms(mq[:, j * LANES:(j + 1) * LANES], gmq, lane)
        memq_out[:, j * LANES:(j + 1) * LANES] = (blk * (MEM_SCALE * LOG2_E)).astype(bf16)


def _inproj(x2, tabs, w, tm):
    T = x2.shape[0]
    nt = tabs[0].shape[0] // tm
    row = lambda i: (i, 0)
    fix = lambda i: (0, 0)
    tab = lambda i: (i % nt, 0)

    def full(a):
        return pl.BlockSpec(a.shape, fix)

    consts1 = (w['g_pre'], w['w_in'], w['g_qa'], w['w_uq'], w['g_kva'], w['w_uk'], w['g_q'], w['g_k'])
    consts2 = (w['w_gk'], w['b_gk'], w['g_mem_q'])
    outs = [(QK_PACKED, bf16), (QK_PACKED, bf16), (KV_RANK, f32), (ROPE_DIM, f32), (D_MIX, f32),
            (GLA_KDIM, f32), (GLA_KDIM, f32), (GLA_WIDTH, f32), (GLA_KDIM, f32), (MEM_WIDTH, bf16)]
    return pl.pallas_call(
        _inproj_body,
        grid=(T // tm,),
        in_specs=[pl.BlockSpec((tm, D_MODEL), row)] + [full(a) for a in consts1]
        + [pl.BlockSpec((tm, LANES), tab)] * 3 + [full(a) for a in consts2],
        out_specs=[pl.BlockSpec((tm, n), row) for n, _ in outs],
        out_shape=[jax.ShapeDtypeStruct((T, n), dt) for n, dt in outs],
        compiler_params=_cparams("arbitrary"),
        name="inproj",
    )(x2, *consts1, *tabs, *consts2)


def _mla_prompt_body(q_ref, k_ref, c_ref, wuvt_ref, o_ref, ct_scr, acc_scr, m_scr, l_scr, ot_scr, *, tq, nbp):
    i = pl.program_id(1)
    nblk = ct_scr.shape[1]

    @pl.when(i == 0)
    def _():
        for bb in range(nbp):
            for j in range(nblk):
                ct_scr[bb, j] = c_ref[bb, j * tq:(j + 1) * tq, :].T.astype(bf16)

    m_scr[...] = jnp.full(m_scr.shape, NEG_INF, f32)
    l_scr[...] = jnp.zeros(l_scr.shape, f32)
    acc_scr[...] = jnp.zeros(acc_scr.shape, f32)
    key = lax.broadcasted_iota(jnp.int32, (tq, tq), 0)
    qry = lax.broadcasted_iota(jnp.int32, (tq, tq), 1)
    causal = key <= qry

    def step(kb, masked):
        ks = pl.ds(pl.multiple_of(kb * tq, tq), tq)
        for h in range(MLA_HEADS):
            hs = slice(h * HEAD_PAD, (h + 1) * HEAD_PAD)
            for bb in range(nbp):
                s = _nt(k_ref[bb, ks, hs], q_ref[bb, :, hs])
                if masked:
                    s = jnp.where(causal, s, NEG_INF)
                m_old = m_scr[bb, h:h + 1, :]
                m_new = jnp.maximum(m_old, jnp.max(s, axis=0, keepdims=True))
                alpha = jnp.exp2(m_old - m_new)
                p = jnp.exp2(s - m_new)
                l_scr[bb, h:h + 1, :] = l_scr[bb, h:h + 1, :] * alpha + jnp.sum(p, axis=0, keepdims=True)
                acc_scr[bb, h] = acc_scr[bb, h] * alpha + _mm(ct_scr[bb, kb], p.astype(bf16))
                m_scr[bb, h:h + 1, :] = m_new

    def body(kb, carry):
        step(kb, False)
        return carry

    lax.fori_loop(0, i, body, 0)
    step(i, True)

    for bb in range(nbp):
        for h in range(MLA_HEADS):
            lat_t = (acc_scr[bb, h] * (1.0 / l_scr[bb, h:h + 1, :])).astype(bf16)
            ot_scr[h * V_DIM:(h + 1) * V_DIM, :] = _mm(wuvt_ref[h], lat_t)
        o_ref[bb] = ot_scr[...].T


def _mla_prompt(q, k, c, wuv_t, nb, seq, tq, nbp):
    nq = seq // tq
    return pl.pallas_call(
        functools.partial(_mla_prompt_body, tq=tq, nbp=nbp),
        grid=(nb // nbp, nq),
        in_specs=[pl.BlockSpec((nbp, tq, QK_PACKED), lambda b, i: (b, i, 0)),
                  pl.BlockSpec((nbp, seq, QK_PACKED), lambda b, i: (b, 0, 0), pipeline_mode=pl.Buffered(1)),
                  pl.BlockSpec((nbp, seq, KV_RANK), lambda b, i: (b, 0, 0), pipeline_mode=pl.Buffered(1)),
                  pl.BlockSpec(wuv_t.shape, lambda b, i: (0, 0, 0))],
        out_specs=pl.BlockSpec((nbp, tq, MLA_WIDTH), lambda b, i: (b, i, 0)),
        out_shape=jax.ShapeDtypeStruct((nb, seq, MLA_WIDTH), f32),
        scratch_shapes=[pltpu.VMEM((nbp, nq, KV_RANK, tq), bf16), pltpu.VMEM((nbp, MLA_HEADS, KV_RANK, tq), f32),
                        pltpu.VMEM((nbp, MLA_HEADS, tq), f32), pltpu.VMEM((nbp, MLA_HEADS, tq), f32),
                        pltpu.VMEM((MLA_WIDTH, tq), f32)],
        compiler_params=_cparams("arbitrary", "arbitrary"),
        name="mla_prompt",
    )(q, k, c, wuv_t)


def _mla_sample_body(pt_ref, q_ref, cnew_ref, kpenewt_ref, wukt_ref, wukg_ref, gkr_ref, cost_ref, sint_ref, wuv_ref,
                     ckv_hbm, kpe_hbm, o_ref,
                     cbuf, kbuf, sem, lhs_scr, qr_scr, cb_scr, s_scr, ql_all, qr_all, acc_scr, m_scr, l_scr,
                     *, layer, gp, n_pages, t_new):
    ndb = ql_all.shape[0]
    ng = (n_pages + 1) // gp
    kb = gp * PAGE_SIZE
    total = ndb * ng
    nrow = MLA_HEADS * t_new
    n_nope = MLA_HEADS * NOPE_DIM
    new_rows = pl.ds(kb - PAGE_SIZE, t_new)
    new_lanes = pl.ds(kb - PAGE_SIZE, PAGE_SIZE)

    def group_copies(t):
        b, g, slot = t // ng, t % ng, t % PAGE_BUFFERS
        cps = []
        for i in range(gp):
            page = pt_ref[b * n_pages + jnp.minimum(g * gp + i, n_pages - 1)]
            lanes = pl.ds(i * PAGE_SIZE, PAGE_SIZE)
            cps.append(pltpu.make_async_copy(ckv_hbm.at[layer, page], cbuf.at[slot, lanes, :], sem.at[0, slot]))
            cps.append(pltpu.make_async_copy(kpe_hbm.at[layer, page], kbuf.at[slot, :, lanes], sem.at[1, slot]))
        return cps

    def reset_state():
        m_scr[...] = jnp.full((nrow, 1), NEG_INF, f32)
        l_scr[...] = jnp.zeros((nrow, 1), f32)
        acc_scr[...] = jnp.zeros((nrow, KV_RANK), f32)

    gkr = gkr_ref[...]
    key_lane = lax.broadcasted_iota(jnp.int32, (nrow, PAGE_SIZE), 1)
    qry_row = lax.broadcasted_iota(jnp.int32, (nrow, PAGE_SIZE), 0) % t_new

    def prepare_group(t):
        b, g, pslot = t // ng, t % ng, t % PAGE_BUFFERS
        lhs_scr[n_nope:, :] = ql_all[b].astype(bf16)
        qr_scr[...] = qr_all[b].astype(bf16)
        for cp in group_copies(t):
            cp.wait()
        is_last = g == ng - 1
        cbuf[pslot, new_rows, :] = jnp.where(is_last, cnew_ref[b], cbuf[pslot, new_rows, :])
        kbuf[pslot, :, new_lanes] = jnp.where(is_last, kpenewt_ref[b], kbuf[pslot, :, new_lanes])

    def score_group(t, slot):
        g, pslot = t % ng, t % PAGE_BUFFERS
        is_last = g == ng - 1
        cb = cbuf[pslot].astype(bf16)
        cb_scr[slot] = cb
        kt = kbuf[pslot]
        cos_t, sin_t = cost_ref[g], sint_ref[g]
        big = _nt(lhs_scr[...], cb)
        kn = big[0:n_nope]
        ss_nope = jnp.sum((kn * kn).reshape(MLA_HEADS, NOPE_DIM, kb), axis=1)
        ss = ss_nope + jnp.sum(kt * kt, axis=0, keepdims=True)
        r = lax.rsqrt(ss * (1.0 / QK_DIM) + EPS)
        kg = kt * gkr
        k1, k2 = kg[0:HALF_ROPE], kg[HALF_ROPE:ROPE_DIM]
        kr = jnp.concatenate([k1 * cos_t - k2 * sin_t, k2 * cos_t + k1 * sin_t], axis=0).astype(bf16)
        s = big[n_nope:] + _mm(qr_scr[...], kr)
        s = (s.reshape(MLA_HEADS, t_new, kb) * r[:, None, :]).reshape(nrow, kb)
        first_dead = jnp.where(is_last, 1, PAGE_SIZE + t_new)
        tail = jnp.where(key_lane >= qry_row + first_dead, NEG_INF, s[:, kb - PAGE_SIZE:])
        s_scr[slot] = jnp.concatenate([s[:, :kb - PAGE_SIZE], tail], axis=1)

    def value_group(slot):
        s = s_scr[slot]
        m_old = m_scr[...]
        m_new = jnp.maximum(m_old, jnp.max(s, axis=-1, keepdims=True))
        alpha = jnp.exp2(m_old - m_new)
        p = jnp.exp2(s - m_new)
        l_scr[...] = l_scr[...] * alpha + jnp.sum(p, axis=-1, keepdims=True)
        acc_scr[...] = acc_scr[...] * alpha + _mm(p.astype(bf16), cb_scr[slot])
        m_scr[...] = m_new

    def finish_batch(b, closed):
        ql_all[b] = jnp.where(closed, acc_scr[...] * (1.0 / l_scr[...]), ql_all[b])
        m_scr[...] = jnp.where(closed, NEG_INF, m_scr[...])
        l_scr[...] = jnp.where(closed, 0.0, l_scr[...])
        acc_scr[...] = jnp.where(closed, 0.0, acc_scr[...])

    lhs_scr[0:n_nope, :] = wukt_ref[...]
    for h in range(MLA_HEADS):
        q_h = q_ref[:, h * HEAD_PAD:(h + 1) * HEAD_PAD]
        rows = slice(h * t_new, (h + 1) * t_new)
        ql_all[:, rows, :] = _mm(q_h, wukg_ref[h]).reshape(ndb, t_new, KV_RANK)
        qr_all[:, rows, :] = q_h[:, KPE_LANE:KPE_LANE + ROPE_DIM].astype(f32).reshape(ndb, t_new, ROPE_DIM)
    reset_state()

    ahead = PAGE_BUFFERS - 1

    def skewed_step(t, slot, prefetch):
        if prefetch:
            for cp in group_copies(t + ahead):
                cp.start()
        prepare_group(t)
        score_group(t, slot)
        value_group(1 - slot)
        closes = t % ng == 0
        finish_batch(jnp.maximum(t // ng - 1, 0), closes)

    for t in range(min(ahead, total)):
        for cp in group_copies(t):
            cp.start()
    if total > ahead:
        for cp in group_copies(ahead):
            cp.start()
    prepare_group(0)
    score_group(0, 0)

    n_pairs = max(total - 1 - ahead, 0) // 2

    def body(k, carry):
        skewed_step(2 * k + 1, 1, True)
        skewed_step(2 * k + 2, 0, True)
        return carry

    lax.fori_loop(0, n_pairs, body, 0)
    for t in range(2 * n_pairs + 1, total):
        skewed_step(t, t % 2, t + ahead < total)
    value_group((total - 1) % 2)
    finish_batch(ndb - 1, True)

    for h in range(MLA_HEADS):
        lat_h = ql_all[:, h * t_new:(h + 1) * t_new, :].reshape(ndb * t_new, KV_RANK).astype(bf16)
        contrib = _mm(lat_h, wuv_ref[h])
        ps = slice((h // 2) * LANES, (h // 2 + 1) * LANES)
        if h % 2 == 0:
            o_ref[:, ps] = contrib
        else:
            o_ref[:, ps] += contrib


def _mla_sample(layer, q, c_new, kpe_new, cache_ckv, cache_kpe_t, page_table, w, tabs_t, gp):
    ndb, n_pages = page_table.shape
    t_new = q.shape[0] // ndb
    ng = (n_pages + 1) // gp
    kb = gp * PAGE_SIZE
    nrow = MLA_HEADS * t_new
    cost, sint = tabs_t
    by_group = lambda a: a.reshape(HALF_ROPE, ng, kb).transpose(1, 0, 2)
    kpe_new_t = jnp.pad(kpe_new.reshape(ndb, t_new, ROPE_DIM).transpose(0, 2, 1),
                        ((0, 0), (0, 0), (0, PAGE_SIZE - t_new)))
    args = (q, c_new.reshape(ndb, t_new, KV_RANK), kpe_new_t, w['w_uk_t'], w['w_uk_g'], w['g_k_rope'],
            by_group(cost), by_group(sint), w['w_uv_pairs'])

    def full(a):
        return pl.BlockSpec(a.shape, lambda i, pt, nd=a.ndim: (0,) * nd)

    return pl.pallas_call(
        functools.partial(_mla_sample_body, layer=layer, gp=gp, n_pages=n_pages, t_new=t_new),
        grid_spec=pltpu.PrefetchScalarGridSpec(
            num_scalar_prefetch=1,
            grid=(1,),
            in_specs=[full(a) for a in args] + [pl.BlockSpec(memory_space=pl.ANY)] * 2,
            out_specs=pl.BlockSpec((ndb * t_new, MLA_WIDTH), lambda i, pt: (0, 0)),
            scratch_shapes=[pltpu.VMEM((PAGE_BUFFERS, kb, KV_RANK), f32),
                            pltpu.VMEM((PAGE_BUFFERS, ROPE_DIM, kb), f32),
                            pltpu.SemaphoreType.DMA((2, PAGE_BUFFERS)),
                            pltpu.VMEM((MLA_HEADS * NOPE_DIM + nrow, KV_RANK), bf16),
                            pltpu.VMEM((nrow, ROPE_DIM), bf16),
                            pltpu.VMEM((2, kb, KV_RANK), bf16),
                            pltpu.VMEM((2, nrow, kb), f32),
                            pltpu.VMEM((ndb, nrow, KV_RANK), f32),
                            pltpu.VMEM((ndb, nrow, ROPE_DIM), f32),
                            pltpu.VMEM((nrow, KV_RANK), f32),
                            pltpu.VMEM((nrow, 1), f32),
                            pltpu.VMEM((nrow, 1), f32)]),
        out_shape=jax.ShapeDtypeStruct((ndb * t_new, MLA_WIDTH), f32),
        compiler_params=_cparams("arbitrary"),
        name="mla_sample",
    )(page_table.reshape(-1), *args, cache_ckv, cache_kpe_t)


def _split3(x):
    hi = x.astype(bf16)
    r1 = x - hi.astype(f32)
    mid = r1.astype(bf16)
    lo = (r1 - mid.astype(f32)).astype(bf16)
    return hi, mid, lo


def _gla_body(*refs, nb, tt, cs, has_init):
    if has_init:
        q_ref, k_ref, v_ref, g_ref, s0_ref, o_ref, sT_out, st_scr = refs
    else:
        q_ref, k_ref, v_ref, g_ref, o_ref, sT_out, st_scr = refs
    ti = pl.program_id(1)
    nchunk = tt // cs

    @pl.when(ti == 0)
    def _():
        if has_init:
            st_scr[...] = s0_ref[...]
        else:
            st_scr[...] = jnp.zeros(st_scr.shape, f32)

    tri = (lax.broadcasted_iota(jnp.int32, (tt, tt), 1) <= lax.broadcasted_iota(jnp.int32, (tt, tt), 0)).astype(bf16)
    kv_head = (lax.broadcasted_iota(jnp.int32, (GLA_KDIM, GLA_WIDTH), 0) // GLA_DK
               == lax.broadcasted_iota(jnp.int32, (GLA_KDIM, GLA_WIDTH), 1) // GLA_DV)
    block_ones = kv_head.astype(bf16)
    vk_head = (lax.broadcasted_iota(jnp.int32, (GLA_WIDTH, GLA_KDIM), 0) // GLA_DV
               == lax.broadcasted_iota(jnp.int32, (GLA_WIDTH, GLA_KDIM), 1) // GLA_DK)
    sel_t = (lax.broadcasted_iota(jnp.int32, (cs, cs * cs), 1) // cs
             == lax.broadcasted_iota(jnp.int32, (cs, cs * cs), 0)).astype(bf16)
    t_idx = lax.broadcasted_iota(jnp.int32, (cs, cs, GLA_KDIM), 0)
    s_idx = lax.broadcasted_iota(jnp.int32, (cs, cs, GLA_KDIM), 1)
    causal3 = s_idx <= t_idx

    cums = []
    for b in range(nb):
        hi, mid, lo = _split3(g_ref[b])
        cums.append(_mm(tri, hi) + _mm(tri, mid) + _mm(tri, lo))
    for n in range(nchunk):
        sl = slice(n * cs, (n + 1) * cs)
        for b in range(nb):
            cum = cums[b]
            base = cum[n * cs - 1:n * cs] if n > 0 else jnp.zeros((1, GLA_KDIM), f32)
            bc = cum[sl] - base
            b_last = bc[cs - 1:cs]
            qc, kc, vc = q_ref[b, sl, :], k_ref[b, sl, :], v_ref[b, sl, :]
            st = st_scr[b]
            o_inter = _nt((qc * jnp.exp2(bc)).astype(bf16), st.astype(bf16))
            diff = jnp.where(causal3, bc[:, None, :] - bc[None, :, :], NEG_INF)
            d3 = qc[:, None, :] * kc[None, :, :] * jnp.exp2(diff)
            a_exp = _mm(d3.reshape(cs * cs, GLA_KDIM).astype(bf16), block_ones)
            xv = (a_exp.reshape(cs, cs, GLA_WIDTH) * vc[None, :, :]).reshape(cs * cs, GLA_WIDTH)
            o_intra = _mm(sel_t, xv.astype(bf16))
            o_ref[b, sl, :] = o_inter + o_intra
            kd = (kc * jnp.exp2(b_last - bc)).astype(bf16)
            upd = _tn(vc.astype(bf16), kd)
            st_scr[b] = st * jnp.exp2(b_last) + jnp.where(vk_head, upd, 0.0)

    @pl.when(ti == pl.num_programs(1) - 1)
    def _():
        sT_out[...] = st_scr[...]


def _gla(q, k, v, g, s0t, nb, tt, cs):
    B, L, _ = q.shape
    has_init = s0t is not None
    tok = lambda bi, ti: (bi, ti, 0)
    st = lambda bi, ti: (bi, 0, 0)
    in_specs = [pl.BlockSpec((nb, tt, GLA_KDIM), tok), pl.BlockSpec((nb, tt, GLA_KDIM), tok),
                pl.BlockSpec((nb, tt, GLA_WIDTH), tok), pl.BlockSpec((nb, tt, GLA_KDIM), tok)]
    args = [q, k, v, g]
    if has_init:
        in_specs.append(pl.BlockSpec((nb, GLA_WIDTH, GLA_KDIM), st))
        args.append(s0t)
    return pl.pallas_call(
        functools.partial(_gla_body, nb=nb, tt=tt, cs=cs, has_init=has_init),
        grid=(B // nb, L // tt),
        in_specs=in_specs,
        out_specs=[pl.BlockSpec((nb, tt, GLA_WIDTH), tok), pl.BlockSpec((nb, GLA_WIDTH, GLA_KDIM), st)],
        out_shape=[jax.ShapeDtypeStruct((B, L, GLA_WIDTH), f32), jax.ShapeDtypeStruct((B, GLA_WIDTH, GLA_KDIM), f32)],
        scratch_shapes=[pltpu.VMEM((nb, GLA_WIDTH, GLA_KDIM), f32)],
        compiler_params=_cparams("arbitrary", "arbitrary"),
        name="gla",
    )(*args)


def _gla_decode_body(q_ref, k_ref, v_ref, g_ref, s0_ref, o_ref, s_out, qt_scr, kt_scr, et_scr, vt_scr, ot_scr):
    nb, t_new, _ = q_ref.shape
    for t in range(t_new):
        qt_scr[t] = q_ref[:, t, :].T
        kt_scr[t] = k_ref[:, t, :].T
        et_scr[t] = jnp.exp2(g_ref[:, t, :].T)
        vt_scr[t] = v_ref[:, t, :].T
    ot_scr[...] = jnp.zeros(ot_scr.shape, f32)

    for h in range(GLA_HEADS):
        vs = slice(h * GLA_DV, (h + 1) * GLA_DV)

        def one_k(kk, carry):
            row = pl.ds(h * GLA_DK + kk, 1)
            s = s0_ref[h, kk]
            for t in range(t_new):
                s = s * et_scr[t, row, :] + kt_scr[t, row, :] * vt_scr[t, vs, :]
                ot_scr[t, vs, :] += qt_scr[t, row, :] * s
            s_out[h, kk] = s
            return carry

        lax.fori_loop(0, GLA_DK, one_k, 0)

    for t in range(t_new):
        o_ref[:, t, :] = ot_scr[t].T


def _gla_decode(q, k, v, g, s0):
    B, T, _ = q.shape
    full = lambda a: pl.BlockSpec(a.shape, lambda i, nd=a.ndim: (0,) * nd)
    o_shape = jax.ShapeDtypeStruct((B, T, GLA_WIDTH), f32)
    s_shape = jax.ShapeDtypeStruct(s0.shape, f32)
    return pl.pallas_call(
        _gla_decode_body,
        grid=(1,),
        in_specs=[full(a) for a in (q, k, v, g, s0)],
        out_specs=[full(o_shape), full(s_shape)],
        out_shape=[o_shape, s_shape],
        scratch_shapes=[pltpu.VMEM((T, GLA_KDIM, B), f32)] * 3 + [pltpu.VMEM((T, GLA_WIDTH, B), f32)] * 2,
        compiler_params=_cparams("arbitrary"),
        name="gla_decode",
    )(q, k, v, g, s0)


def _memkv_body(mem_ref, gmem_ref, w_ref, gk_ref, mkt_out, mvt_out):
    xb = _rms_rows(mem_ref[...], gmem_ref[...]).astype(bf16)
    kv = _mm(xb, w_ref[...])
    lane = lax.broadcasted_iota(jnp.int32, (1, LANES), 1)
    gk = gk_ref[...]
    for j in range(MEM_WIDTH // LANES):
        blk = _half_head_rms(kv[:, j * LANES:(j + 1) * LANES], gk, lane)
        mkt_out[j * LANES:(j + 1) * LANES, :] = blk.T
        mvt_out[j * LANES:(j + 1) * LANES, :] = kv[:, MEM_WIDTH + j * LANES:MEM_WIDTH + (j + 1) * LANES].T


def _memkv(mem2, w, nb, n_mem):
    row = lambda i: (i, 0)
    fix = lambda i: (0, 0)
    return pl.pallas_call(
        _memkv_body,
        grid=(nb,),
        in_specs=[pl.BlockSpec((n_mem, D_MODEL), row), pl.BlockSpec(w['g_mem'].shape, fix),
                  pl.BlockSpec(w['w_mem_kv'].shape, fix), pl.BlockSpec(w['g_mem_k'].shape, fix)],
        out_specs=[pl.BlockSpec((None, MEM_WIDTH, n_mem), lambda i: (i, 0, 0))] * 2,
        out_shape=[jax.ShapeDtypeStruct((nb, MEM_WIDTH, n_mem), f32)] * 2,
        compiler_params=_cparams("arbitrary"),
        name="memkv",
    )(mem2, w['g_mem'], w['w_mem_kv'], w['g_mem_k'])


def _memattn_body(q_ref, mkt_ref, mvt_ref, o_ref, *, nb, tq):
    lane_head = lax.broadcasted_iota(jnp.int32, (1, MEM_WIDTH), 1) // MEM_HEAD_DIM
    for b in range(nb):
        q = q_ref[b]
        mkt = mkt_ref[b].astype(bf16)
        mvt = mvt_ref[b].astype(bf16)
        qs = jnp.concatenate([jnp.where(lane_head == h, q, jnp.zeros_like(q)) for h in range(MEM_HEADS)], axis=0)
        s = _mm(qs, mkt)
        p = jnp.exp2(s - jnp.max(s, axis=-1, keepdims=True))
        pv = _nt(p.astype(bf16), mvt) * (1.0 / jnp.sum(p, axis=-1, keepdims=True))
        o = jnp.zeros((tq, MEM_WIDTH), f32)
        for h in range(MEM_HEADS):
            o = o + jnp.where(lane_head == h, pv[h * tq:(h + 1) * tq], 0.0)
        o_ref[b] = o


def _memattn(layer, q3, mkt4, mvt4, nb, tq):
    B, L, _ = q3.shape
    n_mem = mkt4.shape[-1]
    kv_spec = pl.BlockSpec((None, nb, MEM_WIDTH, n_mem), lambda b, i: (layer, b, 0, 0))
    return pl.pallas_call(
        functools.partial(_memattn_body, nb=nb, tq=tq),
        grid=(B // nb, L // tq),
        in_specs=[pl.BlockSpec((nb, tq, MEM_WIDTH), lambda b, i: (b, i, 0)), kv_spec, kv_spec],
        out_specs=pl.BlockSpec((nb, tq, MEM_WIDTH), lambda b, i: (b, i, 0)),
        out_shape=jax.ShapeDtypeStruct((B, L, MEM_WIDTH), f32),
        compiler_params=_cparams("arbitrary", "arbitrary"),
        name="memattn",
    )(q3, mkt4, mvt4)


def _merge_body(x_ref, mla_ref, gla_ref, mem_ref, gates_ref, ggla_ref, wout_ref, y_ref):
    lane = lax.broadcasted_iota(jnp.int32, (1, LANES), 1)
    ggla = ggla_ref[...]
    y = x_ref[...]
    m1 = (gates_ref[:, 0:MLA_WIDTH] * mla_ref[...]).astype(bf16)
    y = y + _mm(m1, wout_ref[0:MLA_WIDTH, :])
    for j in range(GLA_WIDTH // LANES):
        ls = slice(j * LANES, (j + 1) * LANES)
        gn = _half_head_rms(gla_ref[:, ls], ggla, lane)
        gs = slice(MLA_WIDTH + j * LANES, MLA_WIDTH + (j + 1) * LANES)
        y = y + _mm((gates_ref[:, gs] * gn).astype(bf16), wout_ref[gs, :])
    ms = slice(MLA_WIDTH + GLA_WIDTH, D_MIX)
    m3 = (gates_ref[:, ms] * mem_ref[...]).astype(bf16)
    y_ref[...] = y + _mm(m3, wout_ref[ms, :])


def _merge(x2, mla_o, gla_o, mem_o, gates, w, tm):
    T = x2.shape[0]
    row = lambda i: (i, 0)
    fix = lambda i: (0, 0)
    return pl.pallas_call(
        _merge_body,
        grid=(T // tm,),
        in_specs=[pl.BlockSpec((tm, D_MODEL), row), pl.BlockSpec((tm, MLA_WIDTH), row),
                  pl.BlockSpec((tm, GLA_WIDTH), row), pl.BlockSpec((tm, MEM_WIDTH), row),
                  pl.BlockSpec((tm, D_MIX), row), pl.BlockSpec(w['g_gla_o'].shape, fix),
                  pl.BlockSpec(w['w_out'].shape, fix)],
        out_specs=pl.BlockSpec((tm, D_MODEL), row),
        out_shape=jax.ShapeDtypeStruct((T, D_MODEL), f32),
        compiler_params=_cparams("arbitrary"),
        name="merge",
    )(x2, mla_o, gla_o, mem_o, gates, w['g_gla_o'], w['w_out'])


def _pad_head(a):
    return jnp.pad(a, [(0, 0)] * (a.ndim - 1) + [(0, HEAD_PAD - QK_DIM)])


def _prep_layer(l, p):
    w_in = p['w_in'][l]
    cuts, o = [], 0
    for n in (Q_RANK, KV_RANK, ROPE_DIM, MLA_WIDTH, GLA_KDIM, GLA_KDIM, GLA_WIDTH, GLA_GATE_RANK, GLA_WIDTH,
              MEM_WIDTH, MEM_WIDTH):
        cuts.append(w_in[:, o:o + n])
        o += n
    cq, ckv, kpe, gate_mla, gq, gk, gv, gg, gate_gla, mq, gate_mem = cuts
    z = lambda n: jnp.zeros((D_MODEL, n), f32)
    misc = jnp.concatenate([gg, z(KPE_LANE - GLA_GATE_RANK), kpe, z(LANES - KPE_LANE - ROPE_DIM)], axis=1)
    w_in_packed = jnp.concatenate([cq, ckv, gate_mla, gq, gk, gv, gate_gla, mq, gate_mem, misc], axis=1)

    w_uk = p['w_uk'][l]
    g_k = p['g_mla_k'][l]
    wuk_packed = jnp.pad(w_uk, ((0, 0), (0, 0), (0, HEAD_PAD - NOPE_DIM))).reshape(KV_RANK, QK_PACKED)
    w_uk_t = w_uk.reshape(KV_RANK, MLA_HEADS * NOPE_DIM).T
    w_uk_g = jnp.pad((w_uk * g_k[None, None, :NOPE_DIM]).transpose(1, 2, 0),
                     ((0, 0), (0, HEAD_PAD - NOPE_DIM), (0, 0)))
    w_uv = p['w_uv'][l].transpose(1, 0, 2)
    zeros = jnp.zeros_like(w_uv)
    even = jnp.arange(MLA_HEADS)[:, None, None] % 2 == 0
    w_uv_pairs = jnp.concatenate([jnp.where(even, w_uv, zeros), jnp.where(even, zeros, w_uv)], axis=-1)
    w_gk = jnp.pad(p['w_gk'][l], ((0, LANES - GLA_GATE_RANK), (0, 0)))
    row = lambda a: a.reshape(1, -1)
    return {
        'g_pre': row(p['g_pre'][l]), 'w_in': w_in_packed.astype(bf16), 'g_qa': row(p['g_qa'][l]),
        'w_uq': _pad_head(p['w_uq'][l]).reshape(Q_RANK, QK_PACKED).astype(bf16),
        'g_kva': row(p['g_kva'][l]), 'w_uk': wuk_packed.astype(bf16),
        'g_q': row(_pad_head(p['g_mla_q'][l])), 'g_k': row(_pad_head(g_k)),
        'w_gk': w_gk.astype(bf16), 'b_gk': row(p['b_gk'][l]),
        'g_mem_q': row(jnp.tile(p['g_mem_q'][l], 2)), 'g_mem_k': row(jnp.tile(p['g_mem_k'][l], 2)),
        'g_gla_o': row(jnp.tile(p['g_gla_o'][l], 2)),
        'w_uk_t': w_uk_t.astype(bf16), 'w_uk_g': w_uk_g.astype(bf16),
        'g_k_rope': g_k[NOPE_DIM:].reshape(ROPE_DIM, 1),
        'w_uv_pairs': w_uv_pairs.astype(bf16),
        'w_uv_t': w_uv.transpose(0, 2, 1).astype(bf16),
        'g_mem': row(p['g_mem'][l]), 'w_mem_kv': p['w_mem_kv'][l].astype(bf16),
        'w_out': p['w_out'][l].astype(bf16),
    }


def _angles(pos):
    inv_freq = ROPE_THETA ** (-jnp.arange(HALF_ROPE, dtype=f32) * (2.0 / ROPE_DIM))
    return pos.astype(f32)[:, None] * inv_freq[None, :]


def _packed_tables(pos):
    ang = _angles(pos)
    n = pos.shape[0]
    cos, sin = jnp.cos(ang), jnp.sin(ang)
    one = jnp.ones((n, NOPE_DIM), f32)
    z = lambda w: jnp.zeros((n, w), f32)
    tail = HEAD_PAD - QK_DIM
    cos_p = jnp.concatenate([one, cos, cos, z(tail)], axis=1)
    sin_up = jnp.concatenate([z(NOPE_DIM + HALF_ROPE), sin, z(tail)], axis=1)
    sin_dn = jnp.concatenate([z(NOPE_DIM), -sin, z(HALF_ROPE + tail)], axis=1)
    return cos_p, sin_up, sin_dn


def _transposed_tables(pos):
    ang = _angles(pos).T
    return jnp.cos(ang), jnp.sin(ang)


def _state_to_t(s):
    B = s.shape[0]
    st = s.transpose(0, 1, 3, 2)
    eye = jnp.eye(GLA_HEADS, dtype=s.dtype)
    return jnp.einsum('bhvk,hg->bhvgk', st, eye).reshape(B, GLA_WIDTH, GLA_KDIM)


def _t_to_state(st):
    B = st.shape[0]
    s5 = st.reshape(B, GLA_HEADS, GLA_DV, GLA_HEADS, GLA_DK)
    diag = jnp.stack([s5[:, h, :, h, :] for h in range(GLA_HEADS)], axis=1)
    return diag.transpose(0, 1, 3, 2)


def kernel(x_prompt, x_sample, mem_prompt, cache_ckv, cache_kpe, page_table, state_gla, cache_mem_k, cache_mem_v, g_pre, w_in, g_qa, w_uq, g_kva, w_uk, w_uv, g_mla_q, g_mla_k, w_gk, b_gk, g_gla_o, g_mem, w_mem_kv, g_mem_q, g_mem_k, w_out):
    params = dict(g_pre=g_pre, w_in=w_in, g_qa=g_qa, w_uq=w_uq, g_kva=g_kva, w_uk=w_uk, w_uv=w_uv,
                  g_mla_q=g_mla_q, g_mla_k=g_mla_k, w_gk=w_gk, b_gk=b_gk, g_gla_o=g_gla_o, g_mem=g_mem,
                  w_mem_kv=w_mem_kv, g_mem_q=g_mem_q, g_mem_k=g_mem_k, w_out=w_out)
    nb, seq, _ = x_prompt.shape
    ndb, t_new, _ = x_sample.shape
    n_mem = mem_prompt.shape[1]
    depth = w_in.shape[0]
    n_pages = page_table.shape[1]
    past_len = n_pages * cache_ckv.shape[2]

    tm_p = min(256, seq)
    tm_merge = min(512, seq)
    tq_mem = min(1024, seq)
    tq = min(256, seq)
    tm_s = min(256, ndb * t_new)
    pp = max(d for d in range(1, 17) if (n_pages + 1) % d == 0)
    gla_tt = min(128, seq)
    gla_cs = min(16, gla_tt)
    nb_s = min(8, ndb)
    nbp = min(4, nb)
    nb_gla = min(8, nb)

    cache_kpe_t = cache_kpe.transpose(0, 1, 3, 2)
    to_t = lambda a: a.transpose(0, 1, 3, 4, 2).reshape(depth, ndb, MEM_WIDTH, n_mem)
    cache_mkt, cache_mvt = to_t(cache_mem_k), to_t(cache_mem_v)
    from_t = lambda a: a.reshape(nb, MEM_HEADS, MEM_HEAD_DIM, n_mem).transpose(0, 3, 1, 2)

    tabs_p = _packed_tables(jnp.arange(seq, dtype=jnp.int32))
    pos_s = past_len + jnp.arange(t_new, dtype=jnp.int32)
    tabs_s = tuple(jnp.tile(t, (tm_s // t_new, 1)) for t in _packed_tables(pos_s))
    tabs_t = _transposed_tables(jnp.arange(past_len + PAGE_SIZE, dtype=jnp.int32))

    xp = x_prompt.reshape(nb * seq, D_MODEL)
    xs = x_sample.reshape(ndb * t_new, D_MODEL)
    mem2 = mem_prompt.reshape(nb * n_mem, D_MODEL)
    ckv_p, kpe_p, gla_p, mk_p, mv_p, ckv_s, kpe_s, gla_s = ([] for _ in range(8))
    for l in range(depth):
        w = _prep_layer(l, params)
        q, k, c, kpe, gates, gq, gk, gv, gg, mq = _inproj(xp, tabs_p, w, tm_p)
        r3 = lambda a: a.reshape(nb, seq, a.shape[-1])
        mla_o = _mla_prompt(r3(q), r3(k), r3(c), w['w_uv_t'], nb, seq, tq, nbp).reshape(nb * seq, MLA_WIDTH)
        gla_o, st = _gla(r3(gq), r3(gk), r3(gv), r3(gg), None, nb_gla, gla_tt, gla_cs)
        mkt, mvt = _memkv(mem2, w, nb, n_mem)
        mem_o = _memattn(0, r3(mq), mkt[None], mvt[None], 1, tq_mem)
        xp = _merge(xp, mla_o, gla_o.reshape(nb * seq, GLA_WIDTH), mem_o.reshape(nb * seq, MEM_WIDTH), gates, w, tm_merge)
        ckv_p.append(c.reshape(nb, seq, KV_RANK))
        kpe_p.append(kpe.reshape(nb, seq, ROPE_DIM))
        gla_p.append(_t_to_state(st))
        mk_p.append(from_t(mkt))
        mv_p.append(from_t(mvt))
        q, k, c, kpe, gates, gq, gk, gv, gg, mq = _inproj(xs, tabs_s, w, tm_s)
        mla_o = _mla_sample(l, q, c, kpe, cache_ckv, cache_kpe_t, page_table, w, tabs_t, pp)
        r3 = lambda a: a.reshape(ndb, t_new, a.shape[-1])
        gla_o, st = _gla_decode(r3(gq), r3(gk), r3(gv), r3(gg), state_gla[l].transpose(1, 2, 3, 0))
        mem_o = _memattn(l, r3(mq), cache_mkt, cache_mvt, nb_s, t_new)
        xs = _merge(xs, mla_o, gla_o.reshape(ndb * t_new, GLA_WIDTH), mem_o.reshape(ndb * t_new, MEM_WIDTH), gates, w, tm_s)
        ckv_s.append(c.reshape(ndb, t_new, KV_RANK))
        kpe_s.append(kpe.reshape(ndb, t_new, ROPE_DIM))
        gla_s.append(st.transpose(3, 0, 1, 2))
    return (xp.reshape(nb, seq, D_MODEL), xs.reshape(ndb, t_new, D_MODEL), jnp.stack(ckv_p), jnp.stack(kpe_p),
            jnp.stack(gla_p), jnp.stack(mk_p), jnp.stack(mv_p), jnp.stack(ckv_s), jnp.stack(kpe_s), jnp.stack(gla_s))
```

```python
import functools

import jax
import jax.numpy as jnp
from jax import lax
from jax.experimental import pallas as pl
from jax.experimental.pallas import tpu as pltpu

f32, bf16 = jnp.float32, jnp.bfloat16

D_MODEL = 1024
PAGE_SIZE = 128
MLA_HEADS = 8
NOPE_DIM = 64
ROPE_DIM = 32
HALF_ROPE = ROPE_DIM // 2
QK_DIM = NOPE_DIM + ROPE_DIM
V_DIM = 64
Q_RANK = 384
KV_RANK = 256
MLA_WIDTH = MLA_HEADS * V_DIM
ROPE_THETA = 10000.0
MLA_SCALE = QK_DIM ** -0.5
LOG2_E = 1.4426950408889634
GLA_HEADS = 4
GLA_DK = 32
GLA_DV = 64
GLA_KDIM = GLA_HEADS * GLA_DK
GLA_WIDTH = GLA_HEADS * GLA_DV
GLA_GATE_RANK = 16
GLA_NORMALIZER = 16.0
MEM_HEADS = 4
MEM_HEAD_DIM = 64
MEM_WIDTH = MEM_HEADS * MEM_HEAD_DIM
MEM_SCALE = MEM_HEAD_DIM ** -0.5
D_MIX = MLA_WIDTH + GLA_WIDTH + MEM_WIDTH
EPS = 1e-6
NEG_INF = -1e30

LANES = 128
HEAD_PAD = LANES
QK_PACKED = MLA_HEADS * HEAD_PAD

OFF_CQ = 0
OFF_CKV = OFF_CQ + Q_RANK
OFF_GATE_MLA = OFF_CKV + KV_RANK
OFF_GQ = OFF_GATE_MLA + MLA_WIDTH
OFF_GK = OFF_GQ + GLA_KDIM
OFF_GV = OFF_GK + GLA_KDIM
OFF_GATE_GLA = OFF_GV + GLA_WIDTH
OFF_MQ = OFF_GATE_GLA + GLA_WIDTH
OFF_GATE_MEM = OFF_MQ + MEM_WIDTH
OFF_MISC = OFF_GATE_MEM + MEM_WIDTH
D_IN_PACKED = OFF_MISC + LANES
KPE_LANE = NOPE_DIM

VMEM_LIMIT = 56 * 1024 * 1024
PAGE_BUFFERS = 3


def _cparams(*sem):
    return pltpu.CompilerParams(dimension_semantics=sem, vmem_limit_bytes=VMEM_LIMIT)


def _nt(a, b):
    return lax.dot_general(a, b, (((1,), (1,)), ((), ())), preferred_element_type=f32)


def _tn(a, b):
    return lax.dot_general(a, b, (((0,), (0,)), ((), ())), preferred_element_type=f32)


def _mm(a, b):
    return jnp.dot(a, b, preferred_element_type=f32)


def _rms_rows(x, g):
    return x * lax.rsqrt(jnp.mean(x * x, axis=-1, keepdims=True) + EPS) * g


def _half_head_rms(blk, g, lane):
    sq = blk * blk
    lo = lane < 64
    ss_lo = jnp.sum(jnp.where(lo, sq, 0.0), axis=-1, keepdims=True)
    ss_hi = jnp.sum(jnp.where(lo, 0.0, sq), axis=-1, keepdims=True)
    ss = jnp.where(lo, ss_lo, ss_hi)
    return blk * lax.rsqrt(ss * (1.0 / 64.0) + EPS) * g


def _rope_packed(x, cos, sin_up, sin_dn):
    return x * cos + pltpu.roll(x, HALF_ROPE, 1) * sin_up + pltpu.roll(x, LANES - HALF_ROPE, 1) * sin_dn


def _inproj_body(x_ref, gpre_ref, win_ref, gqa_ref, wuq_ref, gkva_ref, wuk_ref, gq_ref, gk_ref,
                 cos_ref, sup_ref, sdn_ref, wgk_ref, bgk_ref, gmq_ref,
                 q_out, k_out, ckv_out, kpe_out, gates_out, glaq_out, glak_out, glav_out, glag_out, memq_out):
    x = x_ref[...]
    xb = _rms_rows(x, gpre_ref[...]).astype(bf16)

    def seg(off, n):
        return _mm(xb, win_ref[:, off:off + n])

    cos, sup, sdn = cos_ref[...], sup_ref[...], sdn_ref[...]
    lane = lax.broadcasted_iota(jnp.int32, (1, LANES), 1)

    cq = _rms_rows(seg(OFF_CQ, Q_RANK), gqa_ref[...]).astype(bf16)
    qf = _mm(cq, wuq_ref[...])
    gq = gq_ref[...]
    for h in range(MLA_HEADS):
        qh = qf[:, h * HEAD_PAD:(h + 1) * HEAD_PAD]
        ss = jnp.sum(qh * qh, axis=-1, keepdims=True)
        qn = qh * lax.rsqrt(ss * (1.0 / QK_DIM) + EPS) * gq
        q_out[:, h * HEAD_PAD:(h + 1) * HEAD_PAD] = (_rope_packed(qn, cos, sup, sdn) * (MLA_SCALE * LOG2_E)).astype(bf16)

    c = _rms_rows(seg(OFF_CKV, KV_RANK), gkva_ref[...])
    ckv_out[...] = c
    misc = seg(OFF_MISC, LANES)
    kpe_wide = jnp.where((lane >= KPE_LANE) & (lane < KPE_LANE + ROPE_DIM), misc, 0.0)
    kpe_out[...] = misc[:, KPE_LANE:KPE_LANE + ROPE_DIM]
    kf = _mm(c.astype(bf16), wuk_ref[...])
    gk = gk_ref[...]
    for h in range(MLA_HEADS):
        kh = kf[:, h * HEAD_PAD:(h + 1) * HEAD_PAD] + kpe_wide
        ss = jnp.sum(kh * kh, axis=-1, keepdims=True)
        kn = kh * lax.rsqrt(ss * (1.0 / QK_DIM) + EPS) * gk
        k_out[:, h * HEAD_PAD:(h + 1) * HEAD_PAD] = _rope_packed(kn, cos, sup, sdn).astype(bf16)

    g1 = seg(OFF_GATE_MLA, MLA_WIDTH)
    gates_out[:, 0:MLA_WIDTH] = g1 * jax.nn.sigmoid(g1)
    g2 = seg(OFF_GATE_GLA, GLA_WIDTH)
    gates_out[:, MLA_WIDTH:MLA_WIDTH + GLA_WIDTH] = g2 * jax.nn.sigmoid(g2)
    g3 = seg(OFF_GATE_MEM, MEM_WIDTH)
    gates_out[:, MLA_WIDTH + GLA_WIDTH:D_MIX] = g3 * jax.nn.sigmoid(g3)

    glaq_out[...] = seg(OFF_GQ, GLA_KDIM) * (GLA_DK ** -0.5)
    glak_out[...] = seg(OFF_GK, GLA_KDIM)
    glav_out[...] = seg(OFF_GV, GLA_WIDTH)
    gl = _mm(misc.astype(bf16), wgk_ref[...]) + bgk_ref[...]
    glag_out[...] = jax.nn.log_sigmoid(gl) * (LOG2_E / GLA_NORMALIZER)

    mq = seg(OFF_MQ, MEM_WIDTH)
    gmq = gmq_ref[...]
    for j in range(MEM_WIDTH // LANES):
        blk = _half_head_rms(mq[:, j * LANES:(j + 1) * LANES], gmq, lane)
        memq_out[:, j * LANES:(j + 1) * LANES] = (blk * (MEM_SCALE * LOG2_E)).astype(bf16)


def _inproj(x2, tabs, w, tm):
    T = x2.shape[0]
    nt = tabs[0].shape[0] // tm
    row = lambda i: (i, 0)
    fix = lambda i: (0, 0)
    tab = lambda i: (i % nt, 0)

    def full(a):
        return pl.BlockSpec(a.shape, fix)

    consts1 = (w['g_pre'], w['w_in'], w['g_qa'], w['w_uq'], w['g_kva'], w['w_uk'], w['g_q'], w['g_k'])
    consts2 = (w['w_gk'], w['b_gk'], w['g_mem_q'])
    outs = [(QK_PACKED, bf16), (QK_PACKED, bf16), (KV_RANK, f32), (ROPE_DIM, f32), (D_MIX, f32),
            (GLA_KDIM, f32), (GLA_KDIM, f32), (GLA_WIDTH, f32), (GLA_KDIM, f32), (MEM_WIDTH, bf16)]
    return pl.pallas_call(
        _inproj_body,
        grid=(T // tm,),
        in_specs=[pl.BlockSpec((tm, D_MODEL), row)] + [full(a) for a in consts1]
        + [pl.BlockSpec((tm, LANES), tab)] * 3 + [full(a) for a in consts2],
        out_specs=[pl.BlockSpec((tm, n), row) for n, _ in outs],
        out_shape=[jax.ShapeDtypeStruct((T, n), dt) for n, dt in outs],
        compiler_params=_cparams("arbitrary"),
        name="inproj",
    )(x2, *consts1, *tabs, *consts2)


def _mla_prompt_body(q_ref, k_ref, c_ref, wuvt_ref, o_ref, ct_scr, acc_scr, m_scr, l_scr, ot_scr, *, tq, nbp):
    i = pl.program_id(1)
    nblk = ct_scr.shape[1]

    @pl.when(i == 0)
    def _():
        for bb in range(nbp):
            for j in range(nblk):
                ct_scr[bb, j] = c_ref[bb, j * tq:(j + 1) * tq, :].T.astype(bf16)

    m_scr[...] = jnp.full(m_scr.shape, NEG_INF, f32)
    l_scr[...] = jnp.zeros(l_scr.shape, f32)
    acc_scr[...] = jnp.zeros(acc_scr.shape, f32)
    key = lax.broadcasted_iota(jnp.int32, (tq, tq), 0)
    qry = lax.broadcasted_iota(jnp.int32, (tq, tq), 1)
    causal = key <= qry

    def step(kb, masked):
        ks = pl.ds(pl.multiple_of(kb * tq, tq), tq)
        for h in range(MLA_HEADS):
            hs = slice(h * HEAD_PAD, (h + 1) * HEAD_PAD)
            for bb in range(nbp):
                s = _nt(k_ref[bb, ks, hs], q_ref[bb, :, hs])
                if masked:
                    s = jnp.where(causal, s, NEG_INF)
                m_old = m_scr[bb, h:h + 1, :]
                m_new = jnp.maximum(m_old, jnp.max(s, axis=0, keepdims=True))
                alpha = jnp.exp2(m_old - m_new)
                p = jnp.exp2(s - m_new)
                l_scr[bb, h:h + 1, :] = l_scr[bb, h:h + 1, :] * alpha + jnp.sum(p, axis=0, keepdims=True)
                acc_scr[bb, h] = acc_scr[bb, h] * alpha + _mm(ct_scr[bb, kb], p.astype(bf16))
                m_scr[bb, h:h + 1, :] = m_new

    def body(kb, carry):
        step(kb, False)
        return carry

    lax.fori_loop(0, i, body, 0)
    step(i, True)

    for bb in range(nbp):
        for h in range(MLA_HEADS):
            lat_t = (acc_scr[bb, h] * (1.0 / l_scr[bb, h:h + 1, :])).astype(bf16)
            ot_scr[h * V_DIM:(h + 1) * V_DIM, :] = _mm(wuvt_ref[h], lat_t)
        o_ref[bb] = ot_scr[...].T


def _mla_prompt(q, k, c, wuv_t, nb, seq, tq, nbp):
    nq = seq // tq
    return pl.pallas_call(
        functools.partial(_mla_prompt_body, tq=tq, nbp=nbp),
        grid=(nb // nbp, nq),
        in_specs=[pl.BlockSpec((nbp, tq, QK_PACKED), lambda b, i: (b, i, 0)),
                  pl.BlockSpec((nbp, seq, QK_PACKED), lambda b, i: (b, 0, 0), pipeline_mode=pl.Buffered(1)),
                  pl.BlockSpec((nbp, seq, KV_RANK), lambda b, i: (b, 0, 0), pipeline_mode=pl.Buffered(1)),
                  pl.BlockSpec(wuv_t.shape, lambda b, i: (0, 0, 0))],
        out_specs=pl.BlockSpec((nbp, tq, MLA_WIDTH), lambda b, i: (b, i, 0)),
        out_shape=jax.ShapeDtypeStruct((nb, seq, MLA_WIDTH), f32),
        scratch_shapes=[pltpu.VMEM((nbp, nq, KV_RANK, tq), bf16), pltpu.VMEM((nbp, MLA_HEADS, KV_RANK, tq), f32),
                        pltpu.VMEM((nbp, MLA_HEADS, tq), f32), pltpu.VMEM((nbp, MLA_HEADS, tq), f32),
                        pltpu.VMEM((MLA_WIDTH, tq), f32)],
        compiler_params=_cparams("arbitrary", "arbitrary"),
        name="mla_prompt",
    )(q, k, c, wuv_t)


def _mla_sample_body(pt_ref, q_ref, cnew_ref, kpenewt_ref, wukt_ref, wukg_ref, gkr_ref, cost_ref, sint_ref, wuv_ref,
                     ckv_hbm, kpe_hbm, o_ref,
                     cbuf, kbuf, sem, lhs_scr, qr_scr, cb_scr, s_scr, ql_all, qr_all, acc_scr, m_scr, l_scr,
                     *, layer, gp, n_pages, t_new):
    ndb = ql_all.shape[0]
    ng = (n_pages + 1) // gp
    kb = gp * PAGE_SIZE
    total = ndb * ng
    nrow = MLA_HEADS * t_new
    n_nope = MLA_HEADS * NOPE_DIM
    new_rows = pl.ds(kb - PAGE_SIZE, t_new)
    new_lanes = pl.ds(kb - PAGE_SIZE, PAGE_SIZE)

    def group_copies(t):
        b, g, slot = t // ng, t % ng, t % PAGE_BUFFERS
        cps = []
        for i in range(gp):
            page = pt_ref[b * n_pages + jnp.minimum(g * gp + i, n_pages - 1)]
            lanes = pl.ds(i * PAGE_SIZE, PAGE_SIZE)
            cps.append(pltpu.make_async_copy(ckv_hbm.at[layer, page], cbuf.at[slot, lanes, :], sem.at[0, slot]))
            cps.append(pltpu.make_async_copy(kpe_hbm.at[layer, page], kbuf.at[slot, :, lanes], sem.at[1, slot]))
        return cps

    def reset_state():
        m_scr[...] = jnp.full((nrow, 1), NEG_INF, f32)
        l_scr[...] = jnp.zeros((nrow, 1), f32)
        acc_scr[...] = jnp.zeros((nrow, KV_RANK), f32)

    gkr = gkr_ref[...]
    key_lane = lax.broadcasted_iota(jnp.int32, (nrow, PAGE_SIZE), 1)
    qry_row = lax.broadcasted_iota(jnp.int32, (nrow, PAGE_SIZE), 0) % t_new

    def prepare_group(t):
        b, g, pslot = t // ng, t % ng, t % PAGE_BUFFERS
        lhs_scr[n_nope:, :] = ql_all[b].astype(bf16)
        qr_scr[...] = qr_all[b].astype(bf16)
        for cp in group_copies(t):
            cp.wait()
        is_last = g == ng - 1
        cbuf[pslot, new_rows, :] = jnp.where(is_last, cnew_ref[b], cbuf[pslot, new_rows, :])
        kbuf[pslot, :, new_lanes] = jnp.where(is_last, kpenewt_ref[b], kbuf[pslot, :, new_lanes])

    def score_group(t, slot):
        g, pslot = t % ng, t % PAGE_BUFFERS
        is_last = g == ng - 1
        cb = cbuf[pslot].astype(bf16)
        cb_scr[slot] = cb
        kt = kbuf[pslot]
        cos_t, sin_t = cost_ref[g], sint_ref[g]
        big = _nt(lhs_scr[...], cb)
        kn = big[0:n_nope]
        ss_nope = jnp.sum((kn * kn).reshape(MLA_HEADS, NOPE_DIM, kb), axis=1)
        ss = ss_nope + jnp.sum(kt * kt, axis=0, keepdims=True)
        r = lax.rsqrt(ss * (1.0 / QK_DIM) + EPS)
        kg = kt * gkr
        k1, k2 = kg[0:HALF_ROPE], kg[HALF_ROPE:ROPE_DIM]
        kr = jnp.concatenate([k1 * cos_t - k2 * sin_t, k2 * cos_t + k1 * sin_t], axis=0).astype(bf16)
        s = big[n_nope:] + _mm(qr_scr[...], kr)
        s = (s.reshape(MLA_HEADS, t_new, kb) * r[:, None, :]).reshape(nrow, kb)
        first_dead = jnp.where(is_last, 1, PAGE_SIZE + t_new)
        tail = jnp.where(key_lane >= qry_row + first_dead, NEG_INF, s[:, kb - PAGE_SIZE:])
        s_scr[slot] = jnp.concatenate([s[:, :kb - PAGE_SIZE], tail], axis=1)

    def value_group(slot):
        s = s_scr[slot]
        m_old = m_scr[...]
        m_new = jnp.maximum(m_old, jnp.max(s, axis=-1, keepdims=True))
        alpha = jnp.exp2(m_old - m_new)
        p = jnp.exp2(s - m_new)
        l_scr[...] = l_scr[...] * alpha + jnp.sum(p, axis=-1, keepdims=True)
        acc_scr[...] = acc_scr[...] * alpha + _mm(p.astype(bf16), cb_scr[slot])
        m_scr[...] = m_new

    def finish_batch(b, closed):
        ql_all[b] = jnp.where(closed, acc_scr[...] * (1.0 / l_scr[...]), ql_all[b])
        m_scr[...] = jnp.where(closed, NEG_INF, m_scr[...])
        l_scr[...] = jnp.where(closed, 0.0, l_scr[...])
        acc_scr[...] = jnp.where(closed, 0.0, acc_scr[...])

    lhs_scr[0:n_nope, :] = wukt_ref[...]
    for h in range(MLA_HEADS):
        q_h = q_ref[:, h * HEAD_PAD:(h + 1) * HEAD_PAD]
        rows = slice(h * t_new, (h + 1) * t_new)
        ql_all[:, rows, :] = _mm(q_h, wukg_ref[h]).reshape(ndb, t_new, KV_RANK)
        qr_all[:, rows, :] = q_h[:, KPE_LANE:KPE_LANE + ROPE_DIM].astype(f32).reshape(ndb, t_new, ROPE_DIM)
    reset_state()

    ahead = PAGE_BUFFERS - 1

    def skewed_step(t, slot, prefetch):
        if prefetch:
            for cp in group_copies(t + ahead):
                cp.start()
        prepare_group(t)
        score_group(t, slot)
        value_group(1 - slot)
        closes = t % ng == 0
        finish_batch(jnp.maximum(t // ng - 1, 0), closes)

    for t in range(min(ahead, total)):
        for cp in group_copies(t):
            cp.start()
    if total > ahead:
        for cp in group_copies(ahead):
            cp.start()
    prepare_group(0)
    score_group(0, 0)

    n_pairs = max(total - 1 - ahead, 0) // 2

    def body(k, carry):
        skewed_step(2 * k + 1, 1, True)
        skewed_step(2 * k + 2, 0, True)
        return carry

    lax.fori_loop(0, n_pairs, body, 0)
    for t in range(2 * n_pairs + 1, total):
        skewed_step(t, t % 2, t + ahead < total)
    value_group((total - 1) % 2)
    finish_batch(ndb - 1, True)

    for h in range(MLA_HEADS):
        lat_h = ql_all[:, h * t_new:(h + 1) * t_new, :].reshape(ndb * t_new, KV_RANK).astype(bf16)
        contrib = _mm(lat_h, wuv_ref[h])
        ps = slice((h // 2) * LANES, (h // 2 + 1) * LANES)
        if h % 2 == 0:
            o_ref[:, ps] = contrib
        else:
            o_ref[:, ps] += contrib


def _mla_sample(layer, q, c_new, kpe_new, cache_ckv, cache_kpe_t, page_table, w, tabs_t, gp):
    ndb, n_pages = page_table.shape
    t_new = q.shape[0] // ndb
    ng = (n_pages + 1) // gp
    kb = gp * PAGE_SIZE
    nrow = MLA_HEADS * t_new
    cost, sint = tabs_t
    by_group = lambda a: a.reshape(HALF_ROPE, ng, kb).transpose(1, 0, 2)
    kpe_new_t = jnp.pad(kpe_new.reshape(ndb, t_new, ROPE_DIM).transpose(0, 2, 1),
                        ((0, 0), (0, 0), (0, PAGE_SIZE - t_new)))
    args = (q, c_new.reshape(ndb, t_new, KV_RANK), kpe_new_t, w['w_uk_t'], w['w_uk_g'], w['g_k_rope'],
            by_group(cost), by_group(sint), w['w_uv_pairs'])

    def full(a):
        return pl.BlockSpec(a.shape, lambda i, pt, nd=a.ndim: (0,) * nd)

    return pl.pallas_call(
        functools.partial(_mla_sample_body, layer=layer, gp=gp, n_pages=n_pages, t_new=t_new),
        grid_spec=pltpu.PrefetchScalarGridSpec(
            num_scalar_prefetch=1,
            grid=(1,),
            in_specs=[full(a) for a in args] + [pl.BlockSpec(memory_space=pl.ANY)] * 2,
            out_specs=pl.BlockSpec((ndb * t_new, MLA_WIDTH), lambda i, pt: (0, 0)),
            scratch_shapes=[pltpu.VMEM((PAGE_BUFFERS, kb, KV_RANK), f32),
                            pltpu.VMEM((PAGE_BUFFERS, ROPE_DIM, kb), f32),
                            pltpu.SemaphoreType.DMA((2, PAGE_BUFFERS)),
                            pltpu.VMEM((MLA_HEADS * NOPE_DIM + nrow, KV_RANK), bf16),
                            pltpu.VMEM((nrow, ROPE_DIM), bf16),
                            pltpu.VMEM((2, kb, KV_RANK), bf16),
                            pltpu.VMEM((2, nrow, kb), f32),
                            pltpu.VMEM((ndb, nrow, KV_RANK), f32),
                            pltpu.VMEM((ndb, nrow, ROPE_DIM), f32),
                            pltpu.VMEM((nrow, KV_RANK), f32),
                            pltpu.VMEM((nrow, 1), f32),
                            pltpu.VMEM((nrow, 1), f32)]),
        out_shape=jax.ShapeDtypeStruct((ndb * t_new, MLA_WIDTH), f32),
        compiler_params=_cparams("arbitrary"),
        name="mla_sample",
    )(page_table.reshape(-1), *args, cache_ckv, cache_kpe_t)


def _split3(x):
    hi = x.astype(bf16)
    r1 = x - hi.astype(f32)
    mid = r1.astype(bf16)
    lo = (r1 - mid.astype(f32)).astype(bf16)
    return hi, mid, lo


def _gla_body(*refs, nb, tt, cs, has_init):
    if has_init:
        q_ref, k_ref, v_ref, g_ref, s0_ref, o_ref, sT_out, st_scr = refs
    else:
        q_ref, k_ref, v_ref, g_ref, o_ref, sT_out, st_scr = refs
    ti = pl.program_id(1)
    nchunk = tt // cs

    @pl.when(ti == 0)
    def _():
        if has_init:
            st_scr[...] = s0_ref[...]
        else:
            st_scr[...] = jnp.zeros(st_scr.shape, f32)

    tri = (lax.broadcasted_iota(jnp.int32, (tt, tt), 1) <= lax.broadcasted_iota(jnp.int32, (tt, tt), 0)).astype(bf16)
    kv_head = (lax.broadcasted_iota(jnp.int32, (GLA_KDIM, GLA_WIDTH), 0) // GLA_DK
               == lax.broadcasted_iota(jnp.int32, (GLA_KDIM, GLA_WIDTH), 1) // GLA_DV)
    block_ones = kv_head.astype(bf16)
    vk_head = (lax.broadcasted_iota(jnp.int32, (GLA_WIDTH, GLA_KDIM), 0) // GLA_DV
               == lax.broadcasted_iota(jnp.int32, (GLA_WIDTH, GLA_KDIM), 1) // GLA_DK)
    sel_t = (lax.broadcasted_iota(jnp.int32, (cs, cs * cs), 1) // cs
             == lax.broadcasted_iota(jnp.int32, (cs, cs * cs), 0)).astype(bf16)
    t_idx = lax.broadcasted_iota(jnp.int32, (cs, cs, GLA_KDIM), 0)
    s_idx = lax.broadcasted_iota(jnp.int32, (cs, cs, GLA_KDIM), 1)
    causal3 = s_idx <= t_idx

    cums = []
    for b in range(nb):
        hi, mid, lo = _split3(g_ref[b])
        cums.append(_mm(tri, hi) + _mm(tri, mid) + _mm(tri, lo))
    for n in range(nchunk):
        sl = slice(n * cs, (n + 1) * cs)
        for b in range(nb):
            cum = cums[b]
            base = cum[n * cs - 1:n * cs] if n > 0 else jnp.zeros((1, GLA_KDIM), f32)
            bc = cum[sl] - base
            b_last = bc[cs - 1:cs]
            qc, kc, vc = q_ref[b, sl, :], k_ref[b, sl, :], v_ref[b, sl, :]
            st = st_scr[b]
            o_inter = _nt((qc * jnp.exp2(bc)).astype(bf16), st.astype(bf16))
            diff = jnp.where(causal3, bc[:, None, :] - bc[None, :, :], NEG_INF)
            d3 = qc[:, None, :] * kc[None, :, :] * jnp.exp2(diff)
            a_exp = _mm(d3.reshape(cs * cs, GLA_KDIM).astype(bf16), block_ones)
            xv = (a_exp.reshape(cs, cs, GLA_WIDTH) * vc[None, :, :]).reshape(cs * cs, GLA_WIDTH)
            o_intra = _mm(sel_t, xv.astype(bf16))
            o_ref[b, sl, :] = o_inter + o_intra
            kd = (kc * jnp.exp2(b_last - bc)).astype(bf16)
            upd = _tn(vc.astype(bf16), kd)
            st_scr[b] = st * jnp.exp2(b_last) + jnp.where(vk_head, upd, 0.0)

    @pl.when(ti == pl.num_programs(1) - 1)
    def _():
        sT_out[...] = st_scr[...]


def _gla(q, k, v, g, s0t, nb, tt, cs):
    B, L, _ = q.shape
    has_init = s0t is not None
    tok = lambda bi, ti: (bi, ti, 0)
    st = lambda bi, ti: (bi, 0, 0)
    in_specs = [pl.BlockSpec((nb, tt, GLA_KDIM), tok), pl.BlockSpec((nb, tt, GLA_KDIM), tok),
                pl.BlockSpec((nb, tt, GLA_WIDTH), tok), pl.BlockSpec((nb, tt, GLA_KDIM), tok)]
    args = [q, k, v, g]
    if has_init:
        in_specs.append(pl.BlockSpec((nb, GLA_WIDTH, GLA_KDIM), st))
        args.append(s0t)
    return pl.pallas_call(
        functools.partial(_gla_body, nb=nb, tt=tt, cs=cs, has_init=has_init),
        grid=(B // nb, L // tt),
        in_specs=in_specs,
        out_specs=[pl.BlockSpec((nb, tt, GLA_WIDTH), tok), pl.BlockSpec((nb, GLA_WIDTH, GLA_KDIM), st)],
        out_shape=[jax.ShapeDtypeStruct((B, L, GLA_WIDTH), f32), jax.ShapeDtypeStruct((B, GLA_WIDTH, GLA_KDIM), f32)],
        scratch_shapes=[pltpu.VMEM((nb, GLA_WIDTH, GLA_KDIM), f32)],
        compiler_params=_cparams("arbitrary", "arbitrary"),
        name="gla",
    )(*args)


def _gla_decode_body(q_ref, k_ref, v_ref, g_ref, s0_ref, o_ref, s_out, qt_scr, kt_scr, et_scr, vt_scr, ot_scr):
    nb, t_new, _ = q_ref.shape
    for t in range(t_new):
        qt_scr[t] = q_ref[:, t, :].T
        kt_scr[t] = k_ref[:, t, :].T
        et_scr[t] = jnp.exp2(g_ref[:, t, :].T)
        vt_scr[t] = v_ref[:, t, :].T
    ot_scr[...] = jnp.zeros(ot_scr.shape, f32)

    for h in range(GLA_HEADS):
        vs = slice(h * GLA_DV, (h + 1) * GLA_DV)

        def one_k(kk, carry):
            row = pl.ds(h * GLA_DK + kk, 1)
            s = s0_ref[h, kk]
            for t in range(t_new):
                s = s * et_scr[t, row, :] + kt_scr[t, row, :] * vt_scr[t, vs, :]
                ot_scr[t, vs, :] += qt_scr[t, row, :] * s
            s_out[h, kk] = s
            return carry

        lax.fori_loop(0, GLA_DK, one_k, 0)

    for t in range(t_new):
        o_ref[:, t, :] = ot_scr[t].T


def _gla_decode(q, k, v, g, s0):
    B, T, _ = q.shape
    full = lambda a: pl.BlockSpec(a.shape, lambda i, nd=a.ndim: (0,) * nd)
    o_shape = jax.ShapeDtypeStruct((B, T, GLA_WIDTH), f32)
    s_shape = jax.ShapeDtypeStruct(s0.shape, f32)
    return pl.pallas_call(
        _gla_decode_body,
        grid=(1,),
        in_specs=[full(a) for a in (q, k, v, g, s0)],
        out_specs=[full(o_shape), full(s_shape)],
        out_shape=[o_shape, s_shape],
        scratch_shapes=[pltpu.VMEM((T, GLA_KDIM, B), f32)] * 3 + [pltpu.VMEM((T, GLA_WIDTH, B), f32)] * 2,
        compiler_params=_cparams("arbitrary"),
        name="gla_decode",
    )(q, k, v, g, s0)


def _memkv_body(mem_ref, gmem_ref, w_ref, gk_ref, mkt_out, mvt_out):
    xb = _rms_rows(mem_ref[...], gmem_ref[...]).astype(bf16)
    kv = _mm(xb, w_ref[...])
    lane = lax.broadcasted_iota(jnp.int32, (1, LANES), 1)
    gk = gk_ref[...]
    for j in range(MEM_WIDTH // LANES):
        blk = _half_head_rms(kv[:, j * LANES:(j + 1) * LANES], gk, lane)
        mkt_out[j * LANES:(j + 1) * LANES, :] = blk.T
        mvt_out[j * LANES:(j + 1) * LANES, :] = kv[:, MEM_WIDTH + j * LANES:MEM_WIDTH + (j + 1) * LANES].T


def _memkv(mem2, w, nb, n_mem):
    row = lambda i: (i, 0)
    fix = lambda i: (0, 0)
    return pl.pallas_call(
        _memkv_body,
        grid=(nb,),
        in_specs=[pl.BlockSpec((n_mem, D_MODEL), row), pl.BlockSpec(w['g_mem'].shape, fix),
                  pl.BlockSpec(w['w_mem_kv'].shape, fix), pl.BlockSpec(w['g_mem_k'].shape, fix)],
        out_specs=[pl.BlockSpec((None, MEM_WIDTH, n_mem), lambda i: (i, 0, 0))] * 2,
        out_shape=[jax.ShapeDtypeStruct((nb, MEM_WIDTH, n_mem), f32)] * 2,
        compiler_params=_cparams("arbitrary"),
        name="memkv",
    )(mem2, w['g_mem'], w['w_mem_kv'], w['g_mem_k'])


def _mem_attend(q, mkt, mvt):
    tq = q.shape[0]
    lane_head = lax.broadcasted_iota(jnp.int32, (1, MEM_WIDTH), 1) // MEM_HEAD_DIM
    qs = jnp.concatenate([jnp.where(lane_head == h, q, jnp.zeros_like(q)) for h in range(MEM_HEADS)], axis=0)
    s = _mm(qs, mkt.astype(bf16))
    p = jnp.exp2(s - jnp.max(s, axis=-1, keepdims=True))
    pv = _nt(p.astype(bf16), mvt.astype(bf16)) * (1.0 / jnp.sum(p, axis=-1, keepdims=True))
    o = jnp.zeros((tq, MEM_WIDTH), f32)
    for h in range(MEM_HEADS):
        o = o + jnp.where(lane_head == h, pv[h * tq:(h + 1) * tq], 0.0)
    return o


def _memattn_body(q_ref, mkt_ref, mvt_ref, o_ref, *, nb, tq):
    for b in range(nb):
        o_ref[b] = _mem_attend(q_ref[b], mkt_ref[b], mvt_ref[b])


def _memattn(layer, q3, mkt4, mvt4, nb, tq):
    B, L, _ = q3.shape
    n_mem = mkt4.shape[-1]
    kv_spec = pl.BlockSpec((None, nb, MEM_WIDTH, n_mem), lambda b, i: (layer, b, 0, 0))
    return pl.pallas_call(
        functools.partial(_memattn_body, nb=nb, tq=tq),
        grid=(B // nb, L // tq),
        in_specs=[pl.BlockSpec((nb, tq, MEM_WIDTH), lambda b, i: (b, i, 0)), kv_spec, kv_spec],
        out_specs=pl.BlockSpec((nb, tq, MEM_WIDTH), lambda b, i: (b, i, 0)),
        out_shape=jax.ShapeDtypeStruct((B, L, MEM_WIDTH), f32),
        compiler_params=_cparams("arbitrary", "arbitrary"),
        name="memattn",
    )(q3, mkt4, mvt4)


def _merge_tail(x_ref, mla_ref, gla_ref, mem_o, gates_ref, ggla_ref, wout_ref, y_ref):
    lane = lax.broadcasted_iota(jnp.int32, (1, LANES), 1)
    ggla = ggla_ref[...]
    y = x_ref[...]
    m1 = (gates_ref[:, 0:MLA_WIDTH] * mla_ref[...]).astype(bf16)
    y = y + _mm(m1, wout_ref[0:MLA_WIDTH, :])
    for j in range(GLA_WIDTH // LANES):
        ls = slice(j * LANES, (j + 1) * LANES)
        gn = _half_head_rms(gla_ref[:, ls], ggla, lane)
        gs = slice(MLA_WIDTH + j * LANES, MLA_WIDTH + (j + 1) * LANES)
        y = y + _mm((gates_ref[:, gs] * gn).astype(bf16), wout_ref[gs, :])
    ms = slice(MLA_WIDTH + GLA_WIDTH, D_MIX)
    m3 = (gates_ref[:, ms] * mem_o).astype(bf16)
    y_ref[...] = y + _mm(m3, wout_ref[ms, :])


def _merge_body(x_ref, mla_ref, gla_ref, mem_ref, gates_ref, ggla_ref, wout_ref, y_ref):
    _merge_tail(x_ref, mla_ref, gla_ref, mem_ref[...], gates_ref, ggla_ref, wout_ref, y_ref)


def _merge_memattn_body(x_ref, mla_ref, gla_ref, mq_ref, mkt_ref, mvt_ref, gates_ref, ggla_ref, wout_ref, y_ref):
    mem_o = _mem_attend(mq_ref[...], mkt_ref[...], mvt_ref[...])
    _merge_tail(x_ref, mla_ref, gla_ref, mem_o, gates_ref, ggla_ref, wout_ref, y_ref)


def _merge(x2, mla_o, gla_o, mem_o, gates, w, tm):
    T = x2.shape[0]
    row = lambda i: (i, 0)
    fix = lambda i: (0, 0)
    return pl.pallas_call(
        _merge_body,
        grid=(T // tm,),
        in_specs=[pl.BlockSpec((tm, D_MODEL), row), pl.BlockSpec((tm, MLA_WIDTH), row),
                  pl.BlockSpec((tm, GLA_WIDTH), row), pl.BlockSpec((tm, MEM_WIDTH), row),
                  pl.BlockSpec((tm, D_MIX), row), pl.BlockSpec(w['g_gla_o'].shape, fix),
                  pl.BlockSpec(w['w_out'].shape, fix)],
        out_specs=pl.BlockSpec((tm, D_MODEL), row),
        out_shape=jax.ShapeDtypeStruct((T, D_MODEL), f32),
        compiler_params=_cparams("arbitrary"),
        name="merge",
    )(x2, mla_o, gla_o, mem_o, gates, w['g_gla_o'], w['w_out'])


def _merge_memattn(x2, mla_o, gla_o, mq, mkt, mvt, gates, w, tm, seq):
    T = x2.shape[0]
    n_mem = mkt.shape[-1]
    per_batch = seq // tm
    row = lambda i: (i, 0)
    fix = lambda i: (0, 0)
    kv_spec = pl.BlockSpec((None, MEM_WIDTH, n_mem), lambda i: (i // per_batch, 0, 0))
    return pl.pallas_call(
        _merge_memattn_body,
        grid=(T // tm,),
        in_specs=[pl.BlockSpec((tm, D_MODEL), row), pl.BlockSpec((tm, MLA_WIDTH), row),
                  pl.BlockSpec((tm, GLA_WIDTH), row), pl.BlockSpec((tm, MEM_WIDTH), row), kv_spec, kv_spec,
                  pl.BlockSpec((tm, D_MIX), row), pl.BlockSpec(w['g_gla_o'].shape, fix),
                  pl.BlockSpec(w['w_out'].shape, fix)],
        out_specs=pl.BlockSpec((tm, D_MODEL), row),
        out_shape=jax.ShapeDtypeStruct((T, D_MODEL), f32),
        compiler_params=_cparams("arbitrary"),
        name="merge_memattn",
    )(x2, mla_o, gla_o, mq, mkt, mvt, gates, w['g_gla_o'], w['w_out'])


def _pad_head(a):
    return jnp.pad(a, [(0, 0)] * (a.ndim - 1) + [(0, HEAD_PAD - QK_DIM)])


def _prep_layer(l, p):
    w_in = p['w_in'][l]
    cuts, o = [], 0
    for n in (Q_RANK, KV_RANK, ROPE_DIM, MLA_WIDTH, GLA_KDIM, GLA_KDIM, GLA_WIDTH, GLA_GATE_RANK, GLA_WIDTH,
              MEM_WIDTH, MEM_WIDTH):
        cuts.append(w_in[:, o:o + n])
        o += n
    cq, ckv, kpe, gate_mla, gq, gk, gv, gg, gate_gla, mq, gate_mem = cuts
    z = lambda n: jnp.zeros((D_MODEL, n), f32)
    misc = jnp.concatenate([gg, z(KPE_LANE - GLA_GATE_RANK), kpe, z(LANES - KPE_LANE - ROPE_DIM)], axis=1)
    w_in_packed = jnp.concatenate([cq, ckv, gate_mla, gq, gk, gv, gate_gla, mq, gate_mem, misc], axis=1)

    w_uk = p['w_uk'][l]
    g_k = p['g_mla_k'][l]
    wuk_packed = jnp.pad(w_uk, ((0, 0), (0, 0), (0, HEAD_PAD - NOPE_DIM))).reshape(KV_RANK, QK_PACKED)
    w_uk_t = w_uk.reshape(KV_RANK, MLA_HEADS * NOPE_DIM).T
    w_uk_g = jnp.pad((w_uk * g_k[None, None, :NOPE_DIM]).transpose(1, 2, 0),
                     ((0, 0), (0, HEAD_PAD - NOPE_DIM), (0, 0)))
    w_uv = p['w_uv'][l].transpose(1, 0, 2)
    zeros = jnp.zeros_like(w_uv)
    even = jnp.arange(MLA_HEADS)[:, None, None] % 2 == 0
    w_uv_pairs = jnp.concatenate([jnp.where(even, w_uv, zeros), jnp.where(even, zeros, w_uv)], axis=-1)
    w_gk = jnp.pad(p['w_gk'][l], ((0, LANES - GLA_GATE_RANK), (0, 0)))
    row = lambda a: a.reshape(1, -1)
    return {
        'g_pre': row(p['g_pre'][l]), 'w_in': w_in_packed.astype(bf16), 'g_qa': row(p['g_qa'][l]),
        'w_uq': _pad_head(p['w_uq'][l]).reshape(Q_RANK, QK_PACKED).astype(bf16),
        'g_kva': row(p['g_kva'][l]), 'w_uk': wuk_packed.astype(bf16),
        'g_q': row(_pad_head(p['g_mla_q'][l])), 'g_k': row(_pad_head(g_k)),
        'w_gk': w_gk.astype(bf16), 'b_gk': row(p['b_gk'][l]),
        'g_mem_q': row(jnp.tile(p['g_mem_q'][l], 2)), 'g_mem_k': row(jnp.tile(p['g_mem_k'][l], 2)),
        'g_gla_o': row(jnp.tile(p['g_gla_o'][l], 2)),
        'w_uk_t': w_uk_t.astype(bf16), 'w_uk_g': w_uk_g.astype(bf16),
        'g_k_rope': g_k[NOPE_DIM:].reshape(ROPE_DIM, 1),
        'w_uv_pairs': w_uv_pairs.astype(bf16),
        'w_uv_t': w_uv.transpose(0, 2, 1).astype(bf16),
        'g_mem': row(p['g_mem'][l]), 'w_mem_kv': p['w_mem_kv'][l].astype(bf16),
        'w_out': p['w_out'][l].astype(bf16),
    }


def _angles(pos):
    inv_freq = ROPE_THETA ** (-jnp.arange(HALF_ROPE, dtype=f32) * (2.0 / ROPE_DIM))
    return pos.astype(f32)[:, None] * inv_freq[None, :]


def _packed_tables(pos):
    ang = _angles(pos)
    n = pos.shape[0]
    cos, sin = jnp.cos(ang), jnp.sin(ang)
    one = jnp.ones((n, NOPE_DIM), f32)
    z = lambda w: jnp.zeros((n, w), f32)
    tail = HEAD_PAD - QK_DIM
    cos_p = jnp.concatenate([one, cos, cos, z(tail)], axis=1)
    sin_up = jnp.concatenate([z(NOPE_DIM + HALF_ROPE), sin, z(tail)], axis=1)
    sin_dn = jnp.concatenate([z(NOPE_DIM), -sin, z(HALF_ROPE + tail)], axis=1)
    return cos_p, sin_up, sin_dn


def _transposed_tables(pos):
    ang = _angles(pos).T
    return jnp.cos(ang), jnp.sin(ang)


def _state_to_t(s):
    B = s.shape[0]
    st = s.transpose(0, 1, 3, 2)
    eye = jnp.eye(GLA_HEADS, dtype=s.dtype)
    return jnp.einsum('bhvk,hg->bhvgk', st, eye).reshape(B, GLA_WIDTH, GLA_KDIM)


def _t_to_state(st):
    B = st.shape[0]
    s5 = st.reshape(B, GLA_HEADS, GLA_DV, GLA_HEADS, GLA_DK)
    diag = jnp.stack([s5[:, h, :, h, :] for h in range(GLA_HEADS)], axis=1)
    return diag.transpose(0, 1, 3, 2)


def kernel(x_prompt, x_sample, mem_prompt, cache_ckv, cache_kpe, page_table, state_gla, cache_mem_k, cache_mem_v, g_pre, w_in, g_qa, w_uq, g_kva, w_uk, w_uv, g_mla_q, g_mla_k, w_gk, b_gk, g_gla_o, g_mem, w_mem_kv, g_mem_q, g_mem_k, w_out):
    params = dict(g_pre=g_pre, w_in=w_in, g_qa=g_qa, w_uq=w_uq, g_kva=g_kva, w_uk=w_uk, w_uv=w_uv,
                  g_mla_q=g_mla_q, g_mla_k=g_mla_k, w_gk=w_gk, b_gk=b_gk, g_gla_o=g_gla_o, g_mem=g_mem,
                  w_mem_kv=w_mem_kv, g_mem_q=g_mem_q, g_mem_k=g_mem_k, w_out=w_out)
    nb, seq, _ = x_prompt.shape
    ndb, t_new, _ = x_sample.shape
    n_mem = mem_prompt.shape[1]
    depth = w_in.shape[0]
    n_pages = page_table.shape[1]
    past_len = n_pages * cache_ckv.shape[2]

    tm_p = min(256, seq)
    tm_merge = min(512, seq)
    tq = min(256, seq)
    tm_s = min(256, ndb * t_new)
    pp = max(d for d in range(1, 17) if (n_pages + 1) % d == 0)
    gla_tt = min(128, seq)
    gla_cs = min(16, gla_tt)
    nb_s = min(8, ndb)
    nbp = min(4, nb)
    nb_gla = min(8, nb)

    cache_kpe_t = cache_kpe.transpose(0, 1, 3, 2)
    to_t = lambda a: a.transpose(0, 1, 3, 4, 2).reshape(depth, ndb, MEM_WIDTH, n_mem)
    cache_mkt, cache_mvt = to_t(cache_mem_k), to_t(cache_mem_v)
    from_t = lambda a: a.reshape(nb, MEM_HEADS, MEM_HEAD_DIM, n_mem).transpose(0, 3, 1, 2)

    tabs_p = _packed_tables(jnp.arange(seq, dtype=jnp.int32))
    pos_s = past_len + jnp.arange(t_new, dtype=jnp.int32)
    tabs_s = tuple(jnp.tile(t, (tm_s // t_new, 1)) for t in _packed_tables(pos_s))
    tabs_t = _transposed_tables(jnp.arange(past_len + PAGE_SIZE, dtype=jnp.int32))

    xp = x_prompt.reshape(nb * seq, D_MODEL)
    xs = x_sample.reshape(ndb * t_new, D_MODEL)
    mem2 = mem_prompt.reshape(nb * n_mem, D_MODEL)
    ckv_p, kpe_p, gla_p, mk_p, mv_p, ckv_s, kpe_s, gla_s = ([] for _ in range(8))
    for l in range(depth):
        w = _prep_layer(l, params)
        q, k, c, kpe, gates, gq, gk, gv, gg, mq = _inproj(xp, tabs_p, w, tm_p)
        r3 = lambda a: a.reshape(nb, seq, a.shape[-1])
        mla_o = _mla_prompt(r3(q), r3(k), r3(c), w['w_uv_t'], nb, seq, tq, nbp).reshape(nb * seq, MLA_WIDTH)
        gla_o, st = _gla(r3(gq), r3(gk), r3(gv), r3(gg), None, nb_gla, gla_tt, gla_cs)
        mkt, mvt = _memkv(mem2, w, nb, n_mem)
        xp = _merge_memattn(xp, mla_o, gla_o.reshape(nb * seq, GLA_WIDTH), mq, mkt, mvt, gates, w, tm_merge, seq)
        ckv_p.append(c.reshape(nb, seq, KV_RANK))
        kpe_p.append(kpe.reshape(nb, seq, ROPE_DIM))
        gla_p.append(_t_to_state(st))
        mk_p.append(from_t(mkt))
        mv_p.append(from_t(mvt))
        q, k, c, kpe, gates, gq, gk, gv, gg, mq = _inproj(xs, tabs_s, w, tm_s)
        mla_o = _mla_sample(l, q, c, kpe, cache_ckv, cache_kpe_t, page_table, w, tabs_t, pp)
        r3 = lambda a: a.reshape(ndb, t_new, a.shape[-1])
        gla_o, st = _gla_decode(r3(gq), r3(gk), r3(gv), r3(gg), state_gla[l].transpose(1, 2, 3, 0))
        mem_o = _memattn(l, r3(mq), cache_mkt, cache_mvt, nb_s, t_new)
        xs = _merge(xs, mla_o, gla_o.reshape(ndb * t_new, GLA_WIDTH), mem_o.reshape(ndb * t_new, MEM_WIDTH), gates, w, tm_s)
        ckv_s.append(c.reshape(ndb, t_new, KV_RANK))
        kpe_s.append(kpe.reshape(ndb, t_new, ROPE_DIM))
        gla_s.append(st.transpose(3, 0, 1, 2))
    return (xp.reshape(nb, seq, D_MODEL), xs.reshape(ndb, t_new, D_MODEL), jnp.stack(ckv_p), jnp.stack(kpe_p),
            jnp.stack(gla_p), jnp.stack(mk_p), jnp.stack(mv_p), jnp.stack(ckv_s), jnp.stack(kpe_s), jnp.stack(gla_s))
```

```python
import functools

import jax
import jax.numpy as jnp
from jax import lax
from jax.experimental import pallas as pl
from jax.experimental.pallas import tpu as pltpu

f32, bf16 = jnp.float32, jnp.bfloat16

D_MODEL = 1024
PAGE_SIZE = 128
MLA_HEADS = 8
NOPE_DIM = 64
ROPE_DIM = 32
HALF_ROPE = ROPE_DIM // 2
QK_DIM = NOPE_DIM + ROPE_DIM
V_DIM = 64
Q_RANK = 384
KV_RANK = 256
MLA_WIDTH = MLA_HEADS * V_DIM
ROPE_THETA = 10000.0
MLA_SCALE = QK_DIM ** -0.5
LOG2_E = 1.4426950408889634
GLA_HEADS = 4
GLA_DK = 32
GLA_DV = 64
GLA_KDIM = GLA_HEADS * GLA_DK
GLA_WIDTH = GLA_HEADS * GLA_DV
GLA_GATE_RANK = 16
GLA_NORMALIZER = 16.0
MEM_HEADS = 4
MEM_HEAD_DIM = 64
MEM_WIDTH = MEM_HEADS * MEM_HEAD_DIM
MEM_SCALE = MEM_HEAD_DIM ** -0.5
D_MIX = MLA_WIDTH + GLA_WIDTH + MEM_WIDTH
EPS = 1e-6
NEG_INF = -1e30

LANES = 128
HEAD_PAD = LANES
QK_PACKED = MLA_HEADS * HEAD_PAD

OFF_CQ = 0
OFF_CKV = OFF_CQ + Q_RANK
OFF_GATE_MLA = OFF_CKV + KV_RANK
OFF_GQ = OFF_GATE_MLA + MLA_WIDTH
OFF_GK = OFF_GQ + GLA_KDIM
OFF_GV = OFF_GK + GLA_KDIM
OFF_GATE_GLA = OFF_GV + GLA_WIDTH
OFF_MQ = OFF_GATE_GLA + GLA_WIDTH
OFF_GATE_MEM = OFF_MQ + MEM_WIDTH
OFF_MISC = OFF_GATE_MEM + MEM_WIDTH
D_IN_PACKED = OFF_MISC + LANES
KPE_LANE = NOPE_DIM

VMEM_LIMIT = 56 * 1024 * 1024
PAGE_BUFFERS = 3


def _cparams(*sem):
    return pltpu.CompilerParams(dimension_semantics=sem, vmem_limit_bytes=VMEM_LIMIT)


def _nt(a, b):
    return lax.dot_general(a, b, (((1,), (1,)), ((), ())), preferred_element_type=f32)


def _tn(a, b):
    return lax.dot_general(a, b, (((0,), (0,)), ((), ())), preferred_element_type=f32)


def _mm(a, b):
    return jnp.dot(a, b, preferred_element_type=f32)


def _rms_rows(x, g):
    return x * lax.rsqrt(jnp.mean(x * x, axis=-1, keepdims=True) + EPS) * g


def _half_head_rms(blk, g, lane):
    sq = blk * blk
    lo = lane < 64
    ss_lo = jnp.sum(jnp.where(lo, sq, 0.0), axis=-1, keepdims=True)
    ss_hi = jnp.sum(jnp.where(lo, 0.0, sq), axis=-1, keepdims=True)
    ss = jnp.where(lo, ss_lo, ss_hi)
    return blk * lax.rsqrt(ss * (1.0 / 64.0) + EPS) * g


def _rope_packed(x, cos, sin_up, sin_dn):
    return x * cos + pltpu.roll(x, HALF_ROPE, 1) * sin_up + pltpu.roll(x, LANES - HALF_ROPE, 1) * sin_dn


def _inproj_body(x_ref, gpre_ref, win_ref, gqa_ref, wuq_ref, gkva_ref, wuk_ref, gq_ref, gk_ref,
                 cos_ref, sup_ref, sdn_ref, wgk_ref, bgk_ref, gmq_ref,
                 q_out, k_out, ckv_out, kpe_out, gates_out, glaq_out, glak_out, glav_out, glag_out, memq_out):
    x = x_ref[...]
    xb = _rms_rows(x, gpre_ref[...]).astype(bf16)

    def seg(off, n):
        return _mm(xb, win_ref[:, off:off + n])

    cos, sup, sdn = cos_ref[...], sup_ref[...], sdn_ref[...]
    lane = lax.broadcasted_iota(jnp.int32, (1, LANES), 1)

    cq = _rms_rows(seg(OFF_CQ, Q_RANK), gqa_ref[...]).astype(bf16)
    qf = _mm(cq, wuq_ref[...])
    gq = gq_ref[...]
    for h in range(MLA_HEADS):
        qh = qf[:, h * HEAD_PAD:(h + 1) * HEAD_PAD]
        ss = jnp.sum(qh * qh, axis=-1, keepdims=True)
        qn = qh * lax.rsqrt(ss * (1.0 / QK_DIM) + EPS) * gq
        q_out[:, h * HEAD_PAD:(h + 1) * HEAD_PAD] = (_rope_packed(qn, cos, sup, sdn) * (MLA_SCALE * LOG2_E)).astype(bf16)

    c = _rms_rows(seg(OFF_CKV, KV_RANK), gkva_ref[...])
    ckv_out[...] = c
    misc = seg(OFF_MISC, LANES)
    kpe_wide = jnp.where((lane >= KPE_LANE) & (lane < KPE_LANE + ROPE_DIM), misc, 0.0)
    kpe_out[...] = misc[:, KPE_LANE:KPE_LANE + ROPE_DIM]
    kf = _mm(c.astype(bf16), wuk_ref[...])
    gk = gk_ref[...]
    for h in range(MLA_HEADS):
        kh = kf[:, h * HEAD_PAD:(h + 1) * HEAD_PAD] + kpe_wide
        ss = jnp.sum(kh * kh, axis=-1, keepdims=True)
        kn = kh * lax.rsqrt(ss * (1.0 / QK_DIM) + EPS) * gk
        k_out[:, h * HEAD_PAD:(h + 1) * HEAD_PAD] = _rope_packed(kn, cos, sup, sdn).astype(bf16)

    g1 = seg(OFF_GATE_MLA, MLA_WIDTH)
    gates_out[:, 0:MLA_WIDTH] = g1 * jax.nn.sigmoid(g1)
    g2 = seg(OFF_GATE_GLA, GLA_WIDTH)
    gates_out[:, MLA_WIDTH:MLA_WIDTH + GLA_WIDTH] = g2 * jax.nn.sigmoid(g2)
    g3 = seg(OFF_GATE_MEM, MEM_WIDTH)
    gates_out[:, MLA_WIDTH + GLA_WIDTH:D_MIX] = g3 * jax.nn.sigmoid(g3)

    glaq_out[...] = seg(OFF_GQ, GLA_KDIM) * (GLA_DK ** -0.5)
    glak_out[...] = seg(OFF_GK, GLA_KDIM)
    glav_out[...] = seg(OFF_GV, GLA_WIDTH)
    gl = _mm(misc.astype(bf16), wgk_ref[...]) + bgk_ref[...]
    glag_out[...] = jax.nn.log_sigmoid(gl) * (LOG2_E / GLA_NORMALIZER)

    mq = seg(OFF_MQ, MEM_WIDTH)
    gmq = gmq_ref[...]
    for j in range(MEM_WIDTH // LANES):
        blk = _half_head_rms(mq[:, j * LANES:(j + 1) * LANES], gmq, lane)
        memq_out[:, j * LANES:(j + 1) * LANES] = (blk * (MEM_SCALE * LOG2_E)).astype(bf16)


def _inproj(x2, tabs, w, tm):
    T = x2.shape[0]
    nt = tabs[0].shape[0] // tm
    row = lambda i: (i, 0)
    fix = lambda i: (0, 0)
    tab = lambda i: (i % nt, 0)

    def full(a):
        return pl.BlockSpec(a.shape, fix)

    consts1 = (w['g_pre'], w['w_in'], w['g_qa'], w['w_uq'], w['g_kva'], w['w_uk'], w['g_q'], w['g_k'])
    consts2 = (w['w_gk'], w['b_gk'], w['g_mem_q'])
    outs = [(QK_PACKED, bf16), (QK_PACKED, bf16), (KV_RANK, f32), (ROPE_DIM, f32), (D_MIX, f32),
            (GLA_KDIM, f32), (GLA_KDIM, f32), (GLA_WIDTH, f32), (GLA_KDIM, f32), (MEM_WIDTH, bf16)]
    return pl.pallas_call(
        _inproj_body,
        grid=(T // tm,),
        in_specs=[pl.BlockSpec((tm, D_MODEL), row)] + [full(a) for a in consts1]
        + [pl.BlockSpec((tm, LANES), tab)] * 3 + [full(a) for a in consts2],
        out_specs=[pl.BlockSpec((tm, n), row) for n, _ in outs],
        out_shape=[jax.ShapeDtypeStruct((T, n), dt) for n, dt in outs],
        compiler_params=_cparams("arbitrary"),
        name="inproj",
    )(x2, *consts1, *tabs, *consts2)


def _mla_prompt_body(q_ref, k_ref, c_ref, wuvt_ref, o_ref, ct_scr, acc_scr, m_scr, l_scr, ot_scr, *, tq, nbp):
    i = pl.program_id(1)
    nblk = ct_scr.shape[1]

    @pl.when(i == 0)
    def _():
        for bb in range(nbp):
            for j in range(nblk):
                ct_scr[bb, j] = c_ref[bb, j * tq:(j + 1) * tq, :].T.astype(bf16)

    m_scr[...] = jnp.full(m_scr.shape, NEG_INF, f32)
    l_scr[...] = jnp.zeros(l_scr.shape, f32)
    acc_scr[...] = jnp.zeros(acc_scr.shape, f32)
    key = lax.broadcasted_iota(jnp.int32, (tq, tq), 0)
    qry = lax.broadcasted_iota(jnp.int32, (tq, tq), 1)
    causal = key <= qry

    def step(kb, masked):
        ks = pl.ds(pl.multiple_of(kb * tq, tq), tq)
        for h in range(MLA_HEADS):
            hs = slice(h * HEAD_PAD, (h + 1) * HEAD_PAD)
            for bb in range(nbp):
                s = _nt(k_ref[bb, ks, hs], q_ref[bb, :, hs])
                if masked:
                    s = jnp.where(causal, s, NEG_INF)
                m_old = m_scr[bb, h:h + 1, :]
                m_new = jnp.maximum(m_old, jnp.max(s, axis=0, keepdims=True))
                alpha = jnp.exp2(m_old - m_new)
                p = jnp.exp2(s - m_new)
                l_scr[bb, h:h + 1, :] = l_scr[bb, h:h + 1, :] * alpha + jnp.sum(p, axis=0, keepdims=True)
                acc_scr[bb, h] = acc_scr[bb, h] * alpha + _mm(ct_scr[bb, kb], p.astype(bf16))
                m_scr[bb, h:h + 1, :] = m_new

    def body(kb, carry):
        step(kb, False)
        return carry

    lax.fori_loop(0, i, body, 0)
    step(i, True)

    for bb in range(nbp):
        for h in range(MLA_HEADS):
            lat_t = (acc_scr[bb, h] * (1.0 / l_scr[bb, h:h + 1, :])).astype(bf16)
            ot_scr[h * V_DIM:(h + 1) * V_DIM, :] = _mm(wuvt_ref[h], lat_t)
        o_ref[bb] = ot_scr[...].T


def _mla_prompt(q, k, c, wuv_t, nb, seq, tq, nbp):
    nq = seq // tq
    return pl.pallas_call(
        functools.partial(_mla_prompt_body, tq=tq, nbp=nbp),
        grid=(nb // nbp, nq),
        in_specs=[pl.BlockSpec((nbp, tq, QK_PACKED), lambda b, i: (b, i, 0)),
                  pl.BlockSpec((nbp, seq, QK_PACKED), lambda b, i: (b, 0, 0), pipeline_mode=pl.Buffered(1)),
                  pl.BlockSpec((nbp, seq, KV_RANK), lambda b, i: (b, 0, 0), pipeline_mode=pl.Buffered(1)),
                  pl.BlockSpec(wuv_t.shape, lambda b, i: (0, 0, 0))],
        out_specs=pl.BlockSpec((nbp, tq, MLA_WIDTH), lambda b, i: (b, i, 0)),
        out_shape=jax.ShapeDtypeStruct((nb, seq, MLA_WIDTH), f32),
        scratch_shapes=[pltpu.VMEM((nbp, nq, KV_RANK, tq), bf16), pltpu.VMEM((nbp, MLA_HEADS, KV_RANK, tq), f32),
                        pltpu.VMEM((nbp, MLA_HEADS, tq), f32), pltpu.VMEM((nbp, MLA_HEADS, tq), f32),
                        pltpu.VMEM((MLA_WIDTH, tq), f32)],
        compiler_params=_cparams("arbitrary", "arbitrary"),
        name="mla_prompt",
    )(q, k, c, wuv_t)


def _mla_sample_body(pt_ref, q_ref, cnew_ref, kpenewt_ref, wukt_ref, wukg_ref, gkr_ref, cost_ref, sint_ref, wuv_ref,
                     ckv_hbm, kpe_hbm, o_ref,
                     cbuf, kbuf, sem, lhs_scr, qr_scr, cb_scr, s_scr, ql_all, qr_all, acc_scr, m_scr, l_scr,
                     *, layer, gp, n_pages, t_new):
    ndb = ql_all.shape[0]
    ng = (n_pages + 1) // gp
    kb = gp * PAGE_SIZE
    total = ndb * ng
    nrow = MLA_HEADS * t_new
    n_nope = MLA_HEADS * NOPE_DIM
    new_rows = pl.ds(kb - PAGE_SIZE, t_new)
    new_lanes = pl.ds(kb - PAGE_SIZE, PAGE_SIZE)

    def group_copies(t):
        b, g, slot = t // ng, t % ng, t % PAGE_BUFFERS
        cps = []
        for i in range(gp):
            page = pt_ref[b * n_pages + jnp.minimum(g * gp + i, n_pages - 1)]
            lanes = pl.ds(i * PAGE_SIZE, PAGE_SIZE)
            cps.append(pltpu.make_async_copy(ckv_hbm.at[layer, page], cbuf.at[slot, lanes, :], sem.at[0, slot]))
            cps.append(pltpu.make_async_copy(kpe_hbm.at[layer, page], kbuf.at[slot, :, lanes], sem.at[1, slot]))
        return cps

    def reset_state():
        m_scr[...] = jnp.full((nrow, 1), NEG_INF, f32)
        l_scr[...] = jnp.zeros((nrow, 1), f32)
        acc_scr[...] = jnp.zeros((nrow, KV_RANK), f32)

    gkr = gkr_ref[...]
    key_lane = lax.broadcasted_iota(jnp.int32, (nrow, PAGE_SIZE), 1)
    qry_row = lax.broadcasted_iota(jnp.int32, (nrow, PAGE_SIZE), 0) % t_new

    def prepare_group(t):
        b, g, pslot = t // ng, t % ng, t % PAGE_BUFFERS
        lhs_scr[n_nope:, :] = ql_all[b].astype(bf16)
        qr_scr[...] = qr_all[b].astype(bf16)
        for cp in group_copies(t):
            cp.wait()
        is_last = g == ng - 1
        cbuf[pslot, new_rows, :] = jnp.where(is_last, cnew_ref[b], cbuf[pslot, new_rows, :])
        kbuf[pslot, :, new_lanes] = jnp.where(is_last, kpenewt_ref[b], kbuf[pslot, :, new_lanes])

    def score_group(t, slot):
        g, pslot = t % ng, t % PAGE_BUFFERS
        is_last = g == ng - 1
        cb = cbuf[pslot].astype(bf16)
        cb_scr[slot] = cb
        kt = kbuf[pslot]
        cos_t, sin_t = cost_ref[g], sint_ref[g]
        big = _nt(lhs_scr[...], cb)
        kn = big[0:n_nope]
        ss_nope = jnp.sum((kn * kn).reshape(MLA_HEADS, NOPE_DIM, kb), axis=1)
        ss = ss_nope + jnp.sum(kt * kt, axis=0, keepdims=True)
        r = lax.rsqrt(ss * (1.0 / QK_DIM) + EPS)
        kg = kt * gkr
        k1, k2 = kg[0:HALF_ROPE], kg[HALF_ROPE:ROPE_DIM]
        kr = jnp.concatenate([k1 * cos_t - k2 * sin_t, k2 * cos_t + k1 * sin_t], axis=0).astype(bf16)
        s = big[n_nope:] + _mm(qr_scr[...], kr)
        s = (s.reshape(MLA_HEADS, t_new, kb) * r[:, None, :]).reshape(nrow, kb)
        first_dead = jnp.where(is_last, 1, PAGE_SIZE + t_new)
        tail = jnp.where(key_lane >= qry_row + first_dead, NEG_INF, s[:, kb - PAGE_SIZE:])
        s_scr[slot] = jnp.concatenate([s[:, :kb - PAGE_SIZE], tail], axis=1)

    def value_group(slot):
        s = s_scr[slot]
        m_old = m_scr[...]
        m_new = jnp.maximum(m_old, jnp.max(s, axis=-1, keepdims=True))
        alpha = jnp.exp2(m_old - m_new)
        p = jnp.exp2(s - m_new)
        l_scr[...] = l_scr[...] * alpha + jnp.sum(p, axis=-1, keepdims=True)
        acc_scr[...] = acc_scr[...] * alpha + _mm(p.astype(bf16), cb_scr[slot])
        m_scr[...] = m_new

    def finish_batch(b, closed):
        ql_all[b] = jnp.where(closed, acc_scr[...] * (1.0 / l_scr[...]), ql_all[b])
        m_scr[...] = jnp.where(closed, NEG_INF, m_scr[...])
        l_scr[...] = jnp.where(closed, 0.0, l_scr[...])
        acc_scr[...] = jnp.where(closed, 0.0, acc_scr[...])

    lhs_scr[0:n_nope, :] = wukt_ref[...]
    for h in range(MLA_HEADS):
        q_h = q_ref[:, h * HEAD_PAD:(h + 1) * HEAD_PAD]
        rows = slice(h * t_new, (h + 1) * t_new)
        ql_all[:, rows, :] = _mm(q_h, wukg_ref[h]).reshape(ndb, t_new, KV_RANK)
        qr_all[:, rows, :] = q_h[:, KPE_LANE:KPE_LANE + ROPE_DIM].astype(f32).reshape(ndb, t_new, ROPE_DIM)
    reset_state()

    ahead = PAGE_BUFFERS - 1

    def skewed_step(t, slot, prefetch):
        if prefetch:
            for cp in group_copies(t + ahead):
                cp.start()
        prepare_group(t)
        score_group(t, slot)
        value_group(1 - slot)
        closes = t % ng == 0
        finish_batch(jnp.maximum(t // ng - 1, 0), closes)

    for t in range(min(ahead, total)):
        for cp in group_copies(t):
            cp.start()
    if total > ahead:
        for cp in group_copies(ahead):
            cp.start()
    prepare_group(0)
    score_group(0, 0)

    n_pairs = max(total - 1 - ahead, 0) // 2

    def body(k, carry):
        skewed_step(2 * k + 1, 1, True)
        skewed_step(2 * k + 2, 0, True)
        return carry

    lax.fori_loop(0, n_pairs, body, 0)
    for t in range(2 * n_pairs + 1, total):
        skewed_step(t, t % 2, t + ahead < total)
    value_group((total - 1) % 2)
    finish_batch(ndb - 1, True)

    for h in range(MLA_HEADS):
        lat_h = ql_all[:, h * t_new:(h + 1) * t_new, :].reshape(ndb * t_new, KV_RANK).astype(bf16)
        contrib = _mm(lat_h, wuv_ref[h])
        ps = slice((h // 2) * LANES, (h // 2 + 1) * LANES)
        if h % 2 == 0:
            o_ref[:, ps] = contrib
        else:
            o_ref[:, ps] += contrib


def _mla_sample(layer, q, c_new, kpe_new, cache_ckv, cache_kpe_t, page_table, w, tabs_t, gp):
    ndb, n_pages = page_table.shape
    t_new = q.shape[0] // ndb
    ng = (n_pages + 1) // gp
    kb = gp * PAGE_SIZE
    nrow = MLA_HEADS * t_new
    cost, sint = tabs_t
    by_group = lambda a: a.reshape(HALF_ROPE, ng, kb).transpose(1, 0, 2)
    kpe_new_t = jnp.pad(kpe_new.reshape(ndb, t_new, ROPE_DIM).transpose(0, 2, 1),
                        ((0, 0), (0, 0), (0, PAGE_SIZE - t_new)))
    args = (q, c_new.reshape(ndb, t_new, KV_RANK), kpe_new_t, w['w_uk_t'], w['w_uk_g'], w['g_k_rope'],
            by_group(cost), by_group(sint), w['w_uv_pairs'])

    def full(a):
        return pl.BlockSpec(a.shape, lambda i, pt, nd=a.ndim: (0,) * nd)

    return pl.pallas_call(
        functools.partial(_mla_sample_body, layer=layer, gp=gp, n_pages=n_pages, t_new=t_new),
        grid_spec=pltpu.PrefetchScalarGridSpec(
            num_scalar_prefetch=1,
            grid=(1,),
            in_specs=[full(a) for a in args] + [pl.BlockSpec(memory_space=pl.ANY)] * 2,
            out_specs=pl.BlockSpec((ndb * t_new, MLA_WIDTH), lambda i, pt: (0, 0)),
            scratch_shapes=[pltpu.VMEM((PAGE_BUFFERS, kb, KV_RANK), f32),
                            pltpu.VMEM((PAGE_BUFFERS, ROPE_DIM, kb), f32),
                            pltpu.SemaphoreType.DMA((2, PAGE_BUFFERS)),
                            pltpu.VMEM((MLA_HEADS * NOPE_DIM + nrow, KV_RANK), bf16),
                            pltpu.VMEM((nrow, ROPE_DIM), bf16),
                            pltpu.VMEM((2, kb, KV_RANK), bf16),
                            pltpu.VMEM((2, nrow, kb), f32),
                            pltpu.VMEM((ndb, nrow, KV_RANK), f32),
                            pltpu.VMEM((ndb, nrow, ROPE_DIM), f32),
                            pltpu.VMEM((nrow, KV_RANK), f32),
                            pltpu.VMEM((nrow, 1), f32),
                            pltpu.VMEM((nrow, 1), f32)]),
        out_shape=jax.ShapeDtypeStruct((ndb * t_new, MLA_WIDTH), f32),
        compiler_params=_cparams("arbitrary"),
        name="mla_sample",
    )(page_table.reshape(-1), *args, cache_ckv, cache_kpe_t)


def _split3(x):
    hi = x.astype(bf16)
    r1 = x - hi.astype(f32)
    mid = r1.astype(bf16)
    lo = (r1 - mid.astype(f32)).astype(bf16)
    return hi, mid, lo


def _gla_body(q_ref, k_ref, v_ref, g_ref, o_ref, sT_out, st_scr, *, nb, tt, cs):
    ti = pl.program_id(1)
    nchunk = tt // cs

    @pl.when(ti == 0)
    def _():
        st_scr[...] = jnp.zeros(st_scr.shape, f32)

    tri = (lax.broadcasted_iota(jnp.int32, (tt, tt), 1) <= lax.broadcasted_iota(jnp.int32, (tt, tt), 0)).astype(bf16)
    kv_head = (lax.broadcasted_iota(jnp.int32, (GLA_KDIM, GLA_WIDTH), 0) // GLA_DK
               == lax.broadcasted_iota(jnp.int32, (GLA_KDIM, GLA_WIDTH), 1) // GLA_DV)
    block_ones = kv_head.astype(bf16)
    vk_head = (lax.broadcasted_iota(jnp.int32, (GLA_WIDTH, GLA_KDIM), 0) // GLA_DV
               == lax.broadcasted_iota(jnp.int32, (GLA_WIDTH, GLA_KDIM), 1) // GLA_DK)
    sel_t = (lax.broadcasted_iota(jnp.int32, (cs, cs * cs), 1) // cs
             == lax.broadcasted_iota(jnp.int32, (cs, cs * cs), 0)).astype(bf16)
    t_idx = lax.broadcasted_iota(jnp.int32, (cs, cs, GLA_KDIM), 0)
    s_idx = lax.broadcasted_iota(jnp.int32, (cs, cs, GLA_KDIM), 1)
    causal3 = s_idx <= t_idx

    cums = []
    for b in range(nb):
        hi, mid, lo = _split3(g_ref[b])
        cums.append(_mm(tri, hi) + _mm(tri, mid) + _mm(tri, lo))
    for n in range(nchunk):
        sl = slice(n * cs, (n + 1) * cs)
        for b in range(nb):
            cum = cums[b]
            base = cum[n * cs - 1:n * cs] if n > 0 else jnp.zeros((1, GLA_KDIM), f32)
            bc = cum[sl] - base
            b_last = bc[cs - 1:cs]
            qc, kc, vc = q_ref[b, sl, :], k_ref[b, sl, :], v_ref[b, sl, :]
            st = st_scr[b]
            o_inter = _nt((qc * jnp.exp2(bc)).astype(bf16), st.astype(bf16))
            diff = jnp.where(causal3, bc[:, None, :] - bc[None, :, :], NEG_INF)
            d3 = qc[:, None, :] * kc[None, :, :] * jnp.exp2(diff)
            a_exp = _mm(d3.reshape(cs * cs, GLA_KDIM).astype(bf16), block_ones)
            xv = (a_exp.reshape(cs, cs, GLA_WIDTH) * vc[None, :, :]).reshape(cs * cs, GLA_WIDTH)
            o_intra = _mm(sel_t, xv.astype(bf16))
            o_ref[b, sl, :] = o_inter + o_intra
            kd = (kc * jnp.exp2(b_last - bc)).astype(bf16)
            upd = _tn(vc.astype(bf16), kd)
            st_scr[b] = st * jnp.exp2(b_last) + jnp.where(vk_head, upd, 0.0)

    @pl.when(ti == pl.num_programs(1) - 1)
    def _():
        sT_out[...] = st_scr[...]


def _gla(q, k, v, g, nb, tt, cs):
    B, L, _ = q.shape
    tok = lambda bi, ti: (bi, ti, 0)
    st = lambda bi, ti: (bi, 0, 0)
    return pl.pallas_call(
        functools.partial(_gla_body, nb=nb, tt=tt, cs=cs),
        grid=(B // nb, L // tt),
        in_specs=[pl.BlockSpec((nb, tt, GLA_KDIM), tok), pl.BlockSpec((nb, tt, GLA_KDIM), tok),
                  pl.BlockSpec((nb, tt, GLA_WIDTH), tok), pl.BlockSpec((nb, tt, GLA_KDIM), tok)],
        out_specs=[pl.BlockSpec((nb, tt, GLA_WIDTH), tok), pl.BlockSpec((nb, GLA_WIDTH, GLA_KDIM), st)],
        out_shape=[jax.ShapeDtypeStruct((B, L, GLA_WIDTH), f32), jax.ShapeDtypeStruct((B, GLA_WIDTH, GLA_KDIM), f32)],
        scratch_shapes=[pltpu.VMEM((nb, GLA_WIDTH, GLA_KDIM), f32)],
        compiler_params=_cparams("arbitrary", "arbitrary"),
        name="gla",
    )(q, k, v, g)


def _gla_decode_body(q_ref, k_ref, v_ref, g_ref, s0_ref, o_ref, s_out, qt_scr, kt_scr, et_scr, vt_scr, ot_scr):
    nb, t_new, _ = q_ref.shape
    for t in range(t_new):
        qt_scr[t] = q_ref[:, t, :].T
        kt_scr[t] = k_ref[:, t, :].T
        et_scr[t] = jnp.exp2(g_ref[:, t, :].T)
        vt_scr[t] = v_ref[:, t, :].T
    ot_scr[...] = jnp.zeros(ot_scr.shape, f32)

    for h in range(GLA_HEADS):
        vs = slice(h * GLA_DV, (h + 1) * GLA_DV)

        def one_k(kk, carry):
            row = pl.ds(h * GLA_DK + kk, 1)
            s = s0_ref[h, kk]
            for t in range(t_new):
                s = s * et_scr[t, row, :] + kt_scr[t, row, :] * vt_scr[t, vs, :]
                ot_scr[t, vs, :] += qt_scr[t, row, :] * s
            s_out[h, kk] = s
            return carry

        lax.fori_loop(0, GLA_DK, one_k, 0)

    for t in range(t_new):
        o_ref[:, t, :] = ot_scr[t].T


def _gla_decode(q, k, v, g, s0):
    B, T, _ = q.shape
    full = lambda a: pl.BlockSpec(a.shape, lambda i, nd=a.ndim: (0,) * nd)
    o_shape = jax.ShapeDtypeStruct((B, T, GLA_WIDTH), f32)
    s_shape = jax.ShapeDtypeStruct(s0.shape, f32)
    return pl.pallas_call(
        _gla_decode_body,
        grid=(1,),
        in_specs=[full(a) for a in (q, k, v, g, s0)],
        out_specs=[full(o_shape), full(s_shape)],
        out_shape=[o_shape, s_shape],
        scratch_shapes=[pltpu.VMEM((T, GLA_KDIM, B), f32)] * 3 + [pltpu.VMEM((T, GLA_WIDTH, B), f32)] * 2,
        compiler_params=_cparams("arbitrary"),
        name="gla_decode",
    )(q, k, v, g, s0)


def _memkv_body(mem_ref, gmem_ref, w_ref, gk_ref, mkt_out, mvt_out):
    xb = _rms_rows(mem_ref[...], gmem_ref[...]).astype(bf16)
    kv = _mm(xb, w_ref[...])
    lane = lax.broadcasted_iota(jnp.int32, (1, LANES), 1)
    gk = gk_ref[...]
    for j in range(MEM_WIDTH // LANES):
        blk = _half_head_rms(kv[:, j * LANES:(j + 1) * LANES], gk, lane)
        mkt_out[j * LANES:(j + 1) * LANES, :] = blk.T
        mvt_out[j * LANES:(j + 1) * LANES, :] = kv[:, MEM_WIDTH + j * LANES:MEM_WIDTH + (j + 1) * LANES].T


def _memkv(mem2, w, nb, n_mem):
    row = lambda i: (i, 0)
    fix = lambda i: (0, 0)
    return pl.pallas_call(
        _memkv_body,
        grid=(nb,),
        in_specs=[pl.BlockSpec((n_mem, D_MODEL), row), pl.BlockSpec(w['g_mem'].shape, fix),
                  pl.BlockSpec(w['w_mem_kv'].shape, fix), pl.BlockSpec(w['g_mem_k'].shape, fix)],
        out_specs=[pl.BlockSpec((None, MEM_WIDTH, n_mem), lambda i: (i, 0, 0))] * 2,
        out_shape=[jax.ShapeDtypeStruct((nb, MEM_WIDTH, n_mem), f32)] * 2,
        compiler_params=_cparams("arbitrary"),
        name="memkv",
    )(mem2, w['g_mem'], w['w_mem_kv'], w['g_mem_k'])


def _mem_attend(q, mkt, mvt):
    tq = q.shape[0]
    lane_head = lax.broadcasted_iota(jnp.int32, (1, MEM_WIDTH), 1) // MEM_HEAD_DIM
    qs = jnp.concatenate([jnp.where(lane_head == h, q, jnp.zeros_like(q)) for h in range(MEM_HEADS)], axis=0)
    s = _mm(qs, mkt.astype(bf16))
    p = jnp.exp2(s - jnp.max(s, axis=-1, keepdims=True))
    pv = _nt(p.astype(bf16), mvt.astype(bf16)) * (1.0 / jnp.sum(p, axis=-1, keepdims=True))
    o = jnp.zeros((tq, MEM_WIDTH), f32)
    for h in range(MEM_HEADS):
        o = o + jnp.where(lane_head == h, pv[h * tq:(h + 1) * tq], 0.0)
    return o


def _memattn_body(q_ref, mkt_ref, mvt_ref, o_ref, *, nb, tq):
    for b in range(nb):
        o_ref[b] = _mem_attend(q_ref[b], mkt_ref[b], mvt_ref[b])


def _memattn(layer, q3, mkt4, mvt4, nb, tq):
    B, L, _ = q3.shape
    n_mem = mkt4.shape[-1]
    kv_spec = pl.BlockSpec((None, nb, MEM_WIDTH, n_mem), lambda b, i: (layer, b, 0, 0))
    return pl.pallas_call(
        functools.partial(_memattn_body, nb=nb, tq=tq),
        grid=(B // nb, L // tq),
        in_specs=[pl.BlockSpec((nb, tq, MEM_WIDTH), lambda b, i: (b, i, 0)), kv_spec, kv_spec],
        out_specs=pl.BlockSpec((nb, tq, MEM_WIDTH), lambda b, i: (b, i, 0)),
        out_shape=jax.ShapeDtypeStruct((B, L, MEM_WIDTH), f32),
        compiler_params=_cparams("arbitrary", "arbitrary"),
        name="memattn",
    )(q3, mkt4, mvt4)


def _merge_tail(x_ref, mla_ref, gla_ref, mem_o, gates_ref, ggla_ref, wout_ref, y_ref):
    lane = lax.broadcasted_iota(jnp.int32, (1, LANES), 1)
    ggla = ggla_ref[...]
    y = x_ref[...]
    m1 = (gates_ref[:, 0:MLA_WIDTH] * mla_ref[...]).astype(bf16)
    y = y + _mm(m1, wout_ref[0:MLA_WIDTH, :])
    for j in range(GLA_WIDTH // LANES):
        ls = slice(j * LANES, (j + 1) * LANES)
        gn = _half_head_rms(gla_ref[:, ls], ggla, lane)
        gs = slice(MLA_WIDTH + j * LANES, MLA_WIDTH + (j + 1) * LANES)
        y = y + _mm((gates_ref[:, gs] * gn).astype(bf16), wout_ref[gs, :])
    ms = slice(MLA_WIDTH + GLA_WIDTH, D_MIX)
    m3 = (gates_ref[:, ms] * mem_o).astype(bf16)
    y_ref[...] = y + _mm(m3, wout_ref[ms, :])


def _merge_body(x_ref, mla_ref, gla_ref, mem_ref, gates_ref, ggla_ref, wout_ref, y_ref):
    _merge_tail(x_ref, mla_ref, gla_ref, mem_ref[...], gates_ref, ggla_ref, wout_ref, y_ref)


def _merge_memattn_body(x_ref, mla_ref, gla_ref, mq_ref, mkt_ref, mvt_ref, gates_ref, ggla_ref, wout_ref, y_ref):
    mem_o = _mem_attend(mq_ref[...], mkt_ref[...], mvt_ref[...])
    _merge_tail(x_ref, mla_ref, gla_ref, mem_o, gates_ref, ggla_ref, wout_ref, y_ref)


def _merge(x2, mla_o, gla_o, mem_o, gates, w, tm):
    T = x2.shape[0]
    row = lambda i: (i, 0)
    fix = lambda i: (0, 0)
    return pl.pallas_call(
        _merge_body,
        grid=(T // tm,),
        in_specs=[pl.BlockSpec((tm, D_MODEL), row), pl.BlockSpec((tm, MLA_WIDTH), row),
                  pl.BlockSpec((tm, GLA_WIDTH), row), pl.BlockSpec((tm, MEM_WIDTH), row),
                  pl.BlockSpec((tm, D_MIX), row), pl.BlockSpec(w['g_gla_o'].shape, fix),
                  pl.BlockSpec(w['w_out'].shape, fix)],
        out_specs=pl.BlockSpec((tm, D_MODEL), row),
        out_shape=jax.ShapeDtypeStruct((T, D_MODEL), f32),
        compiler_params=_cparams("arbitrary"),
        name="merge",
    )(x2, mla_o, gla_o, mem_o, gates, w['g_gla_o'], w['w_out'])


def _merge_memattn(x2, mla_o, gla_o, mq, mkt, mvt, gates, w, tm, seq):
    T = x2.shape[0]
    n_mem = mkt.shape[-1]
    per_batch = seq // tm
    row = lambda i: (i, 0)
    fix = lambda i: (0, 0)
    kv_spec = pl.BlockSpec((None, MEM_WIDTH, n_mem), lambda i: (i // per_batch, 0, 0))
    return pl.pallas_call(
        _merge_memattn_body,
        grid=(T // tm,),
        in_specs=[pl.BlockSpec((tm, D_MODEL), row), pl.BlockSpec((tm, MLA_WIDTH), row),
                  pl.BlockSpec((tm, GLA_WIDTH), row), pl.BlockSpec((tm, MEM_WIDTH), row), kv_spec, kv_spec,
                  pl.BlockSpec((tm, D_MIX), row), pl.BlockSpec(w['g_gla_o'].shape, fix),
                  pl.BlockSpec(w['w_out'].shape, fix)],
        out_specs=pl.BlockSpec((tm, D_MODEL), row),
        out_shape=jax.ShapeDtypeStruct((T, D_MODEL), f32),
        compiler_params=_cparams("arbitrary"),
        name="merge_memattn",
    )(x2, mla_o, gla_o, mq, mkt, mvt, gates, w['g_gla_o'], w['w_out'])


def _pad_head(a):
    return jnp.pad(a, [(0, 0)] * (a.ndim - 1) + [(0, HEAD_PAD - QK_DIM)])


def _prep_layer(l, p):
    w_in = p['w_in'][l]
    cuts, o = [], 0
    for n in (Q_RANK, KV_RANK, ROPE_DIM, MLA_WIDTH, GLA_KDIM, GLA_KDIM, GLA_WIDTH, GLA_GATE_RANK, GLA_WIDTH,
              MEM_WIDTH, MEM_WIDTH):
        cuts.append(w_in[:, o:o + n])
        o += n
    cq, ckv, kpe, gate_mla, gq, gk, gv, gg, gate_gla, mq, gate_mem = cuts
    z = lambda n: jnp.zeros((D_MODEL, n), f32)
    misc = jnp.concatenate([gg, z(KPE_LANE - GLA_GATE_RANK), kpe, z(LANES - KPE_LANE - ROPE_DIM)], axis=1)
    w_in_packed = jnp.concatenate([cq, ckv, gate_mla, gq, gk, gv, gate_gla, mq, gate_mem, misc], axis=1)

    w_uk = p['w_uk'][l]
    g_k = p['g_mla_k'][l]
    wuk_packed = jnp.pad(w_uk, ((0, 0), (0, 0), (0, HEAD_PAD - NOPE_DIM))).reshape(KV_RANK, QK_PACKED)
    w_uk_t = w_uk.reshape(KV_RANK, MLA_HEADS * NOPE_DIM).T
    w_uk_g = jnp.pad((w_uk * g_k[None, None, :NOPE_DIM]).transpose(1, 2, 0),
                     ((0, 0), (0, HEAD_PAD - NOPE_DIM), (0, 0)))
    w_uv = p['w_uv'][l].transpose(1, 0, 2)
    zeros = jnp.zeros_like(w_uv)
    even = jnp.arange(MLA_HEADS)[:, None, None] % 2 == 0
    w_uv_pairs = jnp.concatenate([jnp.where(even, w_uv, zeros), jnp.where(even, zeros, w_uv)], axis=-1)
    w_gk = jnp.pad(p['w_gk'][l], ((0, LANES - GLA_GATE_RANK), (0, 0)))
    row = lambda a: a.reshape(1, -1)
    return {
        'g_pre': row(p['g_pre'][l]), 'w_in': w_in_packed.astype(bf16), 'g_qa': row(p['g_qa'][l]),
        'w_uq': _pad_head(p['w_uq'][l]).reshape(Q_RANK, QK_PACKED).astype(bf16),
        'g_kva': row(p['g_kva'][l]), 'w_uk': wuk_packed.astype(bf16),
        'g_q': row(_pad_head(p['g_mla_q'][l])), 'g_k': row(_pad_head(g_k)),
        'w_gk': w_gk.astype(bf16), 'b_gk': row(p['b_gk'][l]),
        'g_mem_q': row(jnp.tile(p['g_mem_q'][l], 2)), 'g_mem_k': row(jnp.tile(p['g_mem_k'][l], 2)),
        'g_gla_o': row(jnp.tile(p['g_gla_o'][l], 2)),
        'w_uk_t': w_uk_t.astype(bf16), 'w_uk_g': w_uk_g.astype(bf16),
        'g_k_rope': g_k[NOPE_DIM:].reshape(ROPE_DIM, 1),
        'w_uv_pairs': w_uv_pairs.astype(bf16),
        'w_uv_t': w_uv.transpose(0, 2, 1).astype(bf16),
        'g_mem': row(p['g_mem'][l]), 'w_mem_kv': p['w_mem_kv'][l].astype(bf16),
        'w_out': p['w_out'][l].astype(bf16),
    }


def _angles(pos):
    inv_freq = ROPE_THETA ** (-jnp.arange(HALF_ROPE, dtype=f32) * (2.0 / ROPE_DIM))
    return pos.astype(f32)[:, None] * inv_freq[None, :]


def _packed_tables(pos):
    ang = _angles(pos)
    n = pos.shape[0]
    cos, sin = jnp.cos(ang), jnp.sin(ang)
    one = jnp.ones((n, NOPE_DIM), f32)
    z = lambda w: jnp.zeros((n, w), f32)
    tail = HEAD_PAD - QK_DIM
    cos_p = jnp.concatenate([one, cos, cos, z(tail)], axis=1)
    sin_up = jnp.concatenate([z(NOPE_DIM + HALF_ROPE), sin, z(tail)], axis=1)
    sin_dn = jnp.concatenate([z(NOPE_DIM), -sin, z(HALF_ROPE + tail)], axis=1)
    return cos_p, sin_up, sin_dn


def _transposed_tables(pos):
    ang = _angles(pos).T
    return jnp.cos(ang), jnp.sin(ang)


def _t_to_state(st):
    B = st.shape[0]
    s5 = st.reshape(B, GLA_HEADS, GLA_DV, GLA_HEADS, GLA_DK)
    diag = jnp.stack([s5[:, h, :, h, :] for h in range(GLA_HEADS)], axis=1)
    return diag.transpose(0, 1, 3, 2)


def kernel(x_prompt, x_sample, mem_prompt, cache_ckv, cache_kpe, page_table, state_gla, cache_mem_k, cache_mem_v, g_pre, w_in, g_qa, w_uq, g_kva, w_uk, w_uv, g_mla_q, g_mla_k, w_gk, b_gk, g_gla_o, g_mem, w_mem_kv, g_mem_q, g_mem_k, w_out):
    params = dict(g_pre=g_pre, w_in=w_in, g_qa=g_qa, w_uq=w_uq, g_kva=g_kva, w_uk=w_uk, w_uv=w_uv,
                  g_mla_q=g_mla_q, g_mla_k=g_mla_k, w_gk=w_gk, b_gk=b_gk, g_gla_o=g_gla_o, g_mem=g_mem,
                  w_mem_kv=w_mem_kv, g_mem_q=g_mem_q, g_mem_k=g_mem_k, w_out=w_out)
    nb, seq, _ = x_prompt.shape
    ndb, t_new, _ = x_sample.shape
    n_mem = mem_prompt.shape[1]
    depth = w_in.shape[0]
    n_pages = page_table.shape[1]
    past_len = n_pages * cache_ckv.shape[2]

    tm_p = min(256, seq)
    tm_merge = min(512, seq)
    tq = min(256, seq)
    tm_s = min(256, ndb * t_new)
    pp = max(d for d in range(1, 17) if (n_pages + 1) % d == 0)
    gla_tt = min(128, seq)
    gla_cs = min(16, gla_tt)
    nb_s = min(8, ndb)
    nbp = min(4, nb)
    nb_gla = min(8, nb)

    cache_kpe_t = cache_kpe.transpose(0, 1, 3, 2)
    to_t = lambda a: a.transpose(0, 1, 3, 4, 2).reshape(depth, ndb, MEM_WIDTH, n_mem)
    cache_mkt, cache_mvt = to_t(cache_mem_k), to_t(cache_mem_v)
    from_t = lambda a: a.reshape(nb, MEM_HEADS, MEM_HEAD_DIM, n_mem).transpose(0, 3, 1, 2)

    tabs_p = _packed_tables(jnp.arange(seq, dtype=jnp.int32))
    pos_s = past_len + jnp.arange(t_new, dtype=jnp.int32)
    tabs_s = tuple(jnp.tile(t, (tm_s // t_new, 1)) for t in _packed_tables(pos_s))
    tabs_t = _transposed_tables(jnp.arange(past_len + PAGE_SIZE, dtype=jnp.int32))

    xp = x_prompt.reshape(nb * seq, D_MODEL)
    xs = x_sample.reshape(ndb * t_new, D_MODEL)
    mem2 = mem_prompt.reshape(nb * n_mem, D_MODEL)
    ckv_p, kpe_p, gla_p, mk_p, mv_p, ckv_s, kpe_s, gla_s = ([] for _ in range(8))
    for l in range(depth):
        w = _prep_layer(l, params)
        q, k, c, kpe, gates, gq, gk, gv, gg, mq = _inproj(xp, tabs_p, w, tm_p)
        r3 = lambda a: a.reshape(nb, seq, a.shape[-1])
        mla_o = _mla_prompt(r3(q), r3(k), r3(c), w['w_uv_t'], nb, seq, tq, nbp).reshape(nb * seq, MLA_WIDTH)
        gla_o, st = _gla(r3(gq), r3(gk), r3(gv), r3(gg), nb_gla, gla_tt, gla_cs)
        mkt, mvt = _memkv(mem2, w, nb, n_mem)
        xp = _merge_memattn(xp, mla_o, gla_o.reshape(nb * seq, GLA_WIDTH), mq, mkt, mvt, gates, w, tm_merge, seq)
        ckv_p.append(c.reshape(nb, seq, KV_RANK))
        kpe_p.append(kpe.reshape(nb, seq, ROPE_DIM))
        gla_p.append(_t_to_state(st))
        mk_p.append(from_t(mkt))
        mv_p.append(from_t(mvt))
        q, k, c, kpe, gates, gq, gk, gv, gg, mq = _inproj(xs, tabs_s, w, tm_s)
        mla_o = _mla_sample(l, q, c, kpe, cache_ckv, cache_kpe_t, page_table, w, tabs_t, pp)
        r3 = lambda a: a.reshape(ndb, t_new, a.shape[-1])
        gla_o, st = _gla_decode(r3(gq), r3(gk), r3(gv), r3(gg), state_gla[l].transpose(1, 2, 3, 0))
        mem_o = _memattn(l, r3(mq), cache_mkt, cache_mvt, nb_s, t_new)
        xs = _merge(xs, mla_o, gla_o.reshape(ndb * t_new, GLA_WIDTH), mem_o.reshape(ndb * t_new, MEM_WIDTH), gates, w, tm_s)
        ckv_s.append(c.reshape(ndb, t_new, KV_RANK))
        kpe_s.append(kpe.reshape(ndb, t_new, ROPE_DIM))
        gla_s.append(st.transpose(3, 0, 1, 2))
    return (xp.reshape(nb, seq, D_MODEL), xs.reshape(ndb, t_new, D_MODEL), jnp.stack(ckv_p), jnp.stack(kpe_p),
            jnp.stack(gla_p), jnp.stack(mk_p), jnp.stack(mv_p), jnp.stack(ckv_s), jnp.stack(kpe_s), jnp.stack(gla_s))
```

```python
import functools

import jax
import jax.numpy as jnp
from jax import lax
from jax.experimental import pallas as pl
from jax.experimental.pallas import tpu as pltpu

f32, bf16 = jnp.float32, jnp.bfloat16

D_MODEL = 1024
PAGE_SIZE = 128
MLA_HEADS = 8
NOPE_DIM = 64
ROPE_DIM = 32
HALF_ROPE = ROPE_DIM // 2
QK_DIM = NOPE_DIM + ROPE_DIM
V_DIM = 64
Q_RANK = 384
KV_RANK = 256
MLA_WIDTH = MLA_HEADS * V_DIM
ROPE_THETA = 10000.0
MLA_SCALE = QK_DIM ** -0.5
LOG2_E = 1.4426950408889634
GLA_HEADS = 4
GLA_DK = 32
GLA_DV = 64
GLA_KDIM = GLA_HEADS * GLA_DK
GLA_WIDTH = GLA_HEADS * GLA_DV
GLA_GATE_RANK = 16
GLA_NORMALIZER = 16.0
MEM_HEADS = 4
MEM_HEAD_DIM = 64
MEM_WIDTH = MEM_HEADS * MEM_HEAD_DIM
MEM_SCALE = MEM_HEAD_DIM ** -0.5
D_MIX = MLA_WIDTH + GLA_WIDTH + MEM_WIDTH
EPS = 1e-6
NEG_INF = -1e30

LANES = 128
HEAD_PAD = LANES
QK_PACKED = MLA_HEADS * HEAD_PAD

OFF_CQ = 0
OFF_CKV = OFF_CQ + Q_RANK
OFF_GATE_MLA = OFF_CKV + KV_RANK
OFF_GQ = OFF_GATE_MLA + MLA_WIDTH
OFF_GK = OFF_GQ + GLA_KDIM
OFF_GV = OFF_GK + GLA_KDIM
OFF_GATE_GLA = OFF_GV + GLA_WIDTH
OFF_MQ = OFF_GATE_GLA + GLA_WIDTH
OFF_GATE_MEM = OFF_MQ + MEM_WIDTH
OFF_MISC = OFF_GATE_MEM + MEM_WIDTH
D_IN_PACKED = OFF_MISC + LANES
KPE_LANE = NOPE_DIM

VMEM_LIMIT = 56 * 1024 * 1024
PAGE_BUFFERS = 3


def _cparams(*sem):
    return pltpu.CompilerParams(dimension_semantics=sem, vmem_limit_bytes=VMEM_LIMIT)


def _nt(a, b):
    return lax.dot_general(a, b, (((1,), (1,)), ((), ())), preferred_element_type=f32)


def _tn(a, b):
    return lax.dot_general(a, b, (((0,), (0,)), ((), ())), preferred_element_type=f32)


def _mm(a, b):
    return jnp.dot(a, b, preferred_element_type=f32)


def _rms_rows(x, g):
    return x * lax.rsqrt(jnp.mean(x * x, axis=-1, keepdims=True) + EPS) * g


def _half_head_rms(blk, g, lane):
    sq = blk * blk
    lo = lane < 64
    ss_lo = jnp.sum(jnp.where(lo, sq, 0.0), axis=-1, keepdims=True)
    ss_hi = jnp.sum(jnp.where(lo, 0.0, sq), axis=-1, keepdims=True)
    ss = jnp.where(lo, ss_lo, ss_hi)
    return blk * lax.rsqrt(ss * (1.0 / 64.0) + EPS) * g


def _rope_packed(x, cos, sin_up, sin_dn):
    return x * cos + pltpu.roll(x, HALF_ROPE, 1) * sin_up + pltpu.roll(x, LANES - HALF_ROPE, 1) * sin_dn


def _inproj_body(x_ref, gpre_ref, win_ref, gqa_ref, wuq_ref, gkva_ref, wuk_ref, gq_ref, gk_ref,
                 cos_ref, sup_ref, sdn_ref, wgk_ref, bgk_ref, gmq_ref,
                 q_out, k_out, ckv_out, kpe_out, gates_out, glaq_out, glak_out, glav_out, glag_out, memq_out):
    x = x_ref[...]
    xb = _rms_rows(x, gpre_ref[...]).astype(bf16)

    def seg(off, n):
        return _mm(xb, win_ref[:, off:off + n])

    cos, sup, sdn = cos_ref[...], sup_ref[...], sdn_ref[...]
    lane = lax.broadcasted_iota(jnp.int32, (1, LANES), 1)

    cq = _rms_rows(seg(OFF_CQ, Q_RANK), gqa_ref[...]).astype(bf16)
    qf = _mm(cq, wuq_ref[...])
    gq = gq_ref[...]
    for h in range(MLA_HEADS):
        qh = qf[:, h * HEAD_PAD:(h + 1) * HEAD_PAD]
        ss = jnp.sum(qh * qh, axis=-1, keepdims=True)
        qn = qh * lax.rsqrt(ss * (1.0 / QK_DIM) + EPS) * gq
        q_out[:, h * HEAD_PAD:(h + 1) * HEAD_PAD] = (_rope_packed(qn, cos, sup, sdn) * (MLA_SCALE * LOG2_E)).astype(bf16)

    c = _rms_rows(seg(OFF_CKV, KV_RANK), gkva_ref[...])
    ckv_out[...] = c
    misc = seg(OFF_MISC, LANES)
    kpe_wide = jnp.where((lane >= KPE_LANE) & (lane < KPE_LANE + ROPE_DIM), misc, 0.0)
    kpe_out[...] = misc[:, KPE_LANE:KPE_LANE + ROPE_DIM]
    kf = _mm(c.astype(bf16), wuk_ref[...])
    gk = gk_ref[...]
    for h in range(MLA_HEADS):
        kh = kf[:, h * HEAD_PAD:(h + 1) * HEAD_PAD] + kpe_wide
        ss = jnp.sum(kh * kh, axis=-1, keepdims=True)
        kn = kh * lax.rsqrt(ss * (1.0 / QK_DIM) + EPS) * gk
        k_out[:, h * HEAD_PAD:(h + 1) * HEAD_PAD] = _rope_packed(kn, cos, sup, sdn).astype(bf16)

    g1 = seg(OFF_GATE_MLA, MLA_WIDTH)
    gates_out[:, 0:MLA_WIDTH] = g1 * jax.nn.sigmoid(g1)
    g2 = seg(OFF_GATE_GLA, GLA_WIDTH)
    gates_out[:, MLA_WIDTH:MLA_WIDTH + GLA_WIDTH] = g2 * jax.nn.sigmoid(g2)
    g3 = seg(OFF_GATE_MEM, MEM_WIDTH)
    gates_out[:, MLA_WIDTH + GLA_WIDTH:D_MIX] = g3 * jax.nn.sigmoid(g3)

    glaq_out[...] = seg(OFF_GQ, GLA_KDIM) * (GLA_DK ** -0.5)
    glak_out[...] = seg(OFF_GK, GLA_KDIM)
    glav_out[...] = seg(OFF_GV, GLA_WIDTH)
    gl = _mm(misc.astype(bf16), wgk_ref[...]) + bgk_ref[...]
    glag_out[...] = jax.nn.log_sigmoid(gl) * (LOG2_E / GLA_NORMALIZER)

    mq = seg(OFF_MQ, MEM_WIDTH)
    gmq = gmq_ref[...]
    for j in range(MEM_WIDTH // LANES):
        blk = _half_head_rms(mq[:, j * LANES:(j + 1) * LANES], gmq, lane)
        memq_out[:, j * LANES:(j + 1) * LANES] = (blk * (MEM_SCALE * LOG2_E)).astype(bf16)


def _inproj(x2, tabs, w, tm):
    T = x2.shape[0]
    nt = tabs[0].shape[0] // tm
    row = lambda i: (i, 0)
    fix = lambda i: (0, 0)
    tab = lambda i: (i % nt, 0)

    def full(a):
        return pl.BlockSpec(a.shape, fix)

    consts1 = (w['g_pre'], w['w_in'], w['g_qa'], w['w_uq'], w['g_kva'], w['w_uk'], w['g_q'], w['g_k'])
    consts2 = (w['w_gk'], w['b_gk'], w['g_mem_q'])
    outs = [(QK_PACKED, bf16), (QK_PACKED, bf16), (KV_RANK, f32), (ROPE_DIM, f32), (D_MIX, f32),
            (GLA_KDIM, f32), (GLA_KDIM, f32), (GLA_WIDTH, f32), (GLA_KDIM, f32), (MEM_WIDTH, bf16)]
    return pl.pallas_call(
        _inproj_body,
        grid=(T // tm,),
        in_specs=[pl.BlockSpec((tm, D_MODEL), row)] + [full(a) for a in consts1]
        + [pl.BlockSpec((tm, LANES), tab)] * 3 + [full(a) for a in consts2],
        out_specs=[pl.BlockSpec((tm, n), row) for n, _ in outs],
        out_shape=[jax.ShapeDtypeStruct((T, n), dt) for n, dt in outs],
        compiler_params=_cparams("arbitrary"),
        name="inproj",
    )(x2, *consts1, *tabs, *consts2)


def _mla_prompt_body(q_ref, k_ref, c_ref, wuvt_ref, o_ref, ct_scr, acc_scr, m_scr, l_scr, ot_scr, *, tq, nbp):
    i = pl.program_id(1)
    nblk = ct_scr.shape[1]

    @pl.when(i == 0)
    def _():
        for bb in range(nbp):
            for j in range(nblk):
                ct_scr[bb, j] = c_ref[bb, j * tq:(j + 1) * tq, :].T.astype(bf16)

    m_scr[...] = jnp.full(m_scr.shape, NEG_INF, f32)
    l_scr[...] = jnp.zeros(l_scr.shape, f32)
    acc_scr[...] = jnp.zeros(acc_scr.shape, f32)
    key = lax.broadcasted_iota(jnp.int32, (tq, tq), 0)
    qry = lax.broadcasted_iota(jnp.int32, (tq, tq), 1)
    causal = key <= qry

    def step(kb, masked):
        ks = pl.ds(pl.multiple_of(kb * tq, tq), tq)
        for h in range(MLA_HEADS):
            hs = slice(h * HEAD_PAD, (h + 1) * HEAD_PAD)
            for bb in range(nbp):
                s = _nt(k_ref[bb, ks, hs], q_ref[bb, :, hs])
                if masked:
                    s = jnp.where(causal, s, NEG_INF)
                m_old = m_scr[bb, h:h + 1, :]
                m_new = jnp.maximum(m_old, jnp.max(s, axis=0, keepdims=True))
                alpha = jnp.exp2(m_old - m_new)
                p = jnp.exp2(s - m_new)
                l_scr[bb, h:h + 1, :] = l_scr[bb, h:h + 1, :] * alpha + jnp.sum(p, axis=0, keepdims=True)
                acc_scr[bb, h] = acc_scr[bb, h] * alpha + _mm(ct_scr[bb, kb], p.astype(bf16))
                m_scr[bb, h:h + 1, :] = m_new

    def body(kb, carry):
        step(kb, False)
        return carry

    lax.fori_loop(0, i, body, 0)
    step(i, True)

    for bb in range(nbp):
        for h in range(MLA_HEADS):
            lat_t = (acc_scr[bb, h] * (1.0 / l_scr[bb, h:h + 1, :])).astype(bf16)
            ot_scr[h * V_DIM:(h + 1) * V_DIM, :] = _mm(wuvt_ref[h], lat_t)
        o_ref[bb] = ot_scr[...].T


def _mla_prompt(q, k, c, wuv_t, nb, seq, tq, nbp):
    nq = seq // tq
    return pl.pallas_call(
        functools.partial(_mla_prompt_body, tq=tq, nbp=nbp),
        grid=(nb // nbp, nq),
        in_specs=[pl.BlockSpec((nbp, tq, QK_PACKED), lambda b, i: (b, i, 0)),
                  pl.BlockSpec((nbp, seq, QK_PACKED), lambda b, i: (b, 0, 0), pipeline_mode=pl.Buffered(1)),
                  pl.BlockSpec((nbp, seq, KV_RANK), lambda b, i: (b, 0, 0), pipeline_mode=pl.Buffered(1)),
                  pl.BlockSpec(wuv_t.shape, lambda b, i: (0, 0, 0))],
        out_specs=pl.BlockSpec((nbp, tq, MLA_WIDTH), lambda b, i: (b, i, 0)),
        out_shape=jax.ShapeDtypeStruct((nb, seq, MLA_WIDTH), f32),
        scratch_shapes=[pltpu.VMEM((nbp, nq, KV_RANK, tq), bf16), pltpu.VMEM((nbp, MLA_HEADS, KV_RANK, tq), f32),
                        pltpu.VMEM((nbp, MLA_HEADS, tq), f32), pltpu.VMEM((nbp, MLA_HEADS, tq), f32),
                        pltpu.VMEM((MLA_WIDTH, tq), f32)],
        compiler_params=_cparams("arbitrary", "arbitrary"),
        name="mla_prompt",
    )(q, k, c, wuv_t)


def _mla_sample_body(pt_ref, q_ref, cnew_ref, kpenewt_ref, wukt_ref, wukg_ref, gkr_ref, cost_ref, sint_ref, wuv_ref,
                     ckv_hbm, kpe_hbm, o_ref,
                     cbuf, kbuf, sem, lhs_scr, qr_scr, cb_scr, s_scr, ql_all, qr_all, acc_scr, m_scr, l_scr,
                     *, layer, gp, n_pages, t_new):
    ndb = ql_all.shape[0]
    ng = (n_pages + 1) // gp
    kb = gp * PAGE_SIZE
    total = ndb * ng
    nrow = MLA_HEADS * t_new
    n_nope = MLA_HEADS * NOPE_DIM
    new_rows = pl.ds(kb - PAGE_SIZE, t_new)

    def group_copies(t):
        b, g, slot = t // ng, t % ng, t % PAGE_BUFFERS
        cps = []
        for i in range(gp):
            page = pt_ref[b * n_pages + jnp.minimum(g * gp + i, n_pages - 1)]
            rows = pl.ds(i * PAGE_SIZE, PAGE_SIZE)
            cps.append(pltpu.make_async_copy(ckv_hbm.at[layer, page], cbuf.at[slot, rows, :], sem.at[0, slot]))
            cps.append(pltpu.make_async_copy(kpe_hbm.at[layer, page], kbuf.at[slot, i], sem.at[1, slot]))
        return cps

    def reset_state():
        m_scr[...] = jnp.full((nrow, 1), NEG_INF, f32)
        l_scr[...] = jnp.zeros((nrow, 1), f32)
        acc_scr[...] = jnp.zeros((nrow, KV_RANK), f32)

    gkr = gkr_ref[...]
    key_lane = lax.broadcasted_iota(jnp.int32, (nrow, PAGE_SIZE), 1)
    qry_row = lax.broadcasted_iota(jnp.int32, (nrow, PAGE_SIZE), 0) % t_new

    def prepare_group(t):
        b, g, pslot = t // ng, t % ng, t % PAGE_BUFFERS
        lhs_scr[n_nope:, :] = ql_all[b].astype(bf16)
        qr_scr[...] = qr_all[b].astype(bf16)
        for cp in group_copies(t):
            cp.wait()
        is_last = g == ng - 1
        cbuf[pslot, new_rows, :] = jnp.where(is_last, cnew_ref[b], cbuf[pslot, new_rows, :])
        kbuf[pslot, gp - 1] = jnp.where(is_last, kpenewt_ref[b], kbuf[pslot, gp - 1])

    def score_group(t, slot):
        g, pslot = t % ng, t % PAGE_BUFFERS
        is_last = g == ng - 1
        cb = cbuf[pslot].astype(bf16)
        cb_scr[slot] = cb
        kt = jnp.concatenate([kbuf[pslot, i] for i in range(gp)], axis=1)
        cos_t, sin_t = cost_ref[g], sint_ref[g]
        big = _nt(lhs_scr[...], cb)
        kn = big[0:n_nope]
        ss_nope = jnp.sum((kn * kn).reshape(MLA_HEADS, NOPE_DIM, kb), axis=1)
        ss = ss_nope + jnp.sum(kt * kt, axis=0, keepdims=True)
        r = lax.rsqrt(ss * (1.0 / QK_DIM) + EPS)
        kg = kt * gkr
        k1, k2 = kg[0:HALF_ROPE], kg[HALF_ROPE:ROPE_DIM]
        kr = jnp.concatenate([k1 * cos_t - k2 * sin_t, k2 * cos_t + k1 * sin_t], axis=0).astype(bf16)
        s = big[n_nope:] + _mm(qr_scr[...], kr)
        s = (s.reshape(MLA_HEADS, t_new, kb) * r[:, None, :]).reshape(nrow, kb)
        first_dead = jnp.where(is_last, 1, PAGE_SIZE + t_new)
        tail = jnp.where(key_lane >= qry_row + first_dead, NEG_INF, s[:, kb - PAGE_SIZE:])
        s_scr[slot] = jnp.concatenate([s[:, :kb - PAGE_SIZE], tail], axis=1)

    def value_group(slot):
        s = s_scr[slot]
        m_old = m_scr[...]
        m_new = jnp.maximum(m_old, jnp.max(s, axis=-1, keepdims=True))
        alpha = jnp.exp2(m_old - m_new)
        p = jnp.exp2(s - m_new)
        l_scr[...] = l_scr[...] * alpha + jnp.sum(p, axis=-1, keepdims=True)
        acc_scr[...] = acc_scr[...] * alpha + _mm(p.astype(bf16), cb_scr[slot])
        m_scr[...] = m_new

    def finish_batch(b, closed):
        ql_all[b] = jnp.where(closed, acc_scr[...] * (1.0 / l_scr[...]), ql_all[b])
        m_scr[...] = jnp.where(closed, NEG_INF, m_scr[...])
        l_scr[...] = jnp.where(closed, 0.0, l_scr[...])
        acc_scr[...] = jnp.where(closed, 0.0, acc_scr[...])

    lhs_scr[0:n_nope, :] = wukt_ref[...]
    for h in range(MLA_HEADS):
        q_h = q_ref[:, h * HEAD_PAD:(h + 1) * HEAD_PAD]
        rows = slice(h * t_new, (h + 1) * t_new)
        ql_all[:, rows, :] = _mm(q_h, wukg_ref[h]).reshape(ndb, t_new, KV_RANK)
        qr_all[:, rows, :] = q_h[:, KPE_LANE:KPE_LANE + ROPE_DIM].astype(f32).reshape(ndb, t_new, ROPE_DIM)
    reset_state()

    ahead = PAGE_BUFFERS - 1

    def skewed_step(t, slot, prefetch):
        if prefetch:
            for cp in group_copies(t + ahead):
                cp.start()
        prepare_group(t)
        score_group(t, slot)
        value_group(1 - slot)
        closes = t % ng == 0
        finish_batch(jnp.maximum(t // ng - 1, 0), closes)

    for t in range(min(ahead, total)):
        for cp in group_copies(t):
            cp.start()
    if total > ahead:
        for cp in group_copies(ahead):
            cp.start()
    prepare_group(0)
    score_group(0, 0)

    n_pairs = max(total - 1 - ahead, 0) // 2

    def body(k, carry):
        skewed_step(2 * k + 1, 1, True)
        skewed_step(2 * k + 2, 0, True)
        return carry

    lax.fori_loop(0, n_pairs, body, 0)
    for t in range(2 * n_pairs + 1, total):
        skewed_step(t, t % 2, t + ahead < total)
    value_group((total - 1) % 2)
    finish_batch(ndb - 1, True)

    for h in range(MLA_HEADS):
        lat_h = ql_all[:, h * t_new:(h + 1) * t_new, :].reshape(ndb * t_new, KV_RANK).astype(bf16)
        contrib = _mm(lat_h, wuv_ref[h])
        ps = slice((h // 2) * LANES, (h // 2 + 1) * LANES)
        if h % 2 == 0:
            o_ref[:, ps] = contrib
        else:
            o_ref[:, ps] += contrib


def _mla_sample(layer, q, c_new, kpe_new, cache_ckv, cache_kpe_t, page_table, w, tabs_t, gp):
    ndb, n_pages = page_table.shape
    t_new = q.shape[0] // ndb
    ng = (n_pages + 1) // gp
    kb = gp * PAGE_SIZE
    nrow = MLA_HEADS * t_new
    cost, sint = tabs_t
    by_group = lambda a: a.reshape(HALF_ROPE, ng, kb).transpose(1, 0, 2)
    kpe_new_t = jnp.pad(kpe_new.reshape(ndb, t_new, ROPE_DIM).transpose(0, 2, 1),
                        ((0, 0), (0, 0), (0, PAGE_SIZE - t_new)))
    args = (q, c_new.reshape(ndb, t_new, KV_RANK), kpe_new_t, w['w_uk_t'], w['w_uk_g'], w['g_k_rope'],
            by_group(cost), by_group(sint), w['w_uv_pairs'])

    def full(a):
        return pl.BlockSpec(a.shape, lambda i, pt, nd=a.ndim: (0,) * nd)

    return pl.pallas_call(
        functools.partial(_mla_sample_body, layer=layer, gp=gp, n_pages=n_pages, t_new=t_new),
        grid_spec=pltpu.PrefetchScalarGridSpec(
            num_scalar_prefetch=1,
            grid=(1,),
            in_specs=[full(a) for a in args] + [pl.BlockSpec(memory_space=pl.ANY)] * 2,
            out_specs=pl.BlockSpec((ndb * t_new, MLA_WIDTH), lambda i, pt: (0, 0)),
            scratch_shapes=[pltpu.VMEM((PAGE_BUFFERS, kb, KV_RANK), f32),
                            pltpu.VMEM((PAGE_BUFFERS, gp, ROPE_DIM, PAGE_SIZE), f32),
                            pltpu.SemaphoreType.DMA((2, PAGE_BUFFERS)),
                            pltpu.VMEM((MLA_HEADS * NOPE_DIM + nrow, KV_RANK), bf16),
                            pltpu.VMEM((nrow, ROPE_DIM), bf16),
                            pltpu.VMEM((2, kb, KV_RANK), bf16),
                            pltpu.VMEM((2, nrow, kb), f32),
                            pltpu.VMEM((ndb, nrow, KV_RANK), f32),
                            pltpu.VMEM((ndb, nrow, ROPE_DIM), f32),
                            pltpu.VMEM((nrow, KV_RANK), f32),
                            pltpu.VMEM((nrow, 1), f32),
                            pltpu.VMEM((nrow, 1), f32)]),
        out_shape=jax.ShapeDtypeStruct((ndb * t_new, MLA_WIDTH), f32),
        compiler_params=_cparams("arbitrary"),
        name="mla_sample",
    )(page_table.reshape(-1), *args, cache_ckv, cache_kpe_t)


def _split3(x):
    hi = x.astype(bf16)
    r1 = x - hi.astype(f32)
    mid = r1.astype(bf16)
    lo = (r1 - mid.astype(f32)).astype(bf16)
    return hi, mid, lo


def _gla_body(q_ref, k_ref, v_ref, g_ref, o_ref, sT_out, st_scr, *, nb, tt, cs):
    ti = pl.program_id(1)
    nchunk = tt // cs

    @pl.when(ti == 0)
    def _():
        st_scr[...] = jnp.zeros(st_scr.shape, f32)

    tri = (lax.broadcasted_iota(jnp.int32, (tt, tt), 1) <= lax.broadcasted_iota(jnp.int32, (tt, tt), 0)).astype(bf16)
    kv_head = (lax.broadcasted_iota(jnp.int32, (GLA_KDIM, GLA_WIDTH), 0) // GLA_DK
               == lax.broadcasted_iota(jnp.int32, (GLA_KDIM, GLA_WIDTH), 1) // GLA_DV)
    block_ones = kv_head.astype(bf16)
    vk_head = (lax.broadcasted_iota(jnp.int32, (GLA_WIDTH, GLA_KDIM), 0) // GLA_DV
               == lax.broadcasted_iota(jnp.int32, (GLA_WIDTH, GLA_KDIM), 1) // GLA_DK)
    sel_t = (lax.broadcasted_iota(jnp.int32, (cs, cs * cs), 1) // cs
             == lax.broadcasted_iota(jnp.int32, (cs, cs * cs), 0)).astype(bf16)
    t_idx = lax.broadcasted_iota(jnp.int32, (cs, cs, GLA_KDIM), 0)
    s_idx = lax.broadcasted_iota(jnp.int32, (cs, cs, GLA_KDIM), 1)
    causal3 = s_idx <= t_idx

    cums = []
    for b in range(nb):
        hi, mid, lo = _split3(g_ref[b])
        cums.append(_mm(tri, hi) + _mm(tri, mid) + _mm(tri, lo))
    for n in range(nchunk):
        sl = slice(n * cs, (n + 1) * cs)
        for b in range(nb):
            cum = cums[b]
            base = cum[n * cs - 1:n * cs] if n > 0 else jnp.zeros((1, GLA_KDIM), f32)
            bc = cum[sl] - base
            b_last = bc[cs - 1:cs]
            qc, kc, vc = q_ref[b, sl, :], k_ref[b, sl, :], v_ref[b, sl, :]
            st = st_scr[b]
            o_inter = _nt((qc * jnp.exp2(bc)).astype(bf16), st.astype(bf16))
            diff = jnp.where(causal3, bc[:, None, :] - bc[None, :, :], NEG_INF)
            d3 = qc[:, None, :] * kc[None, :, :] * jnp.exp2(diff)
            a_exp = _mm(d3.reshape(cs * cs, GLA_KDIM).astype(bf16), block_ones)
            xv = (a_exp.reshape(cs, cs, GLA_WIDTH) * vc[None, :, :]).reshape(cs * cs, GLA_WIDTH)
            o_intra = _mm(sel_t, xv.astype(bf16))
            o_ref[b, sl, :] = o_inter + o_intra
            kd = (kc * jnp.exp2(b_last - bc)).astype(bf16)
            upd = _tn(vc.astype(bf16), kd)
            st_scr[b] = st * jnp.exp2(b_last) + jnp.where(vk_head, upd, 0.0)

    @pl.when(ti == pl.num_programs(1) - 1)
    def _():
        sT_out[...] = st_scr[...]


def _gla(q, k, v, g, nb, tt, cs):
    B, L, _ = q.shape
    tok = lambda bi, ti: (bi, ti, 0)
    st = lambda bi, ti: (bi, 0, 0)
    return pl.pallas_call(
        functools.partial(_gla_body, nb=nb, tt=tt, cs=cs),
        grid=(B // nb, L // tt),
        in_specs=[pl.BlockSpec((nb, tt, GLA_KDIM), tok), pl.BlockSpec((nb, tt, GLA_KDIM), tok),
                  pl.BlockSpec((nb, tt, GLA_WIDTH), tok), pl.BlockSpec((nb, tt, GLA_KDIM), tok)],
        out_specs=[pl.BlockSpec((nb, tt, GLA_WIDTH), tok), pl.BlockSpec((nb, GLA_WIDTH, GLA_KDIM), st)],
        out_shape=[jax.ShapeDtypeStruct((B, L, GLA_WIDTH), f32), jax.ShapeDtypeStruct((B, GLA_WIDTH, GLA_KDIM), f32)],
        scratch_shapes=[pltpu.VMEM((nb, GLA_WIDTH, GLA_KDIM), f32)],
        compiler_params=_cparams("arbitrary", "arbitrary"),
        name="gla",
    )(q, k, v, g)


def _gla_decode_body(q_ref, k_ref, v_ref, g_ref, s0_ref, o_ref, s_out, qt_scr, kt_scr, et_scr, vt_scr, ot_scr):
    nb, t_new, _ = q_ref.shape
    for t in range(t_new):
        qt_scr[t] = q_ref[:, t, :].T
        kt_scr[t] = k_ref[:, t, :].T
        et_scr[t] = jnp.exp2(g_ref[:, t, :].T)
        vt_scr[t] = v_ref[:, t, :].T
    ot_scr[...] = jnp.zeros(ot_scr.shape, f32)

    for h in range(GLA_HEADS):
        vs = slice(h * GLA_DV, (h + 1) * GLA_DV)

        def one_k(kk, carry):
            row = pl.ds(h * GLA_DK + kk, 1)
            s = s0_ref[h, kk]
            for t in range(t_new):
                s = s * et_scr[t, row, :] + kt_scr[t, row, :] * vt_scr[t, vs, :]
                ot_scr[t, vs, :] += qt_scr[t, row, :] * s
            s_out[h, kk] = s
            return carry

        lax.fori_loop(0, GLA_DK, one_k, 0)

    for t in range(t_new):
        o_ref[:, t, :] = ot_scr[t].T


def _gla_decode(q, k, v, g, s0):
    B, T, _ = q.shape
    full = lambda a: pl.BlockSpec(a.shape, lambda i, nd=a.ndim: (0,) * nd)
    o_shape = jax.ShapeDtypeStruct((B, T, GLA_WIDTH), f32)
    s_shape = jax.ShapeDtypeStruct(s0.shape, f32)
    return pl.pallas_call(
        _gla_decode_body,
        grid=(1,),
        in_specs=[full(a) for a in (q, k, v, g, s0)],
        out_specs=[full(o_shape), full(s_shape)],
        out_shape=[o_shape, s_shape],
        scratch_shapes=[pltpu.VMEM((T, GLA_KDIM, B), f32)] * 3 + [pltpu.VMEM((T, GLA_WIDTH, B), f32)] * 2,
        compiler_params=_cparams("arbitrary"),
        name="gla_decode",
    )(q, k, v, g, s0)


def _memkv_body(mem_ref, gmem_ref, w_ref, gk_ref, mkt_out, mvt_out):
    xb = _rms_rows(mem_ref[...], gmem_ref[...]).astype(bf16)
    kv = _mm(xb, w_ref[...])
    lane = lax.broadcasted_iota(jnp.int32, (1, LANES), 1)
    gk = gk_ref[...]
    for j in range(MEM_WIDTH // LANES):
        blk = _half_head_rms(kv[:, j * LANES:(j + 1) * LANES], gk, lane)
        mkt_out[j * LANES:(j + 1) * LANES, :] = blk.T
        mvt_out[j * LANES:(j + 1) * LANES, :] = kv[:, MEM_WIDTH + j * LANES:MEM_WIDTH + (j + 1) * LANES].T


def _memkv(mem2, w, nb, n_mem):
    row = lambda i: (i, 0)
    fix = lambda i: (0, 0)
    return pl.pallas_call(
        _memkv_body,
        grid=(nb,),
        in_specs=[pl.BlockSpec((n_mem, D_MODEL), row), pl.BlockSpec(w['g_mem'].shape, fix),
                  pl.BlockSpec(w['w_mem_kv'].shape, fix), pl.BlockSpec(w['g_mem_k'].shape, fix)],
        out_specs=[pl.BlockSpec((None, MEM_WIDTH, n_mem), lambda i: (i, 0, 0))] * 2,
        out_shape=[jax.ShapeDtypeStruct((nb, MEM_WIDTH, n_mem), f32)] * 2,
        compiler_params=_cparams("arbitrary"),
        name="memkv",
    )(mem2, w['g_mem'], w['w_mem_kv'], w['g_mem_k'])


def _mem_attend(q, mkt, mvt):
    tq = q.shape[0]
    lane_head = lax.broadcasted_iota(jnp.int32, (1, MEM_WIDTH), 1) // MEM_HEAD_DIM
    qs = jnp.concatenate([jnp.where(lane_head == h, q, jnp.zeros_like(q)) for h in range(MEM_HEADS)], axis=0)
    s = _mm(qs, mkt.astype(bf16))
    p = jnp.exp2(s - jnp.max(s, axis=-1, keepdims=True))
    pv = _nt(p.astype(bf16), mvt.astype(bf16)) * (1.0 / jnp.sum(p, axis=-1, keepdims=True))
    o = jnp.zeros((tq, MEM_WIDTH), f32)
    for h in range(MEM_HEADS):
        o = o + jnp.where(lane_head == h, pv[h * tq:(h + 1) * tq], 0.0)
    return o


def _memattn_body(q_ref, mkt_ref, mvt_ref, o_ref, *, nb, tq):
    for b in range(nb):
        o_ref[b] = _mem_attend(q_ref[b], mkt_ref[b], mvt_ref[b])


def _memattn(layer, q3, mkt4, mvt4, nb, tq):
    B, L, _ = q3.shape
    n_mem = mkt4.shape[-1]
    kv_spec = pl.BlockSpec((None, nb, MEM_WIDTH, n_mem), lambda b, i: (layer, b, 0, 0))
    return pl.pallas_call(
        functools.partial(_memattn_body, nb=nb, tq=tq),
        grid=(B // nb, L // tq),
        in_specs=[pl.BlockSpec((nb, tq, MEM_WIDTH), lambda b, i: (b, i, 0)), kv_spec, kv_spec],
        out_specs=pl.BlockSpec((nb, tq, MEM_WIDTH), lambda b, i: (b, i, 0)),
        out_shape=jax.ShapeDtypeStruct((B, L, MEM_WIDTH), f32),
        compiler_params=_cparams("arbitrary", "arbitrary"),
        name="memattn",
    )(q3, mkt4, mvt4)


def _merge_tail(x_ref, mla_ref, gla_ref, mem_o, gates_ref, ggla_ref, wout_ref, y_ref):
    lane = lax.broadcasted_iota(jnp.int32, (1, LANES), 1)
    ggla = ggla_ref[...]
    y = x_ref[...]
    m1 = (gates_ref[:, 0:MLA_WIDTH] * mla_ref[...]).astype(bf16)
    y = y + _mm(m1, wout_ref[0:MLA_WIDTH, :])
    for j in range(GLA_WIDTH // LANES):
        ls = slice(j * LANES, (j + 1) * LANES)
        gn = _half_head_rms(gla_ref[:, ls], ggla, lane)
        gs = slice(MLA_WIDTH + j * LANES, MLA_WIDTH + (j + 1) * LANES)
        y = y + _mm((gates_ref[:, gs] * gn).astype(bf16), wout_ref[gs, :])
    ms = slice(MLA_WIDTH + GLA_WIDTH, D_MIX)
    m3 = (gates_ref[:, ms] * mem_o).astype(bf16)
    y_ref[...] = y + _mm(m3, wout_ref[ms, :])


def _merge_body(x_ref, mla_ref, gla_ref, mem_ref, gates_ref, ggla_ref, wout_ref, y_ref):
    _merge_tail(x_ref, mla_ref, gla_ref, mem_ref[...], gates_ref, ggla_ref, wout_ref, y_ref)


def _merge_memattn_body(x_ref, mla_ref, gla_ref, mq_ref, mkt_ref, mvt_ref, gates_ref, ggla_ref, wout_ref, y_ref):
    mem_o = _mem_attend(mq_ref[...], mkt_ref[...], mvt_ref[...])
    _merge_tail(x_ref, mla_ref, gla_ref, mem_o, gates_ref, ggla_ref, wout_ref, y_ref)


def _merge(x2, mla_o, gla_o, mem_o, gates, w, tm):
    T = x2.shape[0]
    row = lambda i: (i, 0)
    fix = lambda i: (0, 0)
    return pl.pallas_call(
        _merge_body,
        grid=(T // tm,),
        in_specs=[pl.BlockSpec((tm, D_MODEL), row), pl.BlockSpec((tm, MLA_WIDTH), row),
                  pl.BlockSpec((tm, GLA_WIDTH), row), pl.BlockSpec((tm, MEM_WIDTH), row),
                  pl.BlockSpec((tm, D_MIX), row), pl.BlockSpec(w['g_gla_o'].shape, fix),
                  pl.BlockSpec(w['w_out'].shape, fix)],
        out_specs=pl.BlockSpec((tm, D_MODEL), row),
        out_shape=jax.ShapeDtypeStruct((T, D_MODEL), f32),
        compiler_params=_cparams("arbitrary"),
        name="merge",
    )(x2, mla_o, gla_o, mem_o, gates, w['g_gla_o'], w['w_out'])


def _merge_memattn(x2, mla_o, gla_o, mq, mkt, mvt, gates, w, tm, seq):
    T = x2.shape[0]
    n_mem = mkt.shape[-1]
    per_batch = seq // tm
    row = lambda i: (i, 0)
    fix = lambda i: (0, 0)
    kv_spec = pl.BlockSpec((None, MEM_WIDTH, n_mem), lambda i: (i // per_batch, 0, 0))
    return pl.pallas_call(
        _merge_memattn_body,
        grid=(T // tm,),
        in_specs=[pl.BlockSpec((tm, D_MODEL), row), pl.BlockSpec((tm, MLA_WIDTH), row),
                  pl.BlockSpec((tm, GLA_WIDTH), row), pl.BlockSpec((tm, MEM_WIDTH), row), kv_spec, kv_spec,
                  pl.BlockSpec((tm, D_MIX), row), pl.BlockSpec(w['g_gla_o'].shape, fix),
                  pl.BlockSpec(w['w_out'].shape, fix)],
        out_specs=pl.BlockSpec((tm, D_MODEL), row),
        out_shape=jax.ShapeDtypeStruct((T, D_MODEL), f32),
        compiler_params=_cparams("arbitrary"),
        name="merge_memattn",
    )(x2, mla_o, gla_o, mq, mkt, mvt, gates, w['g_gla_o'], w['w_out'])


def _pad_head(a):
    return jnp.pad(a, [(0, 0)] * (a.ndim - 1) + [(0, HEAD_PAD - QK_DIM)])


def _prep_layer(l, p):
    w_in = p['w_in'][l]
    cuts, o = [], 0
    for n in (Q_RANK, KV_RANK, ROPE_DIM, MLA_WIDTH, GLA_KDIM, GLA_KDIM, GLA_WIDTH, GLA_GATE_RANK, GLA_WIDTH,
              MEM_WIDTH, MEM_WIDTH):
        cuts.append(w_in[:, o:o + n])
        o += n
    cq, ckv, kpe, gate_mla, gq, gk, gv, gg, gate_gla, mq, gate_mem = cuts
    z = lambda n: jnp.zeros((D_MODEL, n), f32)
    misc = jnp.concatenate([gg, z(KPE_LANE - GLA_GATE_RANK), kpe, z(LANES - KPE_LANE - ROPE_DIM)], axis=1)
    w_in_packed = jnp.concatenate([cq, ckv, gate_mla, gq, gk, gv, gate_gla, mq, gate_mem, misc], axis=1)

    w_uk = p['w_uk'][l]
    g_k = p['g_mla_k'][l]
    wuk_packed = jnp.pad(w_uk, ((0, 0), (0, 0), (0, HEAD_PAD - NOPE_DIM))).reshape(KV_RANK, QK_PACKED)
    w_uk_t = w_uk.reshape(KV_RANK, MLA_HEADS * NOPE_DIM).T
    w_uk_g = jnp.pad((w_uk * g_k[None, None, :NOPE_DIM]).transpose(1, 2, 0),
                     ((0, 0), (0, HEAD_PAD - NOPE_DIM), (0, 0)))
    w_uv = p['w_uv'][l].transpose(1, 0, 2)
    zeros = jnp.zeros_like(w_uv)
    even = jnp.arange(MLA_HEADS)[:, None, None] % 2 == 0
    w_uv_pairs = jnp.concatenate([jnp.where(even, w_uv, zeros), jnp.where(even, zeros, w_uv)], axis=-1)
    w_gk = jnp.pad(p['w_gk'][l], ((0, LANES - GLA_GATE_RANK), (0, 0)))
    row = lambda a: a.reshape(1, -1)
    return {
        'g_pre': row(p['g_pre'][l]), 'w_in': w_in_packed.astype(bf16), 'g_qa': row(p['g_qa'][l]),
        'w_uq': _pad_head(p['w_uq'][l]).reshape(Q_RANK, QK_PACKED).astype(bf16),
        'g_kva': row(p['g_kva'][l]), 'w_uk': wuk_packed.astype(bf16),
        'g_q': row(_pad_head(p['g_mla_q'][l])), 'g_k': row(_pad_head(g_k)),
        'w_gk': w_gk.astype(bf16), 'b_gk': row(p['b_gk'][l]),
        'g_mem_q': row(jnp.tile(p['g_mem_q'][l], 2)), 'g_mem_k': row(jnp.tile(p['g_mem_k'][l], 2)),
        'g_gla_o': row(jnp.tile(p['g_gla_o'][l], 2)),
        'w_uk_t': w_uk_t.astype(bf16), 'w_uk_g': w_uk_g.astype(bf16),
        'g_k_rope': g_k[NOPE_DIM:].reshape(ROPE_DIM, 1),
        'w_uv_pairs': w_uv_pairs.astype(bf16),
        'w_uv_t': w_uv.transpose(0, 2, 1).astype(bf16),
        'g_mem': row(p['g_mem'][l]), 'w_mem_kv': p['w_mem_kv'][l].astype(bf16),
        'w_out': p['w_out'][l].astype(bf16),
    }


def _angles(pos):
    inv_freq = ROPE_THETA ** (-jnp.arange(HALF_ROPE, dtype=f32) * (2.0 / ROPE_DIM))
    return pos.astype(f32)[:, None] * inv_freq[None, :]


def _packed_tables(pos):
    ang = _angles(pos)
    n = pos.shape[0]
    cos, sin = jnp.cos(ang), jnp.sin(ang)
    one = jnp.ones((n, NOPE_DIM), f32)
    z = lambda w: jnp.zeros((n, w), f32)
    tail = HEAD_PAD - QK_DIM
    cos_p = jnp.concatenate([one, cos, cos, z(tail)], axis=1)
    sin_up = jnp.concatenate([z(NOPE_DIM + HALF_ROPE), sin, z(tail)], axis=1)
    sin_dn = jnp.concatenate([z(NOPE_DIM), -sin, z(HALF_ROPE + tail)], axis=1)
    return cos_p, sin_up, sin_dn


def _transposed_tables(pos):
    ang = _angles(pos).T
    return jnp.cos(ang), jnp.sin(ang)


def _t_to_state(st):
    B = st.shape[0]
    s5 = st.reshape(B, GLA_HEADS, GLA_DV, GLA_HEADS, GLA_DK)
    diag = jnp.stack([s5[:, h, :, h, :] for h in range(GLA_HEADS)], axis=1)
    return diag.transpose(0, 1, 3, 2)


def kernel(x_prompt, x_sample, mem_prompt, cache_ckv, cache_kpe, page_table, state_gla, cache_mem_k, cache_mem_v, g_pre, w_in, g_qa, w_uq, g_kva, w_uk, w_uv, g_mla_q, g_mla_k, w_gk, b_gk, g_gla_o, g_mem, w_mem_kv, g_mem_q, g_mem_k, w_out):
    params = dict(g_pre=g_pre, w_in=w_in, g_qa=g_qa, w_uq=w_uq, g_kva=g_kva, w_uk=w_uk, w_uv=w_uv,
                  g_mla_q=g_mla_q, g_mla_k=g_mla_k, w_gk=w_gk, b_gk=b_gk, g_gla_o=g_gla_o, g_mem=g_mem,
                  w_mem_kv=w_mem_kv, g_mem_q=g_mem_q, g_mem_k=g_mem_k, w_out=w_out)
    nb, seq, _ = x_prompt.shape
    ndb, t_new, _ = x_sample.shape
    n_mem = mem_prompt.shape[1]
    depth = w_in.shape[0]
    n_pages = page_table.shape[1]
    past_len = n_pages * cache_ckv.shape[2]

    tm_p = min(256, seq)
    tm_merge = min(512, seq)
    tq = min(256, seq)
    tm_s = min(256, ndb * t_new)
    pp = max(d for d in range(1, 17) if (n_pages + 1) % d == 0)
    gla_tt = min(128, seq)
    gla_cs = min(16, gla_tt)
    nb_s = min(8, ndb)
    nbp = min(4, nb)
    nb_gla = min(8, nb)

    cache_kpe_t = cache_kpe.transpose(0, 1, 3, 2)
    to_t = lambda a: a.transpose(0, 1, 3, 4, 2).reshape(depth, ndb, MEM_WIDTH, n_mem)
    cache_mkt, cache_mvt = to_t(cache_mem_k), to_t(cache_mem_v)
    from_t = lambda a: a.reshape(nb, MEM_HEADS, MEM_HEAD_DIM, n_mem).transpose(0, 3, 1, 2)

    tabs_p = _packed_tables(jnp.arange(seq, dtype=jnp.int32))
    pos_s = past_len + jnp.arange(t_new, dtype=jnp.int32)
    tabs_s = tuple(jnp.tile(t, (tm_s // t_new, 1)) for t in _packed_tables(pos_s))
    tabs_t = _transposed_tables(jnp.arange(past_len + PAGE_SIZE, dtype=jnp.int32))

    xp = x_prompt.reshape(nb * seq, D_MODEL)
    xs = x_sample.reshape(ndb * t_new, D_MODEL)
    mem2 = mem_prompt.reshape(nb * n_mem, D_MODEL)
    ckv_p, kpe_p, gla_p, mk_p, mv_p, ckv_s, kpe_s, gla_s = ([] for _ in range(8))
    for l in range(depth):
        w = _prep_layer(l, params)
        q, k, c, kpe, gates, gq, gk, gv, gg, mq = _inproj(xp, tabs_p, w, tm_p)
        r3 = lambda a: a.reshape(nb, seq, a.shape[-1])
        mla_o = _mla_prompt(r3(q), r3(k), r3(c), w['w_uv_t'], nb, seq, tq, nbp).reshape(nb * seq, MLA_WIDTH)
        gla_o, st = _gla(r3(gq), r3(gk), r3(gv), r3(gg), nb_gla, gla_tt, gla_cs)
        mkt, mvt = _memkv(mem2, w, nb, n_mem)
        xp = _merge_memattn(xp, mla_o, gla_o.reshape(nb * seq, GLA_WIDTH), mq, mkt, mvt, gates, w, tm_merge, seq)
        ckv_p.append(c.reshape(nb, seq, KV_RANK))
        kpe_p.append(kpe.reshape(nb, seq, ROPE_DIM))
        gla_p.append(_t_to_state(st))
        mk_p.append(from_t(mkt))
        mv_p.append(from_t(mvt))
        q, k, c, kpe, gates, gq, gk, gv, gg, mq = _inproj(xs, tabs_s, w, tm_s)
        mla_o = _mla_sample(l, q, c, kpe, cache_ckv, cache_kpe_t, page_table, w, tabs_t, pp)
        r3 = lambda a: a.reshape(ndb, t_new, a.shape[-1])
        gla_o, st = _gla_decode(r3(gq), r3(gk), r3(gv), r3(gg), state_gla[l].transpose(1, 2, 3, 0))
        mem_o = _memattn(l, r3(mq), cache_mkt, cache_mvt, nb_s, t_new)
        xs = _merge(xs, mla_o, gla_o.reshape(ndb * t_new, GLA_WIDTH), mem_o.reshape(ndb * t_new, MEM_WIDTH), gates, w, tm_s)
        ckv_s.append(c.reshape(ndb, t_new, KV_RANK))
        kpe_s.append(kpe.reshape(ndb, t_new, ROPE_DIM))
        gla_s.append(st.transpose(3, 0, 1, 2))
    return (xp.reshape(nb, seq, D_MODEL), xs.reshape(ndb, t_new, D_MODEL), jnp.stack(ckv_p), jnp.stack(kpe_p),
            jnp.stack(gla_p), jnp.stack(mk_p), jnp.stack(mv_p), jnp.stack(ckv_s), jnp.stack(kpe_s), jnp.stack(gla_s))
```

```python
import functools

import jax
import jax.numpy as jnp
from jax import lax
from jax.experimental import pallas as pl
from jax.experimental.pallas import tpu as pltpu

f32, bf16 = jnp.float32, jnp.bfloat16

D_MODEL = 1024
PAGE_SIZE = 128
MLA_HEADS = 8
NOPE_DIM = 64
ROPE_DIM = 32
HALF_ROPE = ROPE_DIM // 2
QK_DIM = NOPE_DIM + ROPE_DIM
V_DIM = 64
Q_RANK = 384
KV_RANK = 256
MLA_WIDTH = MLA_HEADS * V_DIM
ROPE_THETA = 10000.0
MLA_SCALE = QK_DIM ** -0.5
LOG2_E = 1.4426950408889634
GLA_HEADS = 4
GLA_DK = 32
GLA_DV = 64
GLA_KDIM = GLA_HEADS * GLA_DK
GLA_WIDTH = GLA_HEADS * GLA_DV
GLA_GATE_RANK = 16
GLA_NORMALIZER = 16.0
MEM_HEADS = 4
MEM_HEAD_DIM = 64
MEM_WIDTH = MEM_HEADS * MEM_HEAD_DIM
MEM_SCALE = MEM_HEAD_DIM ** -0.5
D_MIX = MLA_WIDTH + GLA_WIDTH + MEM_WIDTH
EPS = 1e-6
NEG_INF = -1e30

LANES = 128
HEAD_PAD = LANES
QK_PACKED = MLA_HEADS * HEAD_PAD

OFF_CQ = 0
OFF_CKV = OFF_CQ + Q_RANK
OFF_GATE_MLA = OFF_CKV + KV_RANK
OFF_GQ = OFF_GATE_MLA + MLA_WIDTH
OFF_GK = OFF_GQ + GLA_KDIM
OFF_GV = OFF_GK + GLA_KDIM
OFF_GATE_GLA = OFF_GV + GLA_WIDTH
OFF_MQ = OFF_GATE_GLA + GLA_WIDTH
OFF_GATE_MEM = OFF_MQ + MEM_WIDTH
OFF_MISC = OFF_GATE_MEM + MEM_WIDTH
D_IN_PACKED = OFF_MISC + LANES
KPE_LANE = NOPE_DIM

VMEM_LIMIT = 56 * 1024 * 1024
PAGE_BUFFERS = 4


def _cparams(*sem):
    return pltpu.CompilerParams(dimension_semantics=sem, vmem_limit_bytes=VMEM_LIMIT)


def _nt(a, b):
    return lax.dot_general(a, b, (((1,), (1,)), ((), ())), preferred_element_type=f32)


def _tn(a, b):
    return lax.dot_general(a, b, (((0,), (0,)), ((), ())), preferred_element_type=f32)


def _mm(a, b):
    return jnp.dot(a, b, preferred_element_type=f32)


def _rms_rows(x, g):
    return x * lax.rsqrt(jnp.mean(x * x, axis=-1, keepdims=True) + EPS) * g


def _half_head_rms(blk, g, lane):
    sq = blk * blk
    lo = lane < 64
    ss_lo = jnp.sum(jnp.where(lo, sq, 0.0), axis=-1, keepdims=True)
    ss_hi = jnp.sum(jnp.where(lo, 0.0, sq), axis=-1, keepdims=True)
    ss = jnp.where(lo, ss_lo, ss_hi)
    return blk * lax.rsqrt(ss * (1.0 / 64.0) + EPS) * g


def _rope_packed(x, cos, sin_up, sin_dn):
    return x * cos + pltpu.roll(x, HALF_ROPE, 1) * sin_up + pltpu.roll(x, LANES - HALF_ROPE, 1) * sin_dn


def _inproj_body(x_ref, gpre_ref, win_ref, gqa_ref, wuq_ref, gkva_ref, wuk_ref, gq_ref, gk_ref,
                 cos_ref, sup_ref, sdn_ref, wgk_ref, bgk_ref, gmq_ref,
                 q_out, k_out, ckv_out, kpe_out, gates_out, glaq_out, glak_out, glav_out, glag_out, memq_out):
    x = x_ref[...]
    xb = _rms_rows(x, gpre_ref[...]).astype(bf16)

    def seg(off, n):
        return _mm(xb, win_ref[:, off:off + n])

    cos, sup, sdn = cos_ref[...], sup_ref[...], sdn_ref[...]
    lane = lax.broadcasted_iota(jnp.int32, (1, LANES), 1)

    cq = _rms_rows(seg(OFF_CQ, Q_RANK), gqa_ref[...]).astype(bf16)
    qf = _mm(cq, wuq_ref[...])
    gq = gq_ref[...]
    for h in range(MLA_HEADS):
        qh = qf[:, h * HEAD_PAD:(h + 1) * HEAD_PAD]
        ss = jnp.sum(qh * qh, axis=-1, keepdims=True)
        qn = qh * lax.rsqrt(ss * (1.0 / QK_DIM) + EPS) * gq
        q_out[:, h * HEAD_PAD:(h + 1) * HEAD_PAD] = (_rope_packed(qn, cos, sup, sdn) * (MLA_SCALE * LOG2_E)).astype(bf16)

    c = _rms_rows(seg(OFF_CKV, KV_RANK), gkva_ref[...])
    ckv_out[...] = c
    misc = seg(OFF_MISC, LANES)
    kpe_wide = jnp.where((lane >= KPE_LANE) & (lane < KPE_LANE + ROPE_DIM), misc, 0.0)
    kpe_out[...] = misc[:, KPE_LANE:KPE_LANE + ROPE_DIM]
    kf = _mm(c.astype(bf16), wuk_ref[...])
    gk = gk_ref[...]
    for h in range(MLA_HEADS):
        kh = kf[:, h * HEAD_PAD:(h + 1) * HEAD_PAD] + kpe_wide
        ss = jnp.sum(kh * kh, axis=-1, keepdims=True)
        kn = kh * lax.rsqrt(ss * (1.0 / QK_DIM) + EPS) * gk
        k_out[:, h * HEAD_PAD:(h + 1) * HEAD_PAD] = _rope_packed(kn, cos, sup, sdn).astype(bf16)

    g1 = seg(OFF_GATE_MLA, MLA_WIDTH)
    gates_out[:, 0:MLA_WIDTH] = g1 * jax.nn.sigmoid(g1)
    g2 = seg(OFF_GATE_GLA, GLA_WIDTH)
    gates_out[:, MLA_WIDTH:MLA_WIDTH + GLA_WIDTH] = g2 * jax.nn.sigmoid(g2)
    g3 = seg(OFF_GATE_MEM, MEM_WIDTH)
    gates_out[:, MLA_WIDTH + GLA_WIDTH:D_MIX] = g3 * jax.nn.sigmoid(g3)

    glaq_out[...] = seg(OFF_GQ, GLA_KDIM) * (GLA_DK ** -0.5)
    glak_out[...] = seg(OFF_GK, GLA_KDIM)
    glav_out[...] = seg(OFF_GV, GLA_WIDTH)
    gl = _mm(misc.astype(bf16), wgk_ref[...]) + bgk_ref[...]
    glag_out[...] = jax.nn.log_sigmoid(gl) * (LOG2_E / GLA_NORMALIZER)

    mq = seg(OFF_MQ, MEM_WIDTH)
    gmq = gmq_ref[...]
    for j in range(MEM_WIDTH // LANES):
        blk = _half_head_rms(mq[:, j * LANES:(j + 1) * LANES], gmq, lane)
        memq_out[:, j * LANES:(j + 1) * LANES] = (blk * (MEM_SCALE * LOG2_E)).astype(bf16)


def _inproj(x2, tabs, w, tm):
    T = x2.shape[0]
    nt = tabs[0].shape[0] // tm
    row = lambda i: (i, 0)
    fix = lambda i: (0, 0)
    tab = lambda i: (i % nt, 0)

    def full(a):
        return pl.BlockSpec(a.shape, fix)

    consts1 = (w['g_pre'], w['w_in'], w['g_qa'], w['w_uq'], w['g_kva'], w['w_uk'], w['g_q'], w['g_k'])
    consts2 = (w['w_gk'], w['b_gk'], w['g_mem_q'])
    outs = [(QK_PACKED, bf16), (QK_PACKED, bf16), (KV_RANK, f32), (ROPE_DIM, f32), (D_MIX, f32),
            (GLA_KDIM, f32), (GLA_KDIM, f32), (GLA_WIDTH, f32), (GLA_KDIM, f32), (MEM_WIDTH, bf16)]
    return pl.pallas_call(
        _inproj_body,
        grid=(T // tm,),
        in_specs=[pl.BlockSpec((tm, D_MODEL), row)] + [full(a) for a in consts1]
        + [pl.BlockSpec((tm, LANES), tab)] * 3 + [full(a) for a in consts2],
        out_specs=[pl.BlockSpec((tm, n), row) for n, _ in outs],
        out_shape=[jax.ShapeDtypeStruct((T, n), dt) for n, dt in outs],
        compiler_params=_cparams("arbitrary"),
        name="inproj",
    )(x2, *consts1, *tabs, *consts2)


def _mla_prompt_body(q_ref, k_ref, c_ref, wuvt_ref, o_ref, ct_scr, acc_scr, m_scr, l_scr, ot_scr, *, tq, nbp):
    i = pl.program_id(1)
    nblk = ct_scr.shape[1]

    @pl.when(i == 0)
    def _():
        for bb in range(nbp):
            for j in range(nblk):
                ct_scr[bb, j] = c_ref[bb, j * tq:(j + 1) * tq, :].T.astype(bf16)

    m_scr[...] = jnp.full(m_scr.shape, NEG_INF, f32)
    l_scr[...] = jnp.zeros(l_scr.shape, f32)
    acc_scr[...] = jnp.zeros(acc_scr.shape, f32)
    key = lax.broadcasted_iota(jnp.int32, (tq, tq), 0)
    qry = lax.broadcasted_iota(jnp.int32, (tq, tq), 1)
    causal = key <= qry

    def step(kb, masked):
        ks = pl.ds(pl.multiple_of(kb * tq, tq), tq)
        for h in range(MLA_HEADS):
            hs = slice(h * HEAD_PAD, (h + 1) * HEAD_PAD)
            for bb in range(nbp):
                s = _nt(k_ref[bb, ks, hs], q_ref[bb, :, hs])
                if masked:
                    s = jnp.where(causal, s, NEG_INF)
                m_old = m_scr[bb, h:h + 1, :]
                m_new = jnp.maximum(m_old, jnp.max(s, axis=0, keepdims=True))
                alpha = jnp.exp2(m_old - m_new)
                p = jnp.exp2(s - m_new)
                l_scr[bb, h:h + 1, :] = l_scr[bb, h:h + 1, :] * alpha + jnp.sum(p, axis=0, keepdims=True)
                acc_scr[bb, h] = acc_scr[bb, h] * alpha + _mm(ct_scr[bb, kb], p.astype(bf16))
                m_scr[bb, h:h + 1, :] = m_new

    def body(kb, carry):
        step(kb, False)
        return carry

    lax.fori_loop(0, i, body, 0)
    step(i, True)

    for bb in range(nbp):
        for h in range(MLA_HEADS):
            lat_t = (acc_scr[bb, h] * (1.0 / l_scr[bb, h:h + 1, :])).astype(bf16)
            ot_scr[h * V_DIM:(h + 1) * V_DIM, :] = _mm(wuvt_ref[h], lat_t)
        o_ref[bb] = ot_scr[...].T


def _mla_prompt(q, k, c, wuv_t, nb, seq, tq, nbp):
    nq = seq // tq
    return pl.pallas_call(
        functools.partial(_mla_prompt_body, tq=tq, nbp=nbp),
        grid=(nb // nbp, nq),
        in_specs=[pl.BlockSpec((nbp, tq, QK_PACKED), lambda b, i: (b, i, 0)),
                  pl.BlockSpec((nbp, seq, QK_PACKED), lambda b, i: (b, 0, 0), pipeline_mode=pl.Buffered(1)),
                  pl.BlockSpec((nbp, seq, KV_RANK), lambda b, i: (b, 0, 0), pipeline_mode=pl.Buffered(1)),
                  pl.BlockSpec(wuv_t.shape, lambda b, i: (0, 0, 0))],
        out_specs=pl.BlockSpec((nbp, tq, MLA_WIDTH), lambda b, i: (b, i, 0)),
        out_shape=jax.ShapeDtypeStruct((nb, seq, MLA_WIDTH), f32),
        scratch_shapes=[pltpu.VMEM((nbp, nq, KV_RANK, tq), bf16), pltpu.VMEM((nbp, MLA_HEADS, KV_RANK, tq), f32),
                        pltpu.VMEM((nbp, MLA_HEADS, tq), f32), pltpu.VMEM((nbp, MLA_HEADS, tq), f32),
                        pltpu.VMEM((MLA_WIDTH, tq), f32)],
        compiler_params=_cparams("arbitrary", "arbitrary"),
        name="mla_prompt",
    )(q, k, c, wuv_t)


def _mla_sample_body(pt_ref, q_ref, cnew_ref, kpenewt_ref, wukt_ref, wukg_ref, gkr_ref, cost_ref, sint_ref, wuv_ref,
                     ckv_hbm, kpe_hbm, o_ref,
                     cbuf, kbuf, sem, lhs_scr, qr_scr, cb_scr, s_scr, ql_all, qr_all, acc_scr, m_scr, l_scr,
                     *, layer, gp, n_pages, t_new):
    ndb = ql_all.shape[0]
    ng = (n_pages + 1) // gp
    kb = gp * PAGE_SIZE
    total = ndb * ng
    nrow = MLA_HEADS * t_new
    n_nope = MLA_HEADS * NOPE_DIM
    new_rows = pl.ds(kb - PAGE_SIZE, t_new)

    def group_copies(t):
        b, g, slot = t // ng, t % ng, t % PAGE_BUFFERS
        cps = []
        for i in range(gp):
            page = pt_ref[b * n_pages + jnp.minimum(g * gp + i, n_pages - 1)]
            rows = pl.ds(i * PAGE_SIZE, PAGE_SIZE)
            cps.append(pltpu.make_async_copy(ckv_hbm.at[layer, page], cbuf.at[slot, rows, :], sem.at[0, slot]))
            cps.append(pltpu.make_async_copy(kpe_hbm.at[layer, page], kbuf.at[slot, i], sem.at[1, slot]))
        return cps

    def reset_state():
        m_scr[...] = jnp.full((nrow, 1), NEG_INF, f32)
        l_scr[...] = jnp.zeros((nrow, 1), f32)
        acc_scr[...] = jnp.zeros((nrow, KV_RANK), f32)

    gkr = gkr_ref[...]
    key_lane = lax.broadcasted_iota(jnp.int32, (nrow, PAGE_SIZE), 1)
    qry_row = lax.broadcasted_iota(jnp.int32, (nrow, PAGE_SIZE), 0) % t_new

    def prepare_group(t):
        b, g, pslot = t // ng, t % ng, t % PAGE_BUFFERS
        lhs_scr[n_nope:, :] = ql_all[b].astype(bf16)
        qr_scr[...] = qr_all[b].astype(bf16)
        for cp in group_copies(t):
            cp.wait()
        is_last = g == ng - 1
        cbuf[pslot, new_rows, :] = jnp.where(is_last, cnew_ref[b], cbuf[pslot, new_rows, :])
        kbuf[pslot, gp - 1] = jnp.where(is_last, kpenewt_ref[b], kbuf[pslot, gp - 1])

    def score_group(t, slot):
        g, pslot = t % ng, t % PAGE_BUFFERS
        is_last = g == ng - 1
        cb = cbuf[pslot].astype(bf16)
        cb_scr[slot] = cb
        kt = jnp.concatenate([kbuf[pslot, i] for i in range(gp)], axis=1)
        cos_t, sin_t = cost_ref[g], sint_ref[g]
        big = _nt(lhs_scr[...], cb)
        kn = big[0:n_nope]
        ss_nope = jnp.sum((kn * kn).reshape(MLA_HEADS, NOPE_DIM, kb), axis=1)
        ss = ss_nope + jnp.sum(kt * kt, axis=0, keepdims=True)
        r = lax.rsqrt(ss * (1.0 / QK_DIM) + EPS)
        kg = kt * gkr
        k1, k2 = kg[0:HALF_ROPE], kg[HALF_ROPE:ROPE_DIM]
        kr = jnp.concatenate([k1 * cos_t - k2 * sin_t, k2 * cos_t + k1 * sin_t], axis=0).astype(bf16)
        s = big[n_nope:] + _mm(qr_scr[...], kr)
        s = (s.reshape(MLA_HEADS, t_new, kb) * r[:, None, :]).reshape(nrow, kb)
        first_dead = jnp.where(is_last, 1, PAGE_SIZE + t_new)
        tail = jnp.where(key_lane >= qry_row + first_dead, NEG_INF, s[:, kb - PAGE_SIZE:])
        s_scr[slot] = jnp.concatenate([s[:, :kb - PAGE_SIZE], tail], axis=1)

    def value_group(slot):
        s = s_scr[slot]
        m_old = m_scr[...]
        m_new = jnp.maximum(m_old, jnp.max(s, axis=-1, keepdims=True))
        alpha = jnp.exp2(m_old - m_new)
        p = jnp.exp2(s - m_new)
        l_scr[...] = l_scr[...] * alpha + jnp.sum(p, axis=-1, keepdims=True)
        acc_scr[...] = acc_scr[...] * alpha + _mm(p.astype(bf16), cb_scr[slot])
        m_scr[...] = m_new

    def finish_batch(b, closed):
        ql_all[b] = jnp.where(closed, acc_scr[...] * (1.0 / l_scr[...]), ql_all[b])
        m_scr[...] = jnp.where(closed, NEG_INF, m_scr[...])
        l_scr[...] = jnp.where(closed, 0.0, l_scr[...])
        acc_scr[...] = jnp.where(closed, 0.0, acc_scr[...])

    lhs_scr[0:n_nope, :] = wukt_ref[...]
    for h in range(MLA_HEADS):
        q_h = q_ref[:, h * HEAD_PAD:(h + 1) * HEAD_PAD]
        rows = slice(h * t_new, (h + 1) * t_new)
        ql_all[:, rows, :] = _mm(q_h, wukg_ref[h]).reshape(ndb, t_new, KV_RANK)
        qr_all[:, rows, :] = q_h[:, KPE_LANE:KPE_LANE + ROPE_DIM].astype(f32).reshape(ndb, t_new, ROPE_DIM)
    reset_state()

    ahead = PAGE_BUFFERS - 1

    def skewed_step(t, slot, prefetch):
        if prefetch:
            for cp in group_copies(t + ahead):
                cp.start()
        prepare_group(t)
        score_group(t, slot)
        value_group(1 - slot)
        closes = t % ng == 0
        finish_batch(jnp.maximum(t // ng - 1, 0), closes)

    for t in range(min(ahead, total)):
        for cp in group_copies(t):
            cp.start()
    if total > ahead:
        for cp in group_copies(ahead):
            cp.start()
    prepare_group(0)
    score_group(0, 0)

    n_pairs = max(total - 1 - ahead, 0) // 2

    def body(k, carry):
        skewed_step(2 * k + 1, 1, True)
        skewed_step(2 * k + 2, 0, True)
        return carry

    lax.fori_loop(0, n_pairs, body, 0)
    for t in range(2 * n_pairs + 1, total):
        skewed_step(t, t % 2, t + ahead < total)
    value_group((total - 1) % 2)
    finish_batch(ndb - 1, True)

    for h in range(MLA_HEADS):
        lat_h = ql_all[:, h * t_new:(h + 1) * t_new, :].reshape(ndb * t_new, KV_RANK).astype(bf16)
        contrib = _mm(lat_h, wuv_ref[h])
        ps = slice((h // 2) * LANES, (h // 2 + 1) * LANES)
        if h % 2 == 0:
            o_ref[:, ps] = contrib
        else:
            o_ref[:, ps] += contrib


def _mla_sample(layer, q, c_new, kpe_new, cache_ckv, cache_kpe_t, page_table, w, tabs_t, gp):
    ndb, n_pages = page_table.shape
    t_new = q.shape[0] // ndb
    ng = (n_pages + 1) // gp
    kb = gp * PAGE_SIZE
    nrow = MLA_HEADS * t_new
    cost, sint = tabs_t
    by_group = lambda a: a.reshape(HALF_ROPE, ng, kb).transpose(1, 0, 2)
    kpe_new_t = jnp.pad(kpe_new.reshape(ndb, t_new, ROPE_DIM).transpose(0, 2, 1),
                        ((0, 0), (0, 0), (0, PAGE_SIZE - t_new)))
    args = (q, c_new.reshape(ndb, t_new, KV_RANK), kpe_new_t, w['w_uk_t'], w['w_uk_g'], w['g_k_rope'],
            by_group(cost), by_group(sint), w['w_uv_pairs'])

    def full(a):
        return pl.BlockSpec(a.shape, lambda i, pt, nd=a.ndim: (0,) * nd)

    return pl.pallas_call(
        functools.partial(_mla_sample_body, layer=layer, gp=gp, n_pages=n_pages, t_new=t_new),
        grid_spec=pltpu.PrefetchScalarGridSpec(
            num_scalar_prefetch=1,
            grid=(1,),
            in_specs=[full(a) for a in args] + [pl.BlockSpec(memory_space=pl.ANY)] * 2,
            out_specs=pl.BlockSpec((ndb * t_new, MLA_WIDTH), lambda i, pt: (0, 0)),
            scratch_shapes=[pltpu.VMEM((PAGE_BUFFERS, kb, KV_RANK), f32),
                            pltpu.VMEM((PAGE_BUFFERS, gp, ROPE_DIM, PAGE_SIZE), f32),
                            pltpu.SemaphoreType.DMA((2, PAGE_BUFFERS)),
                            pltpu.VMEM((MLA_HEADS * NOPE_DIM + nrow, KV_RANK), bf16),
                            pltpu.VMEM((nrow, ROPE_DIM), bf16),
                            pltpu.VMEM((2, kb, KV_RANK), bf16),
                            pltpu.VMEM((2, nrow, kb), f32),
                            pltpu.VMEM((ndb, nrow, KV_RANK), f32),
                            pltpu.VMEM((ndb, nrow, ROPE_DIM), f32),
                            pltpu.VMEM((nrow, KV_RANK), f32),
                            pltpu.VMEM((nrow, 1), f32),
                            pltpu.VMEM((nrow, 1), f32)]),
        out_shape=jax.ShapeDtypeStruct((ndb * t_new, MLA_WIDTH), f32),
        compiler_params=_cparams("arbitrary"),
        name="mla_sample",
    )(page_table.reshape(-1), *args, cache_ckv, cache_kpe_t)


def _split3(x):
    hi = x.astype(bf16)
    r1 = x - hi.astype(f32)
    mid = r1.astype(bf16)
    lo = (r1 - mid.astype(f32)).astype(bf16)
    return hi, mid, lo


def _gla_body(q_ref, k_ref, v_ref, g_ref, o_ref, sT_out, st_scr, *, nb, tt, cs):
    ti = pl.program_id(1)
    nchunk = tt // cs

    @pl.when(ti == 0)
    def _():
        st_scr[...] = jnp.zeros(st_scr.shape, f32)

    tri = (lax.broadcasted_iota(jnp.int32, (tt, tt), 1) <= lax.broadcasted_iota(jnp.int32, (tt, tt), 0)).astype(bf16)
    kv_head = (lax.broadcasted_iota(jnp.int32, (GLA_KDIM, GLA_WIDTH), 0) // GLA_DK
               == lax.broadcasted_iota(jnp.int32, (GLA_KDIM, GLA_WIDTH), 1) // GLA_DV)
    block_ones = kv_head.astype(bf16)
    vk_head = (lax.broadcasted_iota(jnp.int32, (GLA_WIDTH, GLA_KDIM), 0) // GLA_DV
               == lax.broadcasted_iota(jnp.int32, (GLA_WIDTH, GLA_KDIM), 1) // GLA_DK)
    sel_t = (lax.broadcasted_iota(jnp.int32, (cs, cs * cs), 1) // cs
             == lax.broadcasted_iota(jnp.int32, (cs, cs * cs), 0)).astype(bf16)
    t_idx = lax.broadcasted_iota(jnp.int32, (cs, cs, GLA_KDIM), 0)
    s_idx = lax.broadcasted_iota(jnp.int32, (cs, cs, GLA_KDIM), 1)
    causal3 = s_idx <= t_idx

    cums = []
    for b in range(nb):
        hi, mid, lo = _split3(g_ref[b])
        cums.append(_mm(tri, hi) + _mm(tri, mid) + _mm(tri, lo))
    for n in range(nchunk):
        sl = slice(n * cs, (n + 1) * cs)
        for b in range(nb):
            cum = cums[b]
            base = cum[n * cs - 1:n * cs] if n > 0 else jnp.zeros((1, GLA_KDIM), f32)
            bc = cum[sl] - base
            b_last = bc[cs - 1:cs]
            qc, kc, vc = q_ref[b, sl, :], k_ref[b, sl, :], v_ref[b, sl, :]
            st = st_scr[b]
            o_inter = _nt((qc * jnp.exp2(bc)).astype(bf16), st.astype(bf16))
            diff = jnp.where(causal3, bc[:, None, :] - bc[None, :, :], NEG_INF)
            d3 = qc[:, None, :] * kc[None, :, :] * jnp.exp2(diff)
            a_exp = _mm(d3.reshape(cs * cs, GLA_KDIM).astype(bf16), block_ones)
            xv = (a_exp.reshape(cs, cs, GLA_WIDTH) * vc[None, :, :]).reshape(cs * cs, GLA_WIDTH)
            o_intra = _mm(sel_t, xv.astype(bf16))
            o_ref[b, sl, :] = o_inter + o_intra
            kd = (kc * jnp.exp2(b_last - bc)).astype(bf16)
            upd = _tn(vc.astype(bf16), kd)
            st_scr[b] = st * jnp.exp2(b_last) + jnp.where(vk_head, upd, 0.0)

    @pl.when(ti == pl.num_programs(1) - 1)
    def _():
        sT_out[...] = st_scr[...]


def _gla(q, k, v, g, nb, tt, cs):
    B, L, _ = q.shape
    tok = lambda bi, ti: (bi, ti, 0)
    st = lambda bi, ti: (bi, 0, 0)
    return pl.pallas_call(
        functools.partial(_gla_body, nb=nb, tt=tt, cs=cs),
        grid=(B // nb, L // tt),
        in_specs=[pl.BlockSpec((nb, tt, GLA_KDIM), tok), pl.BlockSpec((nb, tt, GLA_KDIM), tok),
                  pl.BlockSpec((nb, tt, GLA_WIDTH), tok), pl.BlockSpec((nb, tt, GLA_KDIM), tok)],
        out_specs=[pl.BlockSpec((nb, tt, GLA_WIDTH), tok), pl.BlockSpec((nb, GLA_WIDTH, GLA_KDIM), st)],
        out_shape=[jax.ShapeDtypeStruct((B, L, GLA_WIDTH), f32), jax.ShapeDtypeStruct((B, GLA_WIDTH, GLA_KDIM), f32)],
        scratch_shapes=[pltpu.VMEM((nb, GLA_WIDTH, GLA_KDIM), f32)],
        compiler_params=_cparams("arbitrary", "arbitrary"),
        name="gla",
    )(q, k, v, g)


def _gla_decode_body(q_ref, k_ref, v_ref, g_ref, s0_ref, o_ref, s_out, qt_scr, kt_scr, et_scr, vt_scr, ot_scr):
    nb, t_new, _ = q_ref.shape
    for t in range(t_new):
        qt_scr[t] = q_ref[:, t, :].T
        kt_scr[t] = k_ref[:, t, :].T
        et_scr[t] = jnp.exp2(g_ref[:, t, :].T)
        vt_scr[t] = v_ref[:, t, :].T
    ot_scr[...] = jnp.zeros(ot_scr.shape, f32)

    for h in range(GLA_HEADS):
        vs = slice(h * GLA_DV, (h + 1) * GLA_DV)

        def one_k(kk, carry):
            row = pl.ds(h * GLA_DK + kk, 1)
            s = s0_ref[h, kk]
            for t in range(t_new):
                s = s * et_scr[t, row, :] + kt_scr[t, row, :] * vt_scr[t, vs, :]
                ot_scr[t, vs, :] += qt_scr[t, row, :] * s
            s_out[h, kk] = s
            return carry

        lax.fori_loop(0, GLA_DK, one_k, 0)

    for t in range(t_new):
        o_ref[:, t, :] = ot_scr[t].T


def _gla_decode(q, k, v, g, s0):
    B, T, _ = q.shape
    full = lambda a: pl.BlockSpec(a.shape, lambda i, nd=a.ndim: (0,) * nd)
    o_shape = jax.ShapeDtypeStruct((B, T, GLA_WIDTH), f32)
    s_shape = jax.ShapeDtypeStruct(s0.shape, f32)
    return pl.pallas_call(
        _gla_decode_body,
        grid=(1,),
        in_specs=[full(a) for a in (q, k, v, g, s0)],
        out_specs=[full(o_shape), full(s_shape)],
        out_shape=[o_shape, s_shape],
        scratch_shapes=[pltpu.VMEM((T, GLA_KDIM, B), f32)] * 3 + [pltpu.VMEM((T, GLA_WIDTH, B), f32)] * 2,
        compiler_params=_cparams("arbitrary"),
        name="gla_decode",
    )(q, k, v, g, s0)


def _memkv_body(mem_ref, gmem_ref, w_ref, gk_ref, mkt_out, mvt_out):
    xb = _rms_rows(mem_ref[...], gmem_ref[...]).astype(bf16)
    kv = _mm(xb, w_ref[...])
    lane = lax.broadcasted_iota(jnp.int32, (1, LANES), 1)
    gk = gk_ref[...]
    for j in range(MEM_WIDTH // LANES):
        blk = _half_head_rms(kv[:, j * LANES:(j + 1) * LANES], gk, lane)
        mkt_out[j * LANES:(j + 1) * LANES, :] = blk.T
        mvt_out[j * LANES:(j + 1) * LANES, :] = kv[:, MEM_WIDTH + j * LANES:MEM_WIDTH + (j + 1) * LANES].T


def _memkv(mem2, w, nb, n_mem):
    row = lambda i: (i, 0)
    fix = lambda i: (0, 0)
    return pl.pallas_call(
        _memkv_body,
        grid=(nb,),
        in_specs=[pl.BlockSpec((n_mem, D_MODEL), row), pl.BlockSpec(w['g_mem'].shape, fix),
                  pl.BlockSpec(w['w_mem_kv'].shape, fix), pl.BlockSpec(w['g_mem_k'].shape, fix)],
        out_specs=[pl.BlockSpec((None, MEM_WIDTH, n_mem), lambda i: (i, 0, 0))] * 2,
        out_shape=[jax.ShapeDtypeStruct((nb, MEM_WIDTH, n_mem), f32)] * 2,
        compiler_params=_cparams("arbitrary"),
        name="memkv",
    )(mem2, w['g_mem'], w['w_mem_kv'], w['g_mem_k'])


def _mem_attend(q, mkt, mvt):
    tq = q.shape[0]
    lane_head = lax.broadcasted_iota(jnp.int32, (1, MEM_WIDTH), 1) // MEM_HEAD_DIM
    qs = jnp.concatenate([jnp.where(lane_head == h, q, jnp.zeros_like(q)) for h in range(MEM_HEADS)], axis=0)
    s = _mm(qs, mkt.astype(bf16))
    p = jnp.exp2(s - jnp.max(s, axis=-1, keepdims=True))
    pv = _nt(p.astype(bf16), mvt.astype(bf16)) * (1.0 / jnp.sum(p, axis=-1, keepdims=True))
    o = jnp.zeros((tq, MEM_WIDTH), f32)
    for h in range(MEM_HEADS):
        o = o + jnp.where(lane_head == h, pv[h * tq:(h + 1) * tq], 0.0)
    return o


def _memattn_body(q_ref, mkt_ref, mvt_ref, o_ref, *, nb, tq):
    for b in range(nb):
        o_ref[b] = _mem_attend(q_ref[b], mkt_ref[b], mvt_ref[b])


def _memattn(layer, q3, mkt4, mvt4, nb, tq):
    B, L, _ = q3.shape
    n_mem = mkt4.shape[-1]
    kv_spec = pl.BlockSpec((None, nb, MEM_WIDTH, n_mem), lambda b, i: (layer, b, 0, 0))
    return pl.pallas_call(
        functools.partial(_memattn_body, nb=nb, tq=tq),
        grid=(B // nb, L // tq),
        in_specs=[pl.BlockSpec((nb, tq, MEM_WIDTH), lambda b, i: (b, i, 0)), kv_spec, kv_spec],
        out_specs=pl.BlockSpec((nb, tq, MEM_WIDTH), lambda b, i: (b, i, 0)),
        out_shape=jax.ShapeDtypeStruct((B, L, MEM_WIDTH), f32),
        compiler_params=_cparams("arbitrary", "arbitrary"),
        name="memattn",
    )(q3, mkt4, mvt4)


def _merge_tail(x_ref, mla_ref, gla_ref, mem_o, gates_ref, ggla_ref, wout_ref, y_ref):
    lane = lax.broadcasted_iota(jnp.int32, (1, LANES), 1)
    ggla = ggla_ref[...]
    y = x_ref[...]
    m1 = (gates_ref[:, 0:MLA_WIDTH] * mla_ref[...]).astype(bf16)
    y = y + _mm(m1, wout_ref[0:MLA_WIDTH, :])
    for j in range(GLA_WIDTH // LANES):
        ls = slice(j * LANES, (j + 1) * LANES)
        gn = _half_head_rms(gla_ref[:, ls], ggla, lane)
        gs = slice(MLA_WIDTH + j * LANES, MLA_WIDTH + (j + 1) * LANES)
        y = y + _mm((gates_ref[:, gs] * gn).astype(bf16), wout_ref[gs, :])
    ms = slice(MLA_WIDTH + GLA_WIDTH, D_MIX)
    m3 = (gates_ref[:, ms] * mem_o).astype(bf16)
    y_ref[...] = y + _mm(m3, wout_ref[ms, :])


def _merge_body(x_ref, mla_ref, gla_ref, mem_ref, gates_ref, ggla_ref, wout_ref, y_ref):
    _merge_tail(x_ref, mla_ref, gla_ref, mem_ref[...], gates_ref, ggla_ref, wout_ref, y_ref)


def _merge_memattn_body(x_ref, mla_ref, gla_ref, mq_ref, mkt_ref, mvt_ref, gates_ref, ggla_ref, wout_ref, y_ref):
    mem_o = _mem_attend(mq_ref[...], mkt_ref[...], mvt_ref[...])
    _merge_tail(x_ref, mla_ref, gla_ref, mem_o, gates_ref, ggla_ref, wout_ref, y_ref)


def _merge(x2, mla_o, gla_o, mem_o, gates, w, tm):
    T = x2.shape[0]
    row = lambda i: (i, 0)
    fix = lambda i: (0, 0)
    return pl.pallas_call(
        _merge_body,
        grid=(T // tm,),
        in_specs=[pl.BlockSpec((tm, D_MODEL), row), pl.BlockSpec((tm, MLA_WIDTH), row),
                  pl.BlockSpec((tm, GLA_WIDTH), row), pl.BlockSpec((tm, MEM_WIDTH), row),
                  pl.BlockSpec((tm, D_MIX), row), pl.BlockSpec(w['g_gla_o'].shape, fix),
                  pl.BlockSpec(w['w_out'].shape, fix)],
        out_specs=pl.BlockSpec((tm, D_MODEL), row),
        out_shape=jax.ShapeDtypeStruct((T, D_MODEL), f32),
        compiler_params=_cparams("arbitrary"),
        name="merge",
    )(x2, mla_o, gla_o, mem_o, gates, w['g_gla_o'], w['w_out'])


def _merge_memattn(x2, mla_o, gla_o, mq, mkt, mvt, gates, w, tm, seq):
    T = x2.shape[0]
    n_mem = mkt.shape[-1]
    per_batch = seq // tm
    row = lambda i: (i, 0)
    fix = lambda i: (0, 0)
    kv_spec = pl.BlockSpec((None, MEM_WIDTH, n_mem), lambda i: (i // per_batch, 0, 0))
    return pl.pallas_call(
        _merge_memattn_body,
        grid=(T // tm,),
        in_specs=[pl.BlockSpec((tm, D_MODEL), row), pl.BlockSpec((tm, MLA_WIDTH), row),
                  pl.BlockSpec((tm, GLA_WIDTH), row), pl.BlockSpec((tm, MEM_WIDTH), row), kv_spec, kv_spec,
                  pl.BlockSpec((tm, D_MIX), row), pl.BlockSpec(w['g_gla_o'].shape, fix),
                  pl.BlockSpec(w['w_out'].shape, fix)],
        out_specs=pl.BlockSpec((tm, D_MODEL), row),
        out_shape=jax.ShapeDtypeStruct((T, D_MODEL), f32),
        compiler_params=_cparams("arbitrary"),
        name="merge_memattn",
    )(x2, mla_o, gla_o, mq, mkt, mvt, gates, w['g_gla_o'], w['w_out'])


def _pad_head(a):
    return jnp.pad(a, [(0, 0)] * (a.ndim - 1) + [(0, HEAD_PAD - QK_DIM)])


def _prep_layer(l, p):
    w_in = p['w_in'][l]
    cuts, o = [], 0
    for n in (Q_RANK, KV_RANK, ROPE_DIM, MLA_WIDTH, GLA_KDIM, GLA_KDIM, GLA_WIDTH, GLA_GATE_RANK, GLA_WIDTH,
              MEM_WIDTH, MEM_WIDTH):
        cuts.append(w_in[:, o:o + n])
        o += n
    cq, ckv, kpe, gate_mla, gq, gk, gv, gg, gate_gla, mq, gate_mem = cuts
    z = lambda n: jnp.zeros((D_MODEL, n), f32)
    misc = jnp.concatenate([gg, z(KPE_LANE - GLA_GATE_RANK), kpe, z(LANES - KPE_LANE - ROPE_DIM)], axis=1)
    w_in_packed = jnp.concatenate([cq, ckv, gate_mla, gq, gk, gv, gate_gla, mq, gate_mem, misc], axis=1)

    w_uk = p['w_uk'][l]
    g_k = p['g_mla_k'][l]
    wuk_packed = jnp.pad(w_uk, ((0, 0), (0, 0), (0, HEAD_PAD - NOPE_DIM))).reshape(KV_RANK, QK_PACKED)
    w_uk_t = w_uk.reshape(KV_RANK, MLA_HEADS * NOPE_DIM).T
    w_uk_g = jnp.pad((w_uk * g_k[None, None, :NOPE_DIM]).transpose(1, 2, 0),
                     ((0, 0), (0, HEAD_PAD - NOPE_DIM), (0, 0)))
    w_uv = p['w_uv'][l].transpose(1, 0, 2)
    zeros = jnp.zeros_like(w_uv)
    even = jnp.arange(MLA_HEADS)[:, None, None] % 2 == 0
    w_uv_pairs = jnp.concatenate([jnp.where(even, w_uv, zeros), jnp.where(even, zeros, w_uv)], axis=-1)
    w_gk = jnp.pad(p['w_gk'][l], ((0, LANES - GLA_GATE_RANK), (0, 0)))
    row = lambda a: a.reshape(1, -1)
    return {
        'g_pre': row(p['g_pre'][l]), 'w_in': w_in_packed.astype(bf16), 'g_qa': row(p['g_qa'][l]),
        'w_uq': _pad_head(p['w_uq'][l]).reshape(Q_RANK, QK_PACKED).astype(bf16),
        'g_kva': row(p['g_kva'][l]), 'w_uk': wuk_packed.astype(bf16),
        'g_q': row(_pad_head(p['g_mla_q'][l])), 'g_k': row(_pad_head(g_k)),
        'w_gk': w_gk.astype(bf16), 'b_gk': row(p['b_gk'][l]),
        'g_mem_q': row(jnp.tile(p['g_mem_q'][l], 2)), 'g_mem_k': row(jnp.tile(p['g_mem_k'][l], 2)),
        'g_gla_o': row(jnp.tile(p['g_gla_o'][l], 2)),
        'w_uk_t': w_uk_t.astype(bf16), 'w_uk_g': w_uk_g.astype(bf16),
        'g_k_rope': g_k[NOPE_DIM:].reshape(ROPE_DIM, 1),
        'w_uv_pairs': w_uv_pairs.astype(bf16),
        'w_uv_t': w_uv.transpose(0, 2, 1).astype(bf16),
        'g_mem': row(p['g_mem'][l]), 'w_mem_kv': p['w_mem_kv'][l].astype(bf16),
        'w_out': p['w_out'][l].astype(bf16),
    }


def _angles(pos):
    inv_freq = ROPE_THETA ** (-jnp.arange(HALF_ROPE, dtype=f32) * (2.0 / ROPE_DIM))
    return pos.astype(f32)[:, None] * inv_freq[None, :]


def _packed_tables(pos):
    ang = _angles(pos)
    n = pos.shape[0]
    cos, sin = jnp.cos(ang), jnp.sin(ang)
    one = jnp.ones((n, NOPE_DIM), f32)
    z = lambda w: jnp.zeros((n, w), f32)
    tail = HEAD_PAD - QK_DIM
    cos_p = jnp.concatenate([one, cos, cos, z(tail)], axis=1)
    sin_up = jnp.concatenate([z(NOPE_DIM + HALF_ROPE), sin, z(tail)], axis=1)
    sin_dn = jnp.concatenate([z(NOPE_DIM), -sin, z(HALF_ROPE + tail)], axis=1)
    return cos_p, sin_up, sin_dn


def _transposed_tables(pos):
    ang = _angles(pos).T
    return jnp.cos(ang), jnp.sin(ang)


def _t_to_state(st):
    B = st.shape[0]
    s5 = st.reshape(B, GLA_HEADS, GLA_DV, GLA_HEADS, GLA_DK)
    diag = jnp.stack([s5[:, h, :, h, :] for h in range(GLA_HEADS)], axis=1)
    return diag.transpose(0, 1, 3, 2)


def kernel(x_prompt, x_sample, mem_prompt, cache_ckv, cache_kpe, page_table, state_gla, cache_mem_k, cache_mem_v, g_pre, w_in, g_qa, w_uq, g_kva, w_uk, w_uv, g_mla_q, g_mla_k, w_gk, b_gk, g_gla_o, g_mem, w_mem_kv, g_mem_q, g_mem_k, w_out):
    params = dict(g_pre=g_pre, w_in=w_in, g_qa=g_qa, w_uq=w_uq, g_kva=g_kva, w_uk=w_uk, w_uv=w_uv,
                  g_mla_q=g_mla_q, g_mla_k=g_mla_k, w_gk=w_gk, b_gk=b_gk, g_gla_o=g_gla_o, g_mem=g_mem,
                  w_mem_kv=w_mem_kv, g_mem_q=g_mem_q, g_mem_k=g_mem_k, w_out=w_out)
    nb, seq, _ = x_prompt.shape
    ndb, t_new, _ = x_sample.shape
    n_mem = mem_prompt.shape[1]
    depth = w_in.shape[0]
    n_pages = page_table.shape[1]
    past_len = n_pages * cache_ckv.shape[2]

    tm_p = min(256, seq)
    tm_merge = min(512, seq)
    tq = min(256, seq)
    tm_s = min(256, ndb * t_new)
    pp = max(d for d in range(1, 17) if (n_pages + 1) % d == 0)
    gla_tt = min(128, seq)
    gla_cs = min(16, gla_tt)
    nb_s = min(8, ndb)
    nbp = min(4, nb)
    nb_gla = min(8, nb)

    cache_kpe_t = cache_kpe.transpose(0, 1, 3, 2)
    to_t = lambda a: a.transpose(0, 1, 3, 4, 2).reshape(depth, ndb, MEM_WIDTH, n_mem)
    cache_mkt, cache_mvt = to_t(cache_mem_k), to_t(cache_mem_v)
    from_t = lambda a: a.reshape(nb, MEM_HEADS, MEM_HEAD_DIM, n_mem).transpose(0, 3, 1, 2)

    tabs_p = _packed_tables(jnp.arange(seq, dtype=jnp.int32))
    pos_s = past_len + jnp.arange(t_new, dtype=jnp.int32)
    tabs_s = tuple(jnp.tile(t, (tm_s // t_new, 1)) for t in _packed_tables(pos_s))
    tabs_t = _transposed_tables(jnp.arange(past_len + PAGE_SIZE, dtype=jnp.int32))

    xp = x_prompt.reshape(nb * seq, D_MODEL)
    xs = x_sample.reshape(ndb * t_new, D_MODEL)
    mem2 = mem_prompt.reshape(nb * n_mem, D_MODEL)
    ckv_p, kpe_p, gla_p, mk_p, mv_p, ckv_s, kpe_s, gla_s = ([] for _ in range(8))
    for l in range(depth):
        w = _prep_layer(l, params)
        q, k, c, kpe, gates, gq, gk, gv, gg, mq = _inproj(xp, tabs_p, w, tm_p)
        r3 = lambda a: a.reshape(nb, seq, a.shape[-1])
        mla_o = _mla_prompt(r3(q), r3(k), r3(c), w['w_uv_t'], nb, seq, tq, nbp).reshape(nb * seq, MLA_WIDTH)
        gla_o, st = _gla(r3(gq), r3(gk), r3(gv), r3(gg), nb_gla, gla_tt, gla_cs)
        mkt, mvt = _memkv(mem2, w, nb, n_mem)
        xp = _merge_memattn(xp, mla_o, gla_o.reshape(nb * seq, GLA_WIDTH), mq, mkt, mvt, gates, w, tm_merge, seq)
        ckv_p.append(c.reshape(nb, seq, KV_RANK))
        kpe_p.append(kpe.reshape(nb, seq, ROPE_DIM))
        gla_p.append(_t_to_state(st))
        mk_p.append(from_t(mkt))
        mv_p.append(from_t(mvt))
        q, k, c, kpe, gates, gq, gk, gv, gg, mq = _inproj(xs, tabs_s, w, tm_s)
        mla_o = _mla_sample(l, q, c, kpe, cache_ckv, cache_kpe_t, page_table, w, tabs_t, pp)
        r3 = lambda a: a.reshape(ndb, t_new, a.shape[-1])
        gla_o, st = _gla_decode(r3(gq), r3(gk), r3(gv), r3(gg), state_gla[l].transpose(1, 2, 3, 0))
        mem_o = _memattn(l, r3(mq), cache_mkt, cache_mvt, nb_s, t_new)
        xs = _merge(xs, mla_o, gla_o.reshape(ndb * t_new, GLA_WIDTH), mem_o.reshape(ndb * t_new, MEM_WIDTH), gates, w, tm_s)
        ckv_s.append(c.reshape(ndb, t_new, KV_RANK))
        kpe_s.append(kpe.reshape(ndb, t_new, ROPE_DIM))
        gla_s.append(st.transpose(3, 0, 1, 2))
    return (xp.reshape(nb, seq, D_MODEL), xs.reshape(ndb, t_new, D_MODEL), jnp.stack(ckv_p), jnp.stack(kpe_p),
            jnp.stack(gla_p), jnp.stack(mk_p), jnp.stack(mv_p), jnp.stack(ckv_s), jnp.stack(kpe_s), jnp.stack(gla_s))
```
